```python
import math
import jax, jax.numpy as jnp
from jax import lax
import numpy as np

D_MODEL = 2048
BATCH = 2
SEQ = 4096
DEPTH = 4
DEC_BATCH = 8
DEC_SEQ = 4
PAST_LEN = 16384
PAGE_SIZE = 128

N_A_LAYERS = DEPTH // 2
N_B_LAYERS = DEPTH - N_A_LAYERS
D_RNN = 3 * D_MODEL // 4
RG_BLOCK = 128
RG_HEADS = D_RNN // RG_BLOCK
RG_CONV = 4
RG_C = 8.0
SB_HEAD_DIM = 128
SB_HEADS = D_RNN // SB_HEAD_DIM
D_SB = SB_HEADS * SB_HEAD_DIM
SB_BIAS_INIT = -8.0
MEM_TOKENS = 256
MEM_HEADS = 4
D_MEM = D_MODEL // 4
MEM_HEAD_DIM = D_MEM // MEM_HEADS
D_MIX = D_RNN + D_MEM
D_FF = 3 * D_MODEL
FFN_CONV = 3
Q_BLOCK = 128
EPS = 1e-6
POOL_NUM = 5
POOL_DEN = 4

kernel_name = 'yoco_rglru_stickbreaking_memxattn_convffn_step'


def rmsnorm(x, g):
    xf = x.astype(jnp.float32)
    y = xf * lax.rsqrt(jnp.mean(xf * xf, axis=-1, keepdims=True) + EPS)
    return (y * g.astype(jnp.float32)).astype(x.dtype)


def causal_dwconv(x, buf, w, b):
    width = w.shape[0]
    t = x.shape[1]
    xp = jnp.concatenate([buf.astype(x.dtype), x], axis=1)
    y = b + xp[:, 0:t] * w[0]
    for j in range(1, width):
        y = y + xp[:, j:j + t] * w[j]
    return y, xp[:, xp.shape[1] - (width - 1):]


def rglru(xc, h0, w_gx, b_gx, w_ga, b_ga, lam):
    b, t, c = xc.shape
    xh = xc.reshape(b, t, RG_HEADS, RG_BLOCK)
    gx = jax.nn.sigmoid(jnp.einsum('bthi,hij->bthj', xh, w_gx).reshape(b, t, c) + b_gx)
    ga = jax.nn.sigmoid(jnp.einsum('bthi,hij->bthj', xh, w_ga).reshape(b, t, c) + b_ga)
    log_a = -RG_C * ga.astype(jnp.float32) * jax.nn.softplus(-lam.astype(jnp.float32))
    a = jnp.exp(log_a)
    mult = jnp.sqrt(-jnp.expm1(2.0 * log_a))
    u = mult * (gx * xc).astype(jnp.float32)
    u = u.at[:, 0].add(a[:, 0] * h0.astype(jnp.float32))

    def combine(lhs, rhs):
        return (lhs[0] * rhs[0], rhs[0] * lhs[1] + rhs[1])

    _, h = lax.associative_scan(combine, (a, u), axis=1)
    return h.astype(xc.dtype), h[:, -1]


def sb_block(q, k, v, q_pos, k_pos, bias):
    z = (jnp.einsum('bqhd,bkhd->bhqk', q, k).astype(jnp.float32) * (SB_HEAD_DIM ** -0.5)
         + bias.astype(jnp.float32)[None, :, None, None])
    mask = (k_pos[None, :] < q_pos[:, None])[None, None]
    log_beta = jax.nn.log_sigmoid(z)
    log_keep = jnp.where(mask, log_beta - z, 0.0)
    later = lax.cumsum(log_keep, axis=3, reverse=True) - log_keep
    w = jnp.where(mask, jnp.exp(log_beta + later), 0.0)
    return jnp.einsum('bhqk,bkhd->bqhd', w.astype(v.dtype), v)


def stick_breaking(q, k, v, past_len, bias):
    b, t, h, dh = q.shape
    k_pos = jnp.arange(k.shape[1], dtype=jnp.int32)
    q_pos = past_len + jnp.arange(t, dtype=jnp.int32)
    if t > Q_BLOCK and t % Q_BLOCK == 0:
        n = t // Q_BLOCK
        qb = q.reshape(b, n, Q_BLOCK, h, dh).transpose(1, 0, 2, 3, 4)
        pb = q_pos.reshape(n, Q_BLOCK)
        ob = lax.map(lambda a: sb_block(a[0], k, v, a[1], k_pos, bias), (qb, pb))
        return ob.transpose(1, 0, 2, 3, 4).reshape(b, t, h, dh)
    return sb_block(q, k, v, q_pos, k_pos, bias)


def mem_kv(mem, g, w, g_k):
    b, m, _ = mem.shape
    kv = rmsnorm(mem, g) @ w
    k, v = jnp.split(kv, 2, axis=-1)
    k = rmsnorm(k.reshape(b, m, MEM_HEADS, MEM_HEAD_DIM), g_k)
    return k, v.reshape(b, m, MEM_HEADS, MEM_HEAD_DIM)


def mem_attend(qm, mk, mv, g_q):
    b, t, _ = qm.shape
    q = rmsnorm(qm.reshape(b, t, MEM_HEADS, MEM_HEAD_DIM), g_q)
    s = jnp.einsum('bthd,bmhd->bhtm', q, mk.astype(q.dtype)).astype(jnp.float32) * (MEM_HEAD_DIM ** -0.5)
    p = jax.nn.softmax(s, axis=-1).astype(q.dtype)
    return jnp.einsum('bhtm,bmhd->bthd', p, mv.astype(q.dtype)).reshape(b, t, D_MEM)


def conv_ffn(h, buf, w_up, cw, cb, w_down):
    u, g = jnp.split(h @ w_up, 2, axis=-1)
    u, new_buf = causal_dwconv(u, buf, cw, cb)
    return (jax.nn.gelu(u) * g) @ w_down, new_buf


def trunk(x, past_len, rg_h0, rg_conv0, ffn_conv0, mem_k, mem_v, k_past, v_past, P):
    b, t, _ = x.shape
    rg_h_out, rg_conv_out, ffn_conv_out = [], [], []
    k_all = v_all = k_new = v_new = None
    for l in range(DEPTH):
        h = rmsnorm(x, P['g_mix'][l])
        if l < N_A_LAYERS:
            p = h @ P['w_in_a'][l]
            xb, gb, qm = jnp.split(p, [D_RNN, 2 * D_RNN], axis=-1)
            xc, cbuf = causal_dwconv(xb, rg_conv0[l], P['rg_conv_w'][l], P['rg_conv_b'][l])
            y_rec, h_last = rglru(xc, rg_h0[l], P['rg_gate_x_w'][l], P['rg_gate_x_b'][l],
                                  P['rg_gate_a_w'][l], P['rg_gate_a_b'][l], P['rg_lambda'][l])
            y_tok = y_rec * jax.nn.gelu(gb)
            rg_h_out.append(h_last)
            rg_conv_out.append(cbuf)
        else:
            j = l - N_A_LAYERS
            p = h @ P['w_in_b'][j]
            qs, qm = jnp.split(p, [D_SB], axis=-1)
            q = rmsnorm(qs.reshape(b, t, SB_HEADS, SB_HEAD_DIM), P['sb_q_norm'][j])
            y_tok = stick_breaking(q, k_all, v_all, past_len, P['sb_beta_bias'][j]).reshape(b, t, D_SB)
        y_mem = mem_attend(qm, mem_k[l], mem_v[l], P['mem_q_norm'][l])
        x = x + jnp.concatenate([y_tok, y_mem], axis=-1) @ P['w_out'][l]
        f, fbuf = conv_ffn(rmsnorm(x, P['g_ffn'][l]), ffn_conv0[l], P['w_ffn_up'][l],
                           P['ffn_conv_w'][l], P['ffn_conv_b'][l], P['w_ffn_down'][l])
        x = x + f
        ffn_conv_out.append(fbuf)
        if l == N_A_LAYERS - 1:
            kv = rmsnorm(x, P['kv_norm']) @ P['w_kv']
            k_new, v_new = jnp.split(kv, 2, axis=-1)
            k_new = rmsnorm(k_new.reshape(b, t, SB_HEADS, SB_HEAD_DIM), P['kv_k_norm'])
            v_new = v_new.reshape(b, t, SB_HEADS, SB_HEAD_DIM)
            k_all = jnp.concatenate([k_past.astype(k_new.dtype), k_new], axis=1)
            v_all = jnp.concatenate([v_past.astype(v_new.dtype), v_new], axis=1)
    return (x, jnp.stack(rg_h_out), jnp.stack(rg_conv_out), jnp.stack(ffn_conv_out), k_new, v_new)


def setup_inputs(seed: int = 0) -> dict:
    key = jax.random.key(seed)
    keys = jax.random.split(key, 48)
    ctr = [0]

    def nk():
        ctr[0] += 1
        return keys[ctr[0] - 1]

    def nrm(shape, scale=1.0):
        return jax.random.normal(nk(), shape, jnp.float32) * scale

    def gain(shape):
        return 1.0 + 0.02 * nrm(shape)

    d = D_MODEL
    n_pages = PAST_LEN // PAGE_SIZE
    n_used = DEC_BATCH * n_pages
    n_phys = (n_used * POOL_NUM) // POOL_DEN
    x_prompt = nrm((BATCH, SEQ, d))
    x_sample = nrm((DEC_BATCH, DEC_SEQ, d))
    mem_prompt = nrm((BATCH, MEM_TOKENS, d))
    state_rglru_h = nrm((N_A_LAYERS, DEC_BATCH, D_RNN), 0.5)
    state_rglru_conv = nrm((N_A_LAYERS, DEC_BATCH, RG_CONV - 1, D_RNN))
    state_ffn_conv = nrm((DEPTH, DEC_BATCH, FFN_CONV - 1, D_FF))
    cache_mem_k = nrm((DEPTH, DEC_BATCH, MEM_TOKENS, MEM_HEADS, MEM_HEAD_DIM))
    cache_mem_v = nrm((DEPTH, DEC_BATCH, MEM_TOKENS, MEM_HEADS, MEM_HEAD_DIM))
    cache_sb_k = nrm((n_phys, PAGE_SIZE, SB_HEADS, SB_HEAD_DIM))
    cache_sb_v = nrm((n_phys, PAGE_SIZE, SB_HEADS, SB_HEAD_DIM))
    perm = jax.random.permutation(nk(), n_phys)
    page_table = perm[:n_used].reshape(DEC_BATCH, n_pages).astype(jnp.int32)
    u = jax.random.uniform(nk(), (N_A_LAYERS, D_RNN), jnp.float32, 0.9, 0.999)
    a0 = u ** (1.0 / RG_C)
    rg_lambda = jnp.log(a0) - jnp.log1p(-a0)
    return {
        'x_prompt': x_prompt,
        'x_sample': x_sample,
        'mem_prompt': mem_prompt,
        'state_rglru_h': state_rglru_h,
        'state_rglru_conv': state_rglru_conv,
        'state_ffn_conv': state_ffn_conv,
        'cache_mem_k': cache_mem_k,
        'cache_mem_v': cache_mem_v,
        'cache_sb_k': cache_sb_k,
        'cache_sb_v': cache_sb_v,
        'page_table': page_table,
        'g_mix': gain((DEPTH, d)),
        'g_ffn': gain((DEPTH, d)),
        'w_in_a': nrm((N_A_LAYERS, d, 2 * D_RNN + D_MEM), d ** -0.5),
        'rg_conv_w': nrm((N_A_LAYERS, RG_CONV, D_RNN), RG_CONV ** -0.5),
        'rg_conv_b': nrm((N_A_LAYERS, D_RNN), 0.01),
        'rg_gate_x_w': nrm((N_A_LAYERS, RG_HEADS, RG_BLOCK, RG_BLOCK), RG_BLOCK ** -0.5),
        'rg_gate_x_b': nrm((N_A_LAYERS, D_RNN), 0.01),
        'rg_gate_a_w': nrm((N_A_LAYERS, RG_HEADS, RG_BLOCK, RG_BLOCK), RG_BLOCK ** -0.5),
        'rg_gate_a_b': nrm((N_A_LAYERS, D_RNN), 0.01),
        'rg_lambda': rg_lambda,
        'w_in_b': nrm((N_B_LAYERS, d, D_SB + D_MEM), d ** -0.5),
        'sb_q_norm': gain((N_B_LAYERS, SB_HEAD_DIM)),
        'sb_beta_bias': SB_BIAS_INIT + 0.1 * nrm((N_B_LAYERS, SB_HEADS)),
        'kv_norm': gain((d,)),
        'w_kv': nrm((d, 2 * D_SB), d ** -0.5),
        'kv_k_norm': gain((SB_HEAD_DIM,)),
        'mem_norm': gain((DEPTH, d)),
        'w_mem_kv': nrm((DEPTH, d, 2 * D_MEM), d ** -0.5),
        'mem_q_norm': gain((DEPTH, MEM_HEAD_DIM)),
        'mem_k_norm': gain((DEPTH, MEM_HEAD_DIM)),
        'w_out': nrm((DEPTH, D_MIX, d), D_MIX ** -0.5),
        'w_ffn_up': nrm((DEPTH, d, 2 * D_FF), d ** -0.5),
        'ffn_conv_w': nrm((DEPTH, FFN_CONV, D_FF), FFN_CONV ** -0.5),
        'ffn_conv_b': nrm((DEPTH, D_FF), 0.01),
        'w_ffn_down': nrm((DEPTH, D_FF, d), D_FF ** -0.5),
    }


def reference(x_prompt, x_sample, mem_prompt, state_rglru_h, state_rglru_conv, state_ffn_conv,
              cache_mem_k, cache_mem_v, cache_sb_k, cache_sb_v, page_table,
              g_mix, g_ffn, w_in_a, rg_conv_w, rg_conv_b, rg_gate_x_w, rg_gate_x_b,
              rg_gate_a_w, rg_gate_a_b, rg_lambda, w_in_b, sb_q_norm, sb_beta_bias, kv_norm, w_kv,
              kv_k_norm, mem_norm, w_mem_kv, mem_q_norm, mem_k_norm, w_out, w_ffn_up, ffn_conv_w,
              ffn_conv_b, w_ffn_down):
    P = {'g_mix': g_mix, 'g_ffn': g_ffn, 'w_in_a': w_in_a, 'rg_conv_w': rg_conv_w,
         'rg_conv_b': rg_conv_b, 'rg_gate_x_w': rg_gate_x_w, 'rg_gate_x_b': rg_gate_x_b,
         'rg_gate_a_w': rg_gate_a_w, 'rg_gate_a_b': rg_gate_a_b, 'rg_lambda': rg_lambda,
         'w_in_b': w_in_b, 'sb_q_norm': sb_q_norm, 'sb_beta_bias': sb_beta_bias,
         'kv_norm': kv_norm, 'w_kv': w_kv, 'kv_k_norm': kv_k_norm, 'mem_q_norm': mem_q_norm,
         'w_out': w_out, 'w_ffn_up': w_ffn_up, 'ffn_conv_w': ffn_conv_w,
         'ffn_conv_b': ffn_conv_b, 'w_ffn_down': w_ffn_down}

    bp = x_prompt.shape[0]
    mk_list, mv_list = [], []
    for l in range(DEPTH):
        mk, mv = mem_kv(mem_prompt, mem_norm[l], w_mem_kv[l], mem_k_norm[l])
        mk_list.append(mk)
        mv_list.append(mv)
    prompt_mem_k = jnp.stack(mk_list)
    prompt_mem_v = jnp.stack(mv_list)
    zeros_h = jnp.zeros((N_A_LAYERS, bp, D_RNN), jnp.float32)
    zeros_rc = jnp.zeros((N_A_LAYERS, bp, RG_CONV - 1, D_RNN), x_prompt.dtype)
    zeros_fc = jnp.zeros((DEPTH, bp, FFN_CONV - 1, D_FF), x_prompt.dtype)
    empty_kv = jnp.zeros((bp, 0, SB_HEADS, SB_HEAD_DIM), x_prompt.dtype)
    (y_prompt, prompt_rglru_h, prompt_rglru_conv, prompt_ffn_conv,
     prompt_sb_k, prompt_sb_v) = trunk(x_prompt, 0, zeros_h, zeros_rc, zeros_fc,
                                       prompt_mem_k, prompt_mem_v, empty_kv, empty_kv, P)

    db, n_pages = page_table.shape
    past_len = n_pages * cache_sb_k.shape[1]
    k_past = cache_sb_k[page_table].reshape(db, past_len, SB_HEADS, SB_HEAD_DIM)
    v_past = cache_sb_v[page_table].reshape(db, past_len, SB_HEADS, SB_HEAD_DIM)
    (y_sample, sample_rglru_h, sample_rglru_conv, sample_ffn_conv,
     sample_sb_k, sample_sb_v) = trunk(x_sample, past_len, state_rglru_h, state_rglru_conv,
                                       state_ffn_conv, cache_mem_k, cache_mem_v, k_past, v_past, P)

    return (y_prompt, y_sample,
            prompt_rglru_h, prompt_rglru_conv, prompt_ffn_conv,
            prompt_sb_k, prompt_sb_v, prompt_mem_k, prompt_mem_v,
            sample_rglru_h, sample_rglru_conv, sample_ffn_conv,
            sample_sb_k, sample_sb_v)
```

```python
import functools
import math

import jax
import jax.numpy as jnp
from jax import lax
from jax.experimental import pallas as pl
from jax.experimental.pallas import tpu as pltpu

F32 = jnp.float32
BF16 = jnp.bfloat16

LANE = 128
SUBLANE = 8
VMEM_LIMIT_BYTES = 56 * 1024 * 1024

D_MODEL = 2048
DEPTH = 4
N_A_LAYERS = 2
D_RNN = 1536
HEAD = 128
RG_HEADS = D_RNN // HEAD
RG_CONV = 4
RG_C = 8.0
SB_HEADS = D_RNN // HEAD
MEM_TOKENS = 256
MEM_HEADS = 4
D_MEM = MEM_HEADS * HEAD
D_FF = 3 * D_MODEL
FFN_CONV = 3
EPS = 1e-6
ATTN_SCALE = HEAD ** -0.5
SAMPLE_T_PAD = 16
Q_ROWS = 8
HEAD_SLOTS = -(-SB_HEADS // SUBLANE) * SUBLANE


def _params(*sem):
    return pltpu.CompilerParams(dimension_semantics=sem, vmem_limit_bytes=VMEM_LIMIT_BYTES)


def _tile(n, pref, mult):
    if n <= pref:
        return n
    t = (pref // mult) * mult
    while t > mult and n % t:
        t -= mult
    assert n % t == 0, (n, pref, mult)
    return t


def _softplus(z):
    return jnp.maximum(z, 0.0) + jnp.log1p(jnp.exp(-jnp.abs(z)))


def _head_rmsnorm(blk, gain):
    ms = jnp.mean(blk * blk, axis=-1, keepdims=True)
    return blk * lax.rsqrt(ms + EPS) * gain


def _norm_matmul_kernel(*refs, head_norm, n_out):
    x_ref, g_ref, w_ref = refs[:3]
    pos = 3
    hg_ref = None
    if head_norm:
        hg_ref = refs[pos]
        pos += 1
    o_refs = refs[pos:pos + n_out]
    xn_ref = refs[pos + n_out]

    @pl.when(pl.program_id(1) == 0)
    def _():
        x = x_ref[...]
        ms = jnp.mean(x * x, axis=-1, keepdims=True)
        xn_ref[...] = (x * lax.rsqrt(ms + EPS) * g_ref[...]).astype(BF16)

    acc = jnp.dot(xn_ref[...], w_ref[...], preferred_element_type=F32)
    tn = acc.shape[1]
    if head_norm:
        for h in range(tn // HEAD):
            cs = slice(h * HEAD, (h + 1) * HEAD)
            y = _head_rmsnorm(acc[:, cs], hg_ref[:, cs])
            for o_ref in o_refs:
                o_ref[:, cs] = y.astype(o_ref.dtype)
    else:
        for o_ref in o_refs:
            o_ref[...] = acc.astype(o_ref.dtype)


def norm_matmul(x, g, w, *, head_gain=None, out_dtypes=(BF16,), tm_pref=1024, tn_pref=1024):
    m, d = x.shape
    n = w.shape[1]
    tm = _tile(m, tm_pref, SUBLANE)
    tn = _tile(n, tn_pref, 2 * LANE)
    in_specs = [
        pl.BlockSpec((tm, d), lambda i, j: (i, 0)),
        pl.BlockSpec((1, d), lambda i, j: (0, 0)),
        pl.BlockSpec((d, tn), lambda i, j: (0, j)),
    ]
    args = [x, g.reshape(1, d).astype(F32), w]
    if head_gain is not None:
        in_specs.append(pl.BlockSpec((1, tn), lambda i, j: (0, j)))
        args.append(head_gain.reshape(1, n).astype(F32))
    outs = pl.pallas_call(
        functools.partial(_norm_matmul_kernel, head_norm=head_gain is not None, n_out=len(out_dtypes)),
        grid=(m // tm, n // tn),
        in_specs=in_specs,
        out_specs=[pl.BlockSpec((tm, tn), lambda i, j: (i, j)) for _ in out_dtypes],
        out_shape=[jax.ShapeDtypeStruct((m, n), dt) for dt in out_dtypes],
        scratch_shapes=[pltpu.VMEM((tm, d), BF16)],
        compiler_params=_params("parallel", "arbitrary"),
        name="norm_matmul",
    )(*args)
    return outs[0] if len(out_dtypes) == 1 else tuple(outs)


def _matmul_residual_kernel(*refs, n_pairs):
    x_ref = refs[2 * n_pairs]
    o_ref = refs[2 * n_pairs + 1]

    @pl.when(pl.program_id(2) == 0)
    def _():
        o_ref[...] = x_ref[...]

    acc = jnp.dot(refs[0][...], refs[1][...], preferred_element_type=F32)
    for p in range(1, n_pairs):
        acc += jnp.dot(refs[2 * p][...], refs[2 * p + 1][...], preferred_element_type=F32)
    o_ref[...] += acc


def matmul_residual(pairs, x, *, n_k=1, tm_pref=1024, tn_pref=1024):
    m, n = x.shape
    tm = _tile(m, tm_pref, SUBLANE)
    tn = _tile(n, tn_pref, 2 * LANE)
    in_specs, args = [], []
    for a, w in pairs:
        kp = a.shape[1]
        assert kp % n_k == 0 and w.shape == (kp, n)
        tk = kp // n_k
        in_specs.append(pl.BlockSpec((tm, tk), lambda i, j, k: (i, k)))
        in_specs.append(pl.BlockSpec((tk, tn), lambda i, j, k: (k, j)))
        args += [a, w]
    in_specs.append(pl.BlockSpec((tm, tn), lambda i, j, k: (i, j)))
    args.append(x)
    return pl.pallas_call(
        functools.partial(_matmul_residual_kernel, n_pairs=len(pairs)),
        grid=(m // tm, n // tn, n_k),
        in_specs=in_specs,
        out_specs=pl.BlockSpec((tm, tn), lambda i, j, k: (i, j)),
        out_shape=jax.ShapeDtypeStruct((m, n), F32),
        compiler_params=_params("parallel", "parallel", "arbitrary"),
        name="matmul_residual",
    )(*args)


def _rglru_kernel(p_ref, h0_ref, cb0_ref, cw_ref, cbias_ref, wgx_ref, bgx_ref, wga_ref, bga_ref, lam_ref,
                  y_ref, hl_ref, cbo_ref, ext_ref, hc_ref, *, tr, tile_last, r_last):
    t = pl.program_id(1)
    halo = SUBLANE

    @pl.when(t == 0)
    def _():
        ext_ref[0:halo, :] = jnp.zeros((halo, D_RNN), F32)
        ext_ref[halo - (RG_CONV - 1):halo, :] = cb0_ref[0]
        hc_ref[...] = h0_ref[0]

    ext_ref[halo:halo + tr, :] = p_ref[0, :, 0:D_RNN].astype(F32)
    row = lax.broadcasted_iota(jnp.int32, (tr, HEAD), 0)
    for h in range(RG_HEADS):
        cs = slice(h * HEAD, (h + 1) * HEAD)
        xc = cbias_ref[:, cs] + ext_ref[halo - 3:halo - 3 + tr, cs] * cw_ref[0:1, cs]
        for j in range(1, RG_CONV):
            xc = xc + ext_ref[halo - 3 + j:halo - 3 + j + tr, cs] * cw_ref[j:j + 1, cs]
        xcb = xc.astype(BF16)
        gx = jax.nn.sigmoid(jnp.dot(xcb, wgx_ref[h], preferred_element_type=F32) + bgx_ref[:, cs])
        ga = jax.nn.sigmoid(jnp.dot(xcb, wga_ref[h], preferred_element_type=F32) + bga_ref[:, cs])
        log_a = (-RG_C) * ga * _softplus(-lam_ref[:, cs])
        a = jnp.exp(log_a)
        u = jnp.sqrt(-jnp.tanh(log_a) * (1.0 + a * a)) * (gx * xc)
        u = u + jnp.where(row == 0, a * hc_ref[:, cs], 0.0)
        d = 1
        while d < tr:
            a_prev = pltpu.roll(a, d, 0)
            u_prev = pltpu.roll(u, d, 0)
            live = row >= d
            u = jnp.where(live, a * u_prev + u, u)
            a = jnp.where(live, a * a_prev, a)
            d *= 2
        gate = jax.nn.gelu(p_ref[0, :, D_RNN + h * HEAD:D_RNN + (h + 1) * HEAD].astype(F32))
        y_ref[0, :, cs] = (u * gate).astype(y_ref.dtype)
        hc_ref[:, cs] = u[tr - 1:tr, :]

        @pl.when(t == tile_last)
        def _():
            hl_ref[0, :, cs] = u[r_last:r_last + 1, :]

    @pl.when(t == tile_last)
    def _():
        cbo_ref[0] = ext_ref[halo + r_last - 2:halo + r_last + 1, :]

    if tr >= halo:
        ext_ref[0:halo, :] = ext_ref[tr:tr + halo, :]


def rglru(p, h0, cb0, cw, cbias, wgx, bgx, wga, bga, lam, *, t_valid):
    b, t, _ = p.shape
    tr = _tile(t, 256, 16)
    row = lambda v: v.reshape(1, D_RNN).astype(F32)
    full = lambda shape: pl.BlockSpec(shape, lambda i, j: (0,) * len(shape))
    y, hl, cbo = pl.pallas_call(
        functools.partial(_rglru_kernel, tr=tr, tile_last=(t_valid - 1) // tr, r_last=(t_valid - 1) % tr),
        grid=(b, t // tr),
        in_specs=[
            pl.BlockSpec((1, tr, 2 * D_RNN), lambda i, j: (i, j, 0)),
            pl.BlockSpec((1, 1, D_RNN), lambda i, j: (i, 0, 0)),
            pl.BlockSpec((1, RG_CONV - 1, D_RNN), lambda i, j: (i, 0, 0)),
            full((RG_CONV, D_RNN)), full((1, D_RNN)),
            full((RG_HEADS, HEAD, HEAD)), full((1, D_RNN)),
            full((RG_HEADS, HEAD, HEAD)), full((1, D_RNN)),
            full((1, D_RNN)),
        ],
        out_specs=[
            pl.BlockSpec((1, tr, D_RNN), lambda i, j: (i, j, 0)),
            pl.BlockSpec((1, 1, D_RNN), lambda i, j: (i, 0, 0)),
            pl.BlockSpec((1, RG_CONV - 1, D_RNN), lambda i, j: (i, 0, 0)),
        ],
        out_shape=[
            jax.ShapeDtypeStruct((b, t, D_RNN), BF16),
            jax.ShapeDtypeStruct((b, 1, D_RNN), F32),
            jax.ShapeDtypeStruct((b, RG_CONV - 1, D_RNN), F32),
        ],
        scratch_shapes=[pltpu.VMEM((tr + SUBLANE, D_RNN), F32), pltpu.VMEM((1, D_RNN), F32)],
        compiler_params=_params("parallel", "arbitrary"),
        name="rglru",
    )(p, h0.reshape(b, 1, D_RNN), cb0, cw.astype(F32), row(cbias), wgx, row(bgx), wga, row(bga), row(lam))
    return y, hl.reshape(b, D_RNN), cbo


def _mem_attn_kernel(q_ref, k_ref, v_ref, o_ref):
    for h in range(MEM_HEADS):
        cs = slice(h * HEAD, (h + 1) * HEAD)
        q = q_ref[0, :, cs]
        k = k_ref[0, :, cs].astype(BF16)
        v = v_ref[0, :, cs].astype(BF16)
        s = lax.dot_general(q, k, (((1,), (1,)), ((), ())), preferred_element_type=F32) * ATTN_SCALE
        e = jnp.exp(s - jnp.max(s, axis=-1, keepdims=True))
        p = e / jnp.sum(e, axis=-1, keepdims=True)
        o_ref[0, :, cs] = jnp.dot(p.astype(BF16), v, preferred_element_type=F32).astype(o_ref.dtype)


def mem_attn(q, q_col_block, mk, mv):
    b, t, _ = q.shape
    tq = _tile(t, 512, 16)
    return pl.pallas_call(
        _mem_attn_kernel,
        grid=(b, t // tq),
        in_specs=[
            pl.BlockSpec((1, tq, D_MEM), lambda i, j: (i, j, q_col_block)),
            pl.BlockSpec((1, MEM_TOKENS, D_MEM), lambda i, j: (i, 0, 0)),
            pl.BlockSpec((1, MEM_TOKENS, D_MEM), lambda i, j: (i, 0, 0)),
        ],
        out_specs=pl.BlockSpec((1, tq, D_MEM), lambda i, j: (i, j, 0)),
        out_shape=jax.ShapeDtypeStruct((b, t, D_MEM), BF16),
        compiler_params=_params("parallel", "parallel"),
        name="mem_attn",
    )(q, mk, mv)


def _ffn_act_kernel(u_ref, g_ref, b0_ref, cw_ref, cb_ref, act_ref, nb_ref, ext_ref, *, tr, tile_last, r_last):
    t = pl.program_id(2)
    halo = SUBLANE
    tc = ext_ref.shape[1]

    @pl.when(t == 0)
    def _():
        ext_ref[0:halo, :] = jnp.zeros((halo, tc), F32)
        ext_ref[halo - (FFN_CONV - 1):halo, :] = b0_ref[0]

    ext_ref[halo:halo + tr, :] = u_ref[0].astype(F32)
    uc = cb_ref[...] + ext_ref[halo - 2:halo - 2 + tr, :] * cw_ref[0:1, :]
    for j in range(1, FFN_CONV):
        uc = uc + ext_ref[halo - 2 + j:halo - 2 + j + tr, :] * cw_ref[j:j + 1, :]
    act_ref[0] = (jax.nn.gelu(uc) * g_ref[0].astype(F32)).astype(act_ref.dtype)

    @pl.when(t == tile_last)
    def _():
        nb_ref[0] = ext_ref[halo + r_last - 1:halo + r_last + 1, :]

    if tr >= halo:
        ext_ref[0:halo, :] = ext_ref[tr:tr + halo, :]


def ffn_act(up, buf0, cw, cb, *, t_valid):
    b, t, _ = up.shape
    tr = _tile(t, 256, 16)
    tc = 1536
    n_c = D_FF // tc
    act, nb = pl.pallas_call(
        functools.partial(_ffn_act_kernel, tr=tr, tile_last=(t_valid - 1) // tr, r_last=(t_valid - 1) % tr),
        grid=(b, n_c, t // tr),
        in_specs=[
            pl.BlockSpec((1, tr, tc), lambda i, c, j: (i, j, c)),
            pl.BlockSpec((1, tr, tc), lambda i, c, j: (i, j, n_c + c)),
            pl.BlockSpec((1, FFN_CONV - 1, tc), lambda i, c, j: (i, 0, c)),
            pl.BlockSpec((FFN_CONV, tc), lambda i, c, j: (0, c)),
            pl.BlockSpec((1, tc), lambda i, c, j: (0, c)),
        ],
        out_specs=[
            pl.BlockSpec((1, tr, tc), lambda i, c, j: (i, j, c)),
            pl.BlockSpec((1, FFN_CONV - 1, tc), lambda i, c, j: (i, 0, c)),
        ],
        out_shape=[
            jax.ShapeDtypeStruct((b, t, D_FF), BF16),
            jax.ShapeDtypeStruct((b, FFN_CONV - 1, D_FF), F32),
        ],
        scratch_shapes=[pltpu.VMEM((tr + SUBLANE, tc), F32)],
        compiler_params=_params("parallel", "parallel", "arbitrary"),
        name="ffn_act",
    )(up, up, buf0, cw.astype(F32), cb.reshape(1, D_FF).astype(F32))
    return act, nb


def _suffix_matrix():
    j = lax.broadcasted_iota(jnp.int32, (2 * HEAD, 2 * HEAD), 0) % HEAD
    s = lax.broadcasted_iota(jnp.int32, (2 * HEAD, 2 * HEAD), 1)
    return jnp.where((s >= HEAD) | (j > s), 1.0, 0.0).astype(BF16)


def _sb_weights(s, bias, carry, suffix, mask):
    z = s * ATTN_SCALE + bias
    sp = _softplus(z)
    lk = -sp
    if mask is not None:
        lk = jnp.where(mask, lk, 0.0)
    hi = lk.astype(BF16)
    lo = (lk - hi.astype(F32)).astype(BF16)
    c2 = jnp.dot(jnp.concatenate([hi, lo], axis=1), suffix, preferred_element_type=F32)
    w = jnp.exp((z - sp) + (carry + c2[:, :HEAD]))
    if mask is not None:
        w = jnp.where(mask, w, 0.0)
    return w, carry + c2[:, HEAD:]


def _nt_dot(a, b):
    return lax.dot_general(a, b, (((1,), (1,)), ((), ())), preferred_element_type=F32)


def _sb_block(q, k, v, bias, carry, suffix, mask):
    w, carry = _sb_weights(_nt_dot(q, k), bias, carry, suffix, mask)
    return jnp.dot(w.astype(BF16), v, preferred_element_type=F32), carry


def _sb_prompt_kernel(bias_ref, q_ref, k_ref, v_ref, o_ref, *, tq):
    h = pl.program_id(1)
    qi = pl.program_id(2)
    bias = bias_ref[h]
    q = q_ref[0]
    suffix = _suffix_matrix()
    n_sub = tq // HEAD
    t_loc = lax.broadcasted_iota(jnp.int32, (tq, HEAD), 0)
    s_loc = lax.broadcasted_iota(jnp.int32, (tq, HEAD), 1)
    carry = jnp.zeros((tq, HEAD), F32)
    acc = jnp.zeros((tq, HEAD), F32)
    for sub in range(n_sub - 1, -1, -1):
        start = pl.multiple_of(qi * tq + sub * HEAD, HEAD)
        mask = (s_loc + sub * HEAD) < t_loc
        pv, carry = _sb_block(q, k_ref[0, pl.ds(start, HEAD), :], v_ref[0, pl.ds(start, HEAD), :],
                              bias, carry, suffix, mask)
        acc = acc + pv

    def body(i, state):
        carry, acc = state
        kb = qi * n_sub - 1 - i
        start = pl.multiple_of(kb * HEAD, HEAD)
        pv, carry = _sb_block(q, k_ref[0, pl.ds(start, HEAD), :], v_ref[0, pl.ds(start, HEAD), :],
                              bias, carry, suffix, None)
        return carry, acc + pv

    carry, acc = lax.fori_loop(0, qi * n_sub, body, (carry, acc))
    o_ref[0] = acc.astype(o_ref.dtype)


def sb_prompt(q, k, v, bias):
    b, t, _ = k.shape
    tq = _tile(t, 256, HEAD)
    return pl.pallas_call(
        functools.partial(_sb_prompt_kernel, tq=tq),
        grid_spec=pltpu.PrefetchScalarGridSpec(
            num_scalar_prefetch=1,
            grid=(b, SB_HEADS, t // tq),
            in_specs=[
                pl.BlockSpec((1, tq, HEAD), lambda i, h, j, bias: (i, j, h)),
                pl.BlockSpec((1, t, HEAD), lambda i, h, j, bias: (i, 0, h)),
                pl.BlockSpec((1, t, HEAD), lambda i, h, j, bias: (i, 0, h)),
            ],
            out_specs=pl.BlockSpec((1, tq, HEAD), lambda i, h, j, bias: (i, j, h)),
        ),
        out_shape=jax.ShapeDtypeStruct((b, t, D_RNN), BF16),
        compiler_params=_params("parallel", "parallel", "arbitrary"),
        name="sb_prompt",
    )(bias.astype(F32), q, k, v)


def _sb_sample_kernel(pt_ref, q_ref, kn_ref, vn_ref, bias_ref, *rest, pages_per_step, t_valid):
    kp_refs = rest[:pages_per_step]
    vp_refs = rest[pages_per_step:2 * pages_per_step]
    o_ref, carry_ref, acc_ref = rest[2 * pages_per_step:]
    g = pl.program_id(1)
    rows = SB_HEADS * Q_ROWS
    suffix = _suffix_matrix()
    bias = bias_ref[...]
    hcols = [slice(h * HEAD, (h + 1) * HEAD) for h in range(SB_HEADS)]

    def update(k_heads, v_heads, mask):
        s = jnp.concatenate([_nt_dot(q_ref[0, :, hcols[h]], k_heads[h])[0:Q_ROWS] for h in range(SB_HEADS)],
                            axis=0)
        w, carry = _sb_weights(s, bias, carry_ref[...], suffix, mask)
        carry_ref[...] = carry
        zpad = jnp.zeros((SAMPLE_T_PAD - Q_ROWS, HEAD), F32)
        for h in range(SB_HEADS):
            rs = slice(h * Q_ROWS, (h + 1) * Q_ROWS)
            wh = jnp.concatenate([w[rs], zpad], axis=0).astype(BF16)
            acc_ref[rs, :] += jnp.dot(wh, v_heads[h], preferred_element_type=F32)[0:Q_ROWS]

    @pl.when(g == 0)
    def _():
        carry_ref[...] = jnp.zeros((rows, HEAD), F32)
        acc_ref[...] = jnp.zeros((rows, HEAD), F32)
        kz = jnp.zeros((HEAD - SAMPLE_T_PAD, HEAD), BF16)
        k_heads = [jnp.concatenate([kn_ref[0, :, hcols[h]], kz], axis=0) for h in range(SB_HEADS)]
        v_heads = [jnp.concatenate([vn_ref[0, :, hcols[h]], kz], axis=0) for h in range(SB_HEADS)]
        i_q = lax.broadcasted_iota(jnp.int32, (rows, HEAD), 0) % Q_ROWS
        j_k = lax.broadcasted_iota(jnp.int32, (rows, HEAD), 1)
        update(k_heads, v_heads, (j_k < i_q) & (j_k < t_valid))

    for r in range(pages_per_step):
        kp = kp_refs[r].reshape(HEAD * HEAD_SLOTS, HEAD)
        vp = vp_refs[r].reshape(HEAD * HEAD_SLOTS, HEAD)
        k_heads = [kp[pl.ds(h, HEAD, stride=HEAD_SLOTS), :].astype(BF16) for h in range(SB_HEADS)]
        v_heads = [vp[pl.ds(h, HEAD, stride=HEAD_SLOTS), :].astype(BF16) for h in range(SB_HEADS)]
        update(k_heads, v_heads, None)

    @pl.when(g == pl.num_programs(1) - 1)
    def _():
        z8 = jnp.zeros((SAMPLE_T_PAD - Q_ROWS, HEAD), F32)
        for h in range(SB_HEADS):
            blk = jnp.concatenate([acc_ref[h * Q_ROWS:(h + 1) * Q_ROWS, :], z8], axis=0)
            o_ref[0, :, h * HEAD:(h + 1) * HEAD] = blk.astype(o_ref.dtype)


def sb_sample(q, k_new, v_new, cache_k, cache_v, page_table, bias, *, t_valid, pages_per_step=4):
    b, n_pages = page_table.shape
    page = cache_k.shape[1]
    assert page == HEAD and n_pages % pages_per_step == 0 and t_valid <= Q_ROWS
    assert cache_k.shape[2:] == (SB_HEADS, HEAD)
    rows = SB_HEADS * Q_ROWS
    bias_rows = jnp.broadcast_to(jnp.repeat(bias.astype(F32), Q_ROWS)[:, None], (rows, HEAD))

    def page_spec(r):
        return pl.BlockSpec(
            (1, page, HEAD_SLOTS, HEAD),
            lambda i, g, pt: (pt[i, n_pages - 1 - (g * pages_per_step + r)], 0, 0, 0))

    return pl.pallas_call(
        functools.partial(_sb_sample_kernel, pages_per_step=pages_per_step, t_valid=t_valid),
        grid_spec=pltpu.PrefetchScalarGridSpec(
            num_scalar_prefetch=1,
            grid=(b, n_pages // pages_per_step),
            in_specs=[
                pl.BlockSpec((1, SAMPLE_T_PAD, D_RNN), lambda i, g, pt: (i, 0, 0)),
                pl.BlockSpec((1, SAMPLE_T_PAD, D_RNN), lambda i, g, pt: (i, 0, 0)),
                pl.BlockSpec((1, SAMPLE_T_PAD, D_RNN), lambda i, g, pt: (i, 0, 0)),
                pl.BlockSpec((rows, HEAD), lambda i, g, pt: (0, 0)),
            ] + [page_spec(r) for r in range(pages_per_step)] + [page_spec(r) for r in range(pages_per_step)],
            out_specs=pl.BlockSpec((1, SAMPLE_T_PAD, D_RNN), lambda i, g, pt: (i, 0, 0)),
            scratch_shapes=[pltpu.VMEM((rows, HEAD), F32), pltpu.VMEM((rows, HEAD), F32)],
        ),
        out_shape=jax.ShapeDtypeStruct((b, SAMPLE_T_PAD, D_RNN), BF16),
        compiler_params=_params("parallel", "arbitrary"),
        name="sb_sample",
    )(page_table, q, k_new, v_new, bias_rows, *([cache_k] * pages_per_step), *([cache_v] * pages_per_step))


def _trunk(x3, t_valid, rg_h0, rg_conv0, ffn_conv0, mem_k, mem_v, sb_attend, P, W):
    b, t, d = x3.shape
    m = b * t
    x = x3.reshape(m, d)
    rg_h_out, rg_conv_out, ffn_conv_out = [], [], []
    k_f32 = v_f32 = k_bf = v_bf = None
    for l in range(DEPTH):
        mem_gain = jnp.tile(P['mem_q_norm'][l], MEM_HEADS)
        if l < N_A_LAYERS:
            p_rg = norm_matmul(x, P['g_mix'][l], W['w_in_a_rg'][l])
            qm = norm_matmul(x, P['g_mix'][l], W['w_in_a_qm'][l], head_gain=mem_gain)
            y_tok, h_last, cbuf = rglru(
                p_rg.reshape(b, t, 2 * D_RNN), rg_h0[l], rg_conv0[l], P['rg_conv_w'][l], P['rg_conv_b'][l],
                W['rg_gate_x_w'][l], P['rg_gate_x_b'][l], W['rg_gate_a_w'][l], P['rg_gate_a_b'][l],
                P['rg_lambda'][l], t_valid=t_valid)
            rg_h_out.append(h_last)
            rg_conv_out.append(cbuf)
            q3, q_col = qm.reshape(b, t, D_MEM), 0
        else:
            j = l - N_A_LAYERS
            gain = jnp.concatenate([jnp.tile(P['sb_q_norm'][j], SB_HEADS), mem_gain])
            pq = norm_matmul(x, P['g_mix'][l], W['w_in_b'][j], head_gain=gain)
            q3, q_col = pq.reshape(b, t, D_RNN + D_MEM), D_RNN // D_MEM
            y_tok = sb_attend(q3, k_bf, v_bf, P['sb_beta_bias'][j])
        y_mem = mem_attn(q3, q_col, mem_k[l], mem_v[l])
        x = matmul_residual([(y_tok.reshape(m, D_RNN), W['w_out_tok'][l]),
                             (y_mem.reshape(m, D_MEM), W['w_out_mem'][l])], x)
        up = norm_matmul(x, P['g_ffn'][l], W['w_ffn_up'][l])
        act, fbuf = ffn_act(up.reshape(b, t, 2 * D_FF), ffn_conv0[l], P['ffn_conv_w'][l], P['ffn_conv_b'][l],
                            t_valid=t_valid)
        x = matmul_residual([(act.reshape(m, D_FF), W['w_ffn_down'][l])], x, n_k=3)
        ffn_conv_out.append(fbuf)
        if l == N_A_LAYERS - 1:
            k_gain = jnp.tile(P['kv_k_norm'], SB_HEADS)
            k_f32, k_bf = norm_matmul(x, P['kv_norm'], W['w_k'], head_gain=k_gain, out_dtypes=(F32, BF16))
            v_f32, v_bf = norm_matmul(x, P['kv_norm'], W['w_v'], out_dtypes=(F32, BF16))
            k_bf = k_bf.reshape(b, t, D_RNN)
            v_bf = v_bf.reshape(b, t, D_RNN)
    return (x.reshape(b, t, d), jnp.stack(rg_h_out), jnp.stack(rg_conv_out), jnp.stack(ffn_conv_out),
            k_f32.reshape(b, t, SB_HEADS, HEAD), v_f32.reshape(b, t, SB_HEADS, HEAD))


def kernel(x_prompt, x_sample, mem_prompt, state_rglru_h, state_rglru_conv, state_ffn_conv, cache_mem_k, cache_mem_v, cache_sb_k, cache_sb_v, page_table, g_mix, g_ffn, w_in_a, rg_conv_w, rg_conv_b, rg_gate_x_w, rg_gate_x_b, rg_gate_a_w, rg_gate_a_b, rg_lambda, w_in_b, sb_q_norm, sb_beta_bias, kv_norm, w_kv, kv_k_norm, mem_norm, w_mem_kv, mem_q_norm, mem_k_norm, w_out, w_ffn_up, ffn_conv_w, ffn_conv_b, w_ffn_down):
    P = {'g_mix': g_mix, 'g_ffn': g_ffn, 'rg_conv_w': rg_conv_w, 'rg_conv_b': rg_conv_b,
         'rg_gate_x_b': rg_gate_x_b, 'rg_gate_a_b': rg_gate_a_b, 'rg_lambda': rg_lambda,
         'sb_q_norm': sb_q_norm, 'sb_beta_bias': sb_beta_bias, 'kv_norm': kv_norm,
         'kv_k_norm': kv_k_norm, 'mem_q_norm': mem_q_norm,
         'ffn_conv_w': ffn_conv_w, 'ffn_conv_b': ffn_conv_b}
    W = {'w_in_a_rg': w_in_a[:, :, :2 * D_RNN].astype(BF16),
         'w_in_a_qm': w_in_a[:, :, 2 * D_RNN:].astype(BF16),
         'rg_gate_x_w': rg_gate_x_w.astype(BF16), 'rg_gate_a_w': rg_gate_a_w.astype(BF16),
         'w_in_b': w_in_b.astype(BF16),
         'w_k': w_kv[:, :D_RNN].astype(BF16), 'w_v': w_kv[:, D_RNN:].astype(BF16),
         'w_out_tok': w_out[:, :D_RNN].astype(BF16), 'w_out_mem': w_out[:, D_RNN:].astype(BF16),
         'w_ffn_up': w_ffn_up.astype(BF16), 'w_ffn_down': w_ffn_down.astype(BF16)}
    w_mem_k = w_mem_kv[:, :, :D_MEM].astype(BF16)
    w_mem_v = w_mem_kv[:, :, D_MEM:].astype(BF16)

    bp, seq, d = x_prompt.shape
    mem2 = mem_prompt.reshape(bp * MEM_TOKENS, d)
    mk_list, mv_list = [], []
    for l in range(DEPTH):
        mk_list.append(norm_matmul(mem2, mem_norm[l], w_mem_k[l], head_gain=jnp.tile(mem_k_norm[l], MEM_HEADS),
                                   out_dtypes=(F32,)))
        mv_list.append(norm_matmul(mem2, mem_norm[l], w_mem_v[l], out_dtypes=(F32,)))
    prompt_mem_k = jnp.stack(mk_list).reshape(DEPTH, bp, MEM_TOKENS, D_MEM)
    prompt_mem_v = jnp.stack(mv_list).reshape(DEPTH, bp, MEM_TOKENS, D_MEM)
    zeros_h = jnp.zeros((N_A_LAYERS, bp, D_RNN), F32)
    zeros_rc = jnp.zeros((N_A_LAYERS, bp, RG_CONV - 1, D_RNN), F32)
    zeros_fc = jnp.zeros((DEPTH, bp, FFN_CONV - 1, D_FF), F32)
    (y_prompt, prompt_rglru_h, prompt_rglru_conv, prompt_ffn_conv, prompt_sb_k, prompt_sb_v) = _trunk(
        x_prompt, seq, zeros_h, zeros_rc, zeros_fc, prompt_mem_k, prompt_mem_v, sb_prompt, P, W)

    db, dec_seq, _ = x_sample.shape
    xs = jnp.pad(x_sample, ((0, 0), (0, SAMPLE_T_PAD - dec_seq), (0, 0)))

    def sb_paged(q3, k_bf, v_bf, bias):
        return sb_sample(q3, k_bf, v_bf, cache_sb_k, cache_sb_v, page_table, bias, t_valid=dec_seq)

    (y_s, sample_rglru_h, sample_rglru_conv, sample_ffn_conv, s_k, s_v) = _trunk(
        xs, dec_seq, state_rglru_h, state_rglru_conv, state_ffn_conv,
        cache_mem_k.reshape(DEPTH, db, MEM_TOKENS, D_MEM), cache_mem_v.reshape(DEPTH, db, MEM_TOKENS, D_MEM),
        sb_paged, P, W)

    mem_shape = (DEPTH, bp, MEM_TOKENS, MEM_HEADS, HEAD)
    return (y_prompt, y_s[:, :dec_seq],
            prompt_rglru_h, prompt_rglru_conv, prompt_ffn_conv,
            prompt_sb_k, prompt_sb_v, prompt_mem_k.reshape(mem_shape), prompt_mem_v.reshape(mem_shape),
            sample_rglru_h, sample_rglru_conv, sample_ffn_conv,
            s_k[:, :dec_seq], s_v[:, :dec_seq])
```

```python
import functools
import math

import jax
import jax.numpy as jnp
from jax import lax
from jax.experimental import pallas as pl
from jax.experimental.pallas import tpu as pltpu

F32 = jnp.float32
BF16 = jnp.bfloat16

LANE = 128
SUBLANE = 8
VMEM_LIMIT_BYTES = 56 * 1024 * 1024

D_MODEL = 2048
DEPTH = 4
N_A_LAYERS = 2
D_RNN = 1536
HEAD = 128
RG_HEADS = D_RNN // HEAD
RG_CONV = 4
RG_C = 8.0
SB_HEADS = D_RNN // HEAD
MEM_TOKENS = 256
MEM_HEADS = 4
D_MEM = MEM_HEADS * HEAD
D_FF = 3 * D_MODEL
FFN_CONV = 3
EPS = 1e-6
ATTN_SCALE = HEAD ** -0.5
SAMPLE_T_PAD = 16
Q_ROWS = 8


def _params(*sem):
    return pltpu.CompilerParams(dimension_semantics=sem, vmem_limit_bytes=VMEM_LIMIT_BYTES)


def _tile(n, pref, mult):
    if n <= pref:
        return n
    t = (pref // mult) * mult
    while t > mult and n % t:
        t -= mult
    assert n % t == 0, (n, pref, mult)
    return t


def _softplus(z):
    return jnp.maximum(z, 0.0) + jnp.log1p(jnp.exp(-jnp.abs(z)))


def _head_rmsnorm(blk, gain):
    ms = jnp.mean(blk * blk, axis=-1, keepdims=True)
    return blk * lax.rsqrt(ms + EPS) * gain


def _norm_matmul_kernel(*refs, head_norm, head_major, n_out):
    x_ref, g_ref, w_ref = refs[:3]
    pos = 3
    hg_ref = None
    if head_norm:
        hg_ref = refs[pos]
        pos += 1
    o_refs = refs[pos:pos + n_out]
    xn_ref = refs[pos + n_out]

    @pl.when(pl.program_id(1) == 0)
    def _():
        x = x_ref[...]
        ms = jnp.mean(x * x, axis=-1, keepdims=True)
        xn_ref[...] = (x * lax.rsqrt(ms + EPS) * g_ref[...]).astype(BF16)

    acc = jnp.dot(xn_ref[...], w_ref[...], preferred_element_type=F32)
    tn = acc.shape[1]
    if head_norm or head_major:
        for h in range(tn // HEAD):
            cs = slice(h * HEAD, (h + 1) * HEAD)
            y = _head_rmsnorm(acc[:, cs], hg_ref[:, cs]) if head_norm else acc[:, cs]
            for o_ref in o_refs:
                if head_major:
                    o_ref[0, h] = y.astype(o_ref.dtype)
                else:
                    o_ref[:, cs] = y.astype(o_ref.dtype)
    else:
        for o_ref in o_refs:
            o_ref[...] = acc.astype(o_ref.dtype)


def norm_matmul(x, g, w, *, head_gain=None, out_dtypes=(BF16,), head_major_seq=None, tm_pref=1024,
                tn_pref=1024):
    m, d = x.shape
    n = w.shape[1]
    tm = _tile(m, tm_pref, SUBLANE)
    tn = _tile(n, tn_pref, 2 * LANE)
    if head_major_seq is None:
        out_specs = [pl.BlockSpec((tm, tn), lambda i, j: (i, j)) for _ in out_dtypes]
        out_shape = [jax.ShapeDtypeStruct((m, n), dt) for dt in out_dtypes]
    else:
        seq = head_major_seq
        assert seq % tm == 0 and m % seq == 0
        per_seq = seq // tm
        out_specs = [pl.BlockSpec((1, tn // HEAD, tm, HEAD), lambda i, j: (i // per_seq, j, i % per_seq, 0))
                     for _ in out_dtypes]
        out_shape = [jax.ShapeDtypeStruct((m // seq, n // HEAD, seq, HEAD), dt) for dt in out_dtypes]
    in_specs = [
        pl.BlockSpec((tm, d), lambda i, j: (i, 0)),
        pl.BlockSpec((1, d), lambda i, j: (0, 0)),
        pl.BlockSpec((d, tn), lambda i, j: (0, j)),
    ]
    args = [x, g.reshape(1, d).astype(F32), w]
    if head_gain is not None:
        in_specs.append(pl.BlockSpec((1, tn), lambda i, j: (0, j)))
        args.append(head_gain.reshape(1, n).astype(F32))
    outs = pl.pallas_call(
        functools.partial(_norm_matmul_kernel, head_norm=head_gain is not None,
                          head_major=head_major_seq is not None, n_out=len(out_dtypes)),
        grid=(m // tm, n // tn),
        in_specs=in_specs,
        out_specs=out_specs,
        out_shape=out_shape,
        scratch_shapes=[pltpu.VMEM((tm, d), BF16)],
        compiler_params=_params("parallel", "arbitrary"),
        name="norm_matmul",
    )(*args)
    return outs[0] if len(out_dtypes) == 1 else tuple(outs)


def _matmul_residual_kernel(*refs, n_pairs):
    x_ref = refs[2 * n_pairs]
    o_ref = refs[2 * n_pairs + 1]

    @pl.when(pl.program_id(2) == 0)
    def _():
        o_ref[...] = x_ref[...]

    acc = jnp.dot(refs[0][...], refs[1][...], preferred_element_type=F32)
    for p in range(1, n_pairs):
        acc += jnp.dot(refs[2 * p][...], refs[2 * p + 1][...], preferred_element_type=F32)
    o_ref[...] += acc


def matmul_residual(pairs, x, *, n_k=1, tm_pref=1024, tn_pref=1024):
    m, n = x.shape
    tm = _tile(m, tm_pref, SUBLANE)
    tn = _tile(n, tn_pref, 2 * LANE)
    in_specs, args = [], []
    for a, w in pairs:
        kp = a.shape[1]
        assert kp % n_k == 0 and w.shape == (kp, n)
        tk = kp // n_k
        in_specs.append(pl.BlockSpec((tm, tk), lambda i, j, k: (i, k)))
        in_specs.append(pl.BlockSpec((tk, tn), lambda i, j, k: (k, j)))
        args += [a, w]
    in_specs.append(pl.BlockSpec((tm, tn), lambda i, j, k: (i, j)))
    args.append(x)
    return pl.pallas_call(
        functools.partial(_matmul_residual_kernel, n_pairs=len(pairs)),
        grid=(m // tm, n // tn, n_k),
        in_specs=in_specs,
        out_specs=pl.BlockSpec((tm, tn), lambda i, j, k: (i, j)),
        out_shape=jax.ShapeDtypeStruct((m, n), F32),
        compiler_params=_params("parallel", "parallel", "arbitrary"),
        name="matmul_residual",
    )(*args)


def _rglru_kernel(p_ref, h0_ref, cb0_ref, cw_ref, cbias_ref, wgx_ref, bgx_ref, wga_ref, bga_ref, lam_ref,
                  y_ref, hl_ref, cbo_ref, ext_ref, hc_ref, *, tr, tile_last, r_last):
    t = pl.program_id(1)
    halo = SUBLANE

    @pl.when(t == 0)
    def _():
        ext_ref[0:halo, :] = jnp.zeros((halo, D_RNN), F32)
        ext_ref[halo - (RG_CONV - 1):halo, :] = cb0_ref[0]
        hc_ref[...] = h0_ref[0]

    ext_ref[halo:halo + tr, :] = p_ref[0, :, 0:D_RNN].astype(F32)
    row = lax.broadcasted_iota(jnp.int32, (tr, HEAD), 0)
    for h in range(RG_HEADS):
        cs = slice(h * HEAD, (h + 1) * HEAD)
        xc = cbias_ref[:, cs] + ext_ref[halo - 3:halo - 3 + tr, cs] * cw_ref[0:1, cs]
        for j in range(1, RG_CONV):
            xc = xc + ext_ref[halo - 3 + j:halo - 3 + j + tr, cs] * cw_ref[j:j + 1, cs]
        xcb = xc.astype(BF16)
        gx = jax.nn.sigmoid(jnp.dot(xcb, wgx_ref[h], preferred_element_type=F32) + bgx_ref[:, cs])
        ga = jax.nn.sigmoid(jnp.dot(xcb, wga_ref[h], preferred_element_type=F32) + bga_ref[:, cs])
        log_a = (-RG_C) * ga * _softplus(-lam_ref[:, cs])
        a = jnp.exp(log_a)
        u = jnp.sqrt(-jnp.tanh(log_a) * (1.0 + a * a)) * (gx * xc)
        u = u + jnp.where(row == 0, a * hc_ref[:, cs], 0.0)
        d = 1
        while d < tr:
            a_prev = pltpu.roll(a, d, 0)
            u_prev = pltpu.roll(u, d, 0)
            live = row >= d
            u = jnp.where(live, a * u_prev + u, u)
            a = jnp.where(live, a * a_prev, a)
            d *= 2
        gate = jax.nn.gelu(p_ref[0, :, D_RNN + h * HEAD:D_RNN + (h + 1) * HEAD].astype(F32))
        y_ref[0, :, cs] = (u * gate).astype(y_ref.dtype)
        hc_ref[:, cs] = u[tr - 1:tr, :]

        @pl.when(t == tile_last)
        def _():
            hl_ref[0, :, cs] = u[r_last:r_last + 1, :]

    @pl.when(t == tile_last)
    def _():
        cbo_ref[0] = ext_ref[halo + r_last - 2:halo + r_last + 1, :]

    if tr >= halo:
        ext_ref[0:halo, :] = ext_ref[tr:tr + halo, :]


def rglru(p, h0, cb0, cw, cbias, wgx, bgx, wga, bga, lam, *, t_valid):
    b, t, _ = p.shape
    tr = _tile(t, 256, 16)
    row = lambda v: v.reshape(1, D_RNN).astype(F32)
    full = lambda shape: pl.BlockSpec(shape, lambda i, j: (0,) * len(shape))
    y, hl, cbo = pl.pallas_call(
        functools.partial(_rglru_kernel, tr=tr, tile_last=(t_valid - 1) // tr, r_last=(t_valid - 1) % tr),
        grid=(b, t // tr),
        in_specs=[
            pl.BlockSpec((1, tr, 2 * D_RNN), lambda i, j: (i, j, 0)),
            pl.BlockSpec((1, 1, D_RNN), lambda i, j: (i, 0, 0)),
            pl.BlockSpec((1, RG_CONV - 1, D_RNN), lambda i, j: (i, 0, 0)),
            full((RG_CONV, D_RNN)), full((1, D_RNN)),
            full((RG_HEADS, HEAD, HEAD)), full((1, D_RNN)),
            full((RG_HEADS, HEAD, HEAD)), full((1, D_RNN)),
            full((1, D_RNN)),
        ],
        out_specs=[
            pl.BlockSpec((1, tr, D_RNN), lambda i, j: (i, j, 0)),
            pl.BlockSpec((1, 1, D_RNN), lambda i, j: (i, 0, 0)),
            pl.BlockSpec((1, RG_CONV - 1, D_RNN), lambda i, j: (i, 0, 0)),
        ],
        out_shape=[
            jax.ShapeDtypeStruct((b, t, D_RNN), BF16),
            jax.ShapeDtypeStruct((b, 1, D_RNN), F32),
            jax.ShapeDtypeStruct((b, RG_CONV - 1, D_RNN), F32),
        ],
        scratch_shapes=[pltpu.VMEM((tr + SUBLANE, D_RNN), F32), pltpu.VMEM((1, D_RNN), F32)],
        compiler_params=_params("parallel", "arbitrary"),
        name="rglru",
    )(p, h0.reshape(b, 1, D_RNN), cb0, cw.astype(F32), row(cbias), wgx, row(bgx), wga, row(bga), row(lam))
    return y, hl.reshape(b, D_RNN), cbo


def _mem_attn_kernel(q_ref, k_ref, v_ref, o_ref):
    for h in range(MEM_HEADS):
        cs = slice(h * HEAD, (h + 1) * HEAD)
        q = q_ref[0, :, cs]
        k = k_ref[0, :, cs].astype(BF16)
        v = v_ref[0, :, cs].astype(BF16)
        s = lax.dot_general(q, k, (((1,), (1,)), ((), ())), preferred_element_type=F32)
        e = jnp.exp(s - jnp.max(s, axis=-1, keepdims=True))
        p = e / jnp.sum(e, axis=-1, keepdims=True)
        o_ref[0, :, cs] = jnp.dot(p.astype(BF16), v, preferred_element_type=F32).astype(o_ref.dtype)


def mem_attn(q, q_col_block, mk, mv):
    b, t, _ = q.shape
    tq = _tile(t, 512, 16)
    return pl.pallas_call(
        _mem_attn_kernel,
        grid=(b, t // tq),
        in_specs=[
            pl.BlockSpec((1, tq, D_MEM), lambda i, j: (i, j, q_col_block)),
            pl.BlockSpec((1, MEM_TOKENS, D_MEM), lambda i, j: (i, 0, 0)),
            pl.BlockSpec((1, MEM_TOKENS, D_MEM), lambda i, j: (i, 0, 0)),
        ],
        out_specs=pl.BlockSpec((1, tq, D_MEM), lambda i, j: (i, j, 0)),
        out_shape=jax.ShapeDtypeStruct((b, t, D_MEM), BF16),
        compiler_params=_params("parallel", "parallel"),
        name="mem_attn",
    )(q, mk, mv)


def _ffn_act_kernel(u_ref, g_ref, b0_ref, cw_ref, cb_ref, act_ref, nb_ref, ext_ref, *, tr, tile_last, r_last):
    t = pl.program_id(2)
    halo = SUBLANE
    tc = ext_ref.shape[1]

    @pl.when(t == 0)
    def _():
        ext_ref[0:halo, :] = jnp.zeros((halo, tc), F32)
        ext_ref[halo - (FFN_CONV - 1):halo, :] = b0_ref[0]

    ext_ref[halo:halo + tr, :] = u_ref[0].astype(F32)
    uc = cb_ref[...] + ext_ref[halo - 2:halo - 2 + tr, :] * cw_ref[0:1, :]
    for j in range(1, FFN_CONV):
        uc = uc + ext_ref[halo - 2 + j:halo - 2 + j + tr, :] * cw_ref[j:j + 1, :]
    act_ref[0] = (jax.nn.gelu(uc) * g_ref[0].astype(F32)).astype(act_ref.dtype)

    @pl.when(t == tile_last)
    def _():
        nb_ref[0] = ext_ref[halo + r_last - 1:halo + r_last + 1, :]

    if tr >= halo:
        ext_ref[0:halo, :] = ext_ref[tr:tr + halo, :]


def ffn_act(up, buf0, cw, cb, *, t_valid):
    b, t, _ = up.shape
    tr = _tile(t, 256, 16)
    tc = 1536
    n_c = D_FF // tc
    act, nb = pl.pallas_call(
        functools.partial(_ffn_act_kernel, tr=tr, tile_last=(t_valid - 1) // tr, r_last=(t_valid - 1) % tr),
        grid=(b, n_c, t // tr),
        in_specs=[
            pl.BlockSpec((1, tr, tc), lambda i, c, j: (i, j, c)),
            pl.BlockSpec((1, tr, tc), lambda i, c, j: (i, j, n_c + c)),
            pl.BlockSpec((1, FFN_CONV - 1, tc), lambda i, c, j: (i, 0, c)),
            pl.BlockSpec((FFN_CONV, tc), lambda i, c, j: (0, c)),
            pl.BlockSpec((1, tc), lambda i, c, j: (0, c)),
        ],
        out_specs=[
            pl.BlockSpec((1, tr, tc), lambda i, c, j: (i, j, c)),
            pl.BlockSpec((1, FFN_CONV - 1, tc), lambda i, c, j: (i, 0, c)),
        ],
        out_shape=[
            jax.ShapeDtypeStruct((b, t, D_FF), BF16),
            jax.ShapeDtypeStruct((b, FFN_CONV - 1, D_FF), F32),
        ],
        scratch_shapes=[pltpu.VMEM((tr + SUBLANE, tc), F32)],
        compiler_params=_params("parallel", "parallel", "arbitrary"),
        name="ffn_act",
    )(up, up, buf0, cw.astype(F32), cb.reshape(1, D_FF).astype(F32))
    return act, nb


def _suffix_matrix():
    j = lax.broadcasted_iota(jnp.int32, (2 * HEAD, 2 * HEAD), 0) % HEAD
    s = lax.broadcasted_iota(jnp.int32, (2 * HEAD, 2 * HEAD), 1)
    return jnp.where((s >= HEAD) | (j >= s), 1.0, 0.0).astype(BF16)


def _sb_weights(s, neg_bias, carry, suffix, mask):
    nz, lk = _sb_log_keep(s, neg_bias, mask)
    return _sb_finish(nz, _sb_suffix_sums(lk, suffix), carry, mask)


def _sb_log_keep(s, neg_bias, mask):
    nz = neg_bias - s
    lk = jnp.minimum(nz, 0.0) - jnp.log1p(jnp.exp(-jnp.abs(nz)))
    if mask is not None:
        lk = jnp.where(mask, lk, 0.0)
    return nz, lk


def _sb_suffix_sums(lk, suffix):
    hi = lk.astype(BF16)
    lo = (lk - hi.astype(F32)).astype(BF16)
    return jnp.dot(jnp.concatenate([hi, lo], axis=1), suffix, preferred_element_type=F32)


def _sb_finish(nz, c2, carry, mask):
    w = jnp.exp((carry + c2[:, :HEAD]) - nz)
    if mask is not None:
        w = jnp.where(mask, w, 0.0)
    return w, carry + c2[:, HEAD:]


def _nt_dot(a, b):
    return lax.dot_general(a, b, (((1,), (1,)), ((), ())), preferred_element_type=F32)


def _sb_prompt_kernel(nbias_ref, q_ref, k_ref, v_ref, o_ref, carry_ref, acc_ref, *, tq):
    h = pl.program_id(1)
    qi = pl.program_id(2)
    neg_bias = nbias_ref[h]
    q = q_ref[0]
    suffix = _suffix_matrix()
    n_sub = tq // HEAD
    t_loc = lax.broadcasted_iota(jnp.int32, (tq, HEAD), 0)
    s_loc = lax.broadcasted_iota(jnp.int32, (tq, HEAD), 1)
    carry_ref[...] = jnp.zeros((tq, HEAD), F32)
    acc_ref[...] = jnp.zeros((tq, HEAD), F32)

    def block(start, masked):
        s = _nt_dot(q, k_ref[0, 0, pl.ds(start, tq), :])
        carry = carry_ref[...]
        ws = [None] * n_sub
        for j in range(n_sub - 1, -1, -1):
            mask = ((s_loc + j * HEAD) < t_loc) if masked else None
            w, carry = _sb_weights(s[:, j * HEAD:(j + 1) * HEAD], neg_bias, carry, suffix, mask)
            ws[j] = w.astype(BF16)
        carry_ref[...] = carry
        acc_ref[...] += jnp.dot(jnp.concatenate(ws, axis=1), v_ref[0, 0, pl.ds(start, tq), :],
                                preferred_element_type=F32)

    block(pl.multiple_of(qi * tq, tq), True)

    @pl.loop(0, qi)
    def _(i):
        block(pl.multiple_of((qi - 1 - i) * tq, tq), False)

    o_ref[0] = acc_ref[...].astype(o_ref.dtype)


def sb_prompt(q, k, v, bias):
    b, _, t, _ = k.shape
    tq = _tile(t, 512, HEAD)
    return pl.pallas_call(
        functools.partial(_sb_prompt_kernel, tq=tq),
        grid_spec=pltpu.PrefetchScalarGridSpec(
            num_scalar_prefetch=1,
            grid=(b, SB_HEADS, t // tq),
            in_specs=[
                pl.BlockSpec((1, tq, HEAD), lambda i, h, j, nb: (i, j, h)),
                pl.BlockSpec((1, 1, t, HEAD), lambda i, h, j, nb: (i, h, 0, 0)),
                pl.BlockSpec((1, 1, t, HEAD), lambda i, h, j, nb: (i, h, 0, 0)),
            ],
            out_specs=pl.BlockSpec((1, tq, HEAD), lambda i, h, j, nb: (i, j, h)),
            scratch_shapes=[pltpu.VMEM((tq, HEAD), F32), pltpu.VMEM((tq, HEAD), F32)],
        ),
        out_shape=jax.ShapeDtypeStruct((b, t, D_RNN), BF16),
        compiler_params=_params("parallel", "parallel", "arbitrary"),
        name="sb_prompt",
    )(-bias.astype(F32), q, k, v)


def _sb_sample_kernel(pt_ref, q_ref, kn_ref, vn_ref, bias_ref, *rest, pages_per_step, t_valid):
    kp_refs = rest[:pages_per_step]
    vp_refs = rest[pages_per_step:2 * pages_per_step]
    o_ref, carry_ref, acc_ref = rest[2 * pages_per_step:]
    g = pl.program_id(1)
    rows = SB_HEADS * Q_ROWS
    suffix = _suffix_matrix()
    neg_bias = bias_ref[...]
    hcols = [slice(h * HEAD, (h + 1) * HEAD) for h in range(SB_HEADS)]
    q_heads = [q_ref[0, :, hcols[h]] for h in range(SB_HEADS)]

    def update(blocks, mask):
        n = len(blocks)
        nzs, lks = [], []
        for k_heads, _ in blocks:
            s = jnp.concatenate([_nt_dot(q_heads[h], k_heads[h])[0:Q_ROWS] for h in range(SB_HEADS)], axis=0)
            nz, lk = _sb_log_keep(s, neg_bias, mask)
            nzs.append(nz)
            lks.append(lk)
        c2 = _sb_suffix_sums(jnp.concatenate(lks, axis=0), suffix)
        carry = carry_ref[...]
        ws = []
        for r in range(n):
            w, carry = _sb_finish(nzs[r], c2[r * rows:(r + 1) * rows], carry, mask)
            ws.append(w)
        carry_ref[...] = carry
        zpad = jnp.zeros((SAMPLE_T_PAD - Q_ROWS, HEAD), F32)
        for h in range(SB_HEADS):
            rs = slice(h * Q_ROWS, (h + 1) * Q_ROWS)
            pv = None
            for r in range(n):
                wh = jnp.concatenate([ws[r][rs], zpad], axis=0).astype(BF16)
                d = jnp.dot(wh, blocks[r][1][h], preferred_element_type=F32)
                pv = d if pv is None else pv + d
            acc_ref[rs, :] += pv[0:Q_ROWS]

    @pl.when(g == 0)
    def _():
        carry_ref[...] = jnp.zeros((rows, HEAD), F32)
        acc_ref[...] = jnp.zeros((rows, HEAD), F32)
        kz = jnp.zeros((HEAD - SAMPLE_T_PAD, HEAD), BF16)
        k_heads = [jnp.concatenate([kn_ref[0, :, hcols[h]], kz], axis=0) for h in range(SB_HEADS)]
        v_heads = [jnp.concatenate([vn_ref[0, :, hcols[h]], kz], axis=0) for h in range(SB_HEADS)]
        i_q = lax.broadcasted_iota(jnp.int32, (rows, HEAD), 0) % Q_ROWS
        j_k = lax.broadcasted_iota(jnp.int32, (rows, HEAD), 1)
        update([(k_heads, v_heads)], (j_k < i_q) & (j_k < t_valid))

    update([([kp_refs[r][0, h].astype(BF16) for h in range(SB_HEADS)],
             [vp_refs[r][0, h].astype(BF16) for h in range(SB_HEADS)]) for r in range(pages_per_step)], None)

    @pl.when(g == pl.num_programs(1) - 1)
    def _():
        z8 = jnp.zeros((SAMPLE_T_PAD - Q_ROWS, HEAD), F32)
        for h in range(SB_HEADS):
            blk = jnp.concatenate([acc_ref[h * Q_ROWS:(h + 1) * Q_ROWS, :], z8], axis=0)
            o_ref[0, :, h * HEAD:(h + 1) * HEAD] = blk.astype(o_ref.dtype)


def sb_sample(q, k_new, v_new, cache_k, cache_v, page_table, bias, *, t_valid, pages_per_step=8):
    b, n_pages = page_table.shape
    page = cache_k.shape[2]
    assert page == HEAD and n_pages % pages_per_step == 0 and t_valid <= Q_ROWS
    assert cache_k.shape[1:] == (SB_HEADS, page, HEAD)
    rows = SB_HEADS * Q_ROWS
    bias_rows = jnp.broadcast_to(jnp.repeat(-bias.astype(F32), Q_ROWS)[:, None], (rows, HEAD))

    def page_spec(r):
        return pl.BlockSpec(
            (1, SB_HEADS, page, HEAD),
            lambda i, g, pt: (pt[i, n_pages - 1 - (g * pages_per_step + r)], 0, 0, 0))

    return pl.pallas_call(
        functools.partial(_sb_sample_kernel, pages_per_step=pages_per_step, t_valid=t_valid),
        grid_spec=pltpu.PrefetchScalarGridSpec(
            num_scalar_prefetch=1,
            grid=(b, n_pages // pages_per_step),
            in_specs=[
                pl.BlockSpec((1, SAMPLE_T_PAD, D_RNN), lambda i, g, pt: (i, 0, 0)),
                pl.BlockSpec((1, SAMPLE_T_PAD, D_RNN), lambda i, g, pt: (i, 0, 0)),
                pl.BlockSpec((1, SAMPLE_T_PAD, D_RNN), lambda i, g, pt: (i, 0, 0)),
                pl.BlockSpec((rows, HEAD), lambda i, g, pt: (0, 0)),
            ] + [page_spec(r) for r in range(pages_per_step)] + [page_spec(r) for r in range(pages_per_step)],
            out_specs=pl.BlockSpec((1, SAMPLE_T_PAD, D_RNN), lambda i, g, pt: (i, 0, 0)),
            scratch_shapes=[pltpu.VMEM((rows, HEAD), F32), pltpu.VMEM((rows, HEAD), F32)],
        ),
        out_shape=jax.ShapeDtypeStruct((b, SAMPLE_T_PAD, D_RNN), BF16),
        compiler_params=_params("parallel", "arbitrary"),
        name="sb_sample",
    )(page_table, q, k_new, v_new, bias_rows, *([cache_k] * pages_per_step), *([cache_v] * pages_per_step))


def _trunk(x3, t_valid, rg_h0, rg_conv0, ffn_conv0, mem_k, mem_v, sb_attend, head_major_kv, P, W):
    b, t, d = x3.shape
    m = b * t
    x = x3.reshape(m, d)
    rg_h_out, rg_conv_out, ffn_conv_out = [], [], []
    k_f32 = v_f32 = k_bf = v_bf = None
    for l in range(DEPTH):
        mem_gain = jnp.tile(P['mem_q_norm'][l] * ATTN_SCALE, MEM_HEADS)
        if l < N_A_LAYERS:
            p_rg = norm_matmul(x, P['g_mix'][l], W['w_in_a_rg'][l])
            qm = norm_matmul(x, P['g_mix'][l], W['w_in_a_qm'][l], head_gain=mem_gain)
            y_tok, h_last, cbuf = rglru(
                p_rg.reshape(b, t, 2 * D_RNN), rg_h0[l], rg_conv0[l], P['rg_conv_w'][l], P['rg_conv_b'][l],
                W['rg_gate_x_w'][l], P['rg_gate_x_b'][l], W['rg_gate_a_w'][l], P['rg_gate_a_b'][l],
                P['rg_lambda'][l], t_valid=t_valid)
            rg_h_out.append(h_last)
            rg_conv_out.append(cbuf)
            q3, q_col = qm.reshape(b, t, D_MEM), 0
        else:
            j = l - N_A_LAYERS
            gain = jnp.concatenate([jnp.tile(P['sb_q_norm'][j] * ATTN_SCALE, SB_HEADS), mem_gain])
            pq = norm_matmul(x, P['g_mix'][l], W['w_in_b'][j], head_gain=gain)
            q3, q_col = pq.reshape(b, t, D_RNN + D_MEM), D_RNN // D_MEM
            y_tok = sb_attend(q3, k_bf, v_bf, P['sb_beta_bias'][j])
        y_mem = mem_attn(q3, q_col, mem_k[l], mem_v[l])
        x = matmul_residual([(y_tok.reshape(m, D_RNN), W['w_out_tok'][l]),
                             (y_mem.reshape(m, D_MEM), W['w_out_mem'][l])], x)
        up = norm_matmul(x, P['g_ffn'][l], W['w_ffn_up'][l])
        act, fbuf = ffn_act(up.reshape(b, t, 2 * D_FF), ffn_conv0[l], P['ffn_conv_w'][l], P['ffn_conv_b'][l],
                            t_valid=t_valid)
        x = matmul_residual([(act.reshape(m, D_FF), W['w_ffn_down'][l])], x, n_k=3)
        ffn_conv_out.append(fbuf)
        if l == N_A_LAYERS - 1:
            k_gain = jnp.tile(P['kv_k_norm'], SB_HEADS)
            hm = t if head_major_kv else None
            k_f32, k_bf = norm_matmul(x, P['kv_norm'], W['w_k'], head_gain=k_gain, out_dtypes=(F32, BF16),
                                      head_major_seq=hm)
            v_f32, v_bf = norm_matmul(x, P['kv_norm'], W['w_v'], out_dtypes=(F32, BF16), head_major_seq=hm)
            if not head_major_kv:
                k_bf = k_bf.reshape(b, t, D_RNN)
                v_bf = v_bf.reshape(b, t, D_RNN)
    if head_major_kv:
        k_out, v_out = k_f32.transpose(0, 2, 1, 3), v_f32.transpose(0, 2, 1, 3)
    else:
        k_out, v_out = k_f32.reshape(b, t, SB_HEADS, HEAD), v_f32.reshape(b, t, SB_HEADS, HEAD)
    return (x.reshape(b, t, d), jnp.stack(rg_h_out), jnp.stack(rg_conv_out), jnp.stack(ffn_conv_out),
            k_out, v_out)


def kernel(x_prompt, x_sample, mem_prompt, state_rglru_h, state_rglru_conv, state_ffn_conv, cache_mem_k, cache_mem_v, cache_sb_k, cache_sb_v, page_table, g_mix, g_ffn, w_in_a, rg_conv_w, rg_conv_b, rg_gate_x_w, rg_gate_x_b, rg_gate_a_w, rg_gate_a_b, rg_lambda, w_in_b, sb_q_norm, sb_beta_bias, kv_norm, w_kv, kv_k_norm, mem_norm, w_mem_kv, mem_q_norm, mem_k_norm, w_out, w_ffn_up, ffn_conv_w, ffn_conv_b, w_ffn_down):
    P = {'g_mix': g_mix, 'g_ffn': g_ffn, 'rg_conv_w': rg_conv_w, 'rg_conv_b': rg_conv_b,
         'rg_gate_x_b': rg_gate_x_b, 'rg_gate_a_b': rg_gate_a_b, 'rg_lambda': rg_lambda,
         'sb_q_norm': sb_q_norm, 'sb_beta_bias': sb_beta_bias, 'kv_norm': kv_norm,
         'kv_k_norm': kv_k_norm, 'mem_q_norm': mem_q_norm,
         'ffn_conv_w': ffn_conv_w, 'ffn_conv_b': ffn_conv_b}
    W = {'w_in_a_rg': w_in_a[:, :, :2 * D_RNN].astype(BF16),
         'w_in_a_qm': w_in_a[:, :, 2 * D_RNN:].astype(BF16),
         'rg_gate_x_w': rg_gate_x_w.astype(BF16), 'rg_gate_a_w': rg_gate_a_w.astype(BF16),
         'w_in_b': w_in_b.astype(BF16),
         'w_k': w_kv[:, :D_RNN].astype(BF16), 'w_v': w_kv[:, D_RNN:].astype(BF16),
         'w_out_tok': w_out[:, :D_RNN].astype(BF16), 'w_out_mem': w_out[:, D_RNN:].astype(BF16),
         'w_ffn_up': w_ffn_up.astype(BF16), 'w_ffn_down': w_ffn_down.astype(BF16)}
    w_mem_k = w_mem_kv[:, :, :D_MEM].astype(BF16)
    w_mem_v = w_mem_kv[:, :, D_MEM:].astype(BF16)

    bp, seq, d = x_prompt.shape
    mem2 = mem_prompt.reshape(bp * MEM_TOKENS, d)
    mk_list, mv_list = [], []
    for l in range(DEPTH):
        mk_list.append(norm_matmul(mem2, mem_norm[l], w_mem_k[l], head_gain=jnp.tile(mem_k_norm[l], MEM_HEADS),
                                   out_dtypes=(F32,)))
        mv_list.append(norm_matmul(mem2, mem_norm[l], w_mem_v[l], out_dtypes=(F32,)))
    prompt_mem_k = jnp.stack(mk_list).reshape(DEPTH, bp, MEM_TOKENS, D_MEM)
    prompt_mem_v = jnp.stack(mv_list).reshape(DEPTH, bp, MEM_TOKENS, D_MEM)
    zeros_h = jnp.zeros((N_A_LAYERS, bp, D_RNN), F32)
    zeros_rc = jnp.zeros((N_A_LAYERS, bp, RG_CONV - 1, D_RNN), F32)
    zeros_fc = jnp.zeros((DEPTH, bp, FFN_CONV - 1, D_FF), F32)
    (y_prompt, prompt_rglru_h, prompt_rglru_conv, prompt_ffn_conv, prompt_sb_k, prompt_sb_v) = _trunk(
        x_prompt, seq, zeros_h, zeros_rc, zeros_fc, prompt_mem_k, prompt_mem_v, sb_prompt, True, P, W)

    db, dec_seq, _ = x_sample.shape
    xs = jnp.pad(x_sample, ((0, 0), (0, SAMPLE_T_PAD - dec_seq), (0, 0)))

    cache_k_hm = cache_sb_k.transpose(0, 2, 1, 3)
    cache_v_hm = cache_sb_v.transpose(0, 2, 1, 3)

    def sb_paged(q3, k_bf, v_bf, bias):
        return sb_sample(q3, k_bf, v_bf, cache_k_hm, cache_v_hm, page_table, bias, t_valid=dec_seq)

    (y_s, sample_rglru_h, sample_rglru_conv, sample_ffn_conv, s_k, s_v) = _trunk(
        xs, dec_seq, state_rglru_h, state_rglru_conv, state_ffn_conv,
        cache_mem_k.reshape(DEPTH, db, MEM_TOKENS, D_MEM), cache_mem_v.reshape(DEPTH, db, MEM_TOKENS, D_MEM),
        sb_paged, False, P, W)

    mem_shape = (DEPTH, bp, MEM_TOKENS, MEM_HEADS, HEAD)
    return (y_prompt, y_s[:, :dec_seq],
            prompt_rglru_h, prompt_rglru_conv, prompt_ffn_conv,
            prompt_sb_k, prompt_sb_v, prompt_mem_k.reshape(mem_shape), prompt_mem_v.reshape(mem_shape),
            sample_rglru_h, sample_rglru_conv, sample_ffn_conv,
            s_k[:, :dec_seq], s_v[:, :dec_seq])
```

```python
import functools
import math

import jax
import jax.numpy as jnp
from jax import lax
from jax.experimental import pallas as pl
from jax.experimental.pallas import tpu as pltpu

F32 = jnp.float32
BF16 = jnp.bfloat16

LANE = 128
SUBLANE = 8
VMEM_LIMIT_BYTES = 56 * 1024 * 1024

D_MODEL = 2048
DEPTH = 4
N_A_LAYERS = 2
D_RNN = 1536
HEAD = 128
RG_HEADS = D_RNN // HEAD
RG_CONV = 4
RG_C = 8.0
SB_HEADS = D_RNN // HEAD
MEM_TOKENS = 256
MEM_HEADS = 4
D_MEM = MEM_HEADS * HEAD
D_FF = 3 * D_MODEL
FFN_CONV = 3
EPS = 1e-6
ATTN_SCALE = HEAD ** -0.5
LOG2E = math.log2(math.e)
SAMPLE_T_PAD = 16
Q_ROWS = 8
FFN_ROW_CHUNK = 256
SB_HEADS_PER_STEP = 3


def _params(*sem):
    return pltpu.CompilerParams(dimension_semantics=sem, vmem_limit_bytes=VMEM_LIMIT_BYTES)


def _tile(n, pref, mult):
    if n <= pref:
        return n
    t = (pref // mult) * mult
    while t > mult and n % t:
        t -= mult
    assert n % t == 0, (n, pref, mult)
    return t


def _softplus(z):
    return jnp.maximum(z, 0.0) + jnp.log1p(jnp.exp(-jnp.abs(z)))


def _head_rmsnorm(blk, gain):
    ms = jnp.mean(blk * blk, axis=-1, keepdims=True)
    return blk * lax.rsqrt(ms + EPS) * gain


def _norm_matmul_kernel(*refs, head_norm, head_major, n_out):
    x_ref, g_ref, w_ref = refs[:3]
    pos = 3
    hg_ref = None
    if head_norm:
        hg_ref = refs[pos]
        pos += 1
    o_refs = refs[pos:pos + n_out]
    xn_ref = refs[pos + n_out]

    @pl.when(pl.program_id(1) == 0)
    def _():
        x = x_ref[...]
        ms = jnp.mean(x * x, axis=-1, keepdims=True)
        xn_ref[...] = (x * lax.rsqrt(ms + EPS) * g_ref[...]).astype(BF16)

    acc = jnp.dot(xn_ref[...], w_ref[...], preferred_element_type=F32)
    tn = acc.shape[1]
    if head_norm or head_major:
        for h in range(tn // HEAD):
            cs = slice(h * HEAD, (h + 1) * HEAD)
            y = _head_rmsnorm(acc[:, cs], hg_ref[:, cs]) if head_norm else acc[:, cs]
            for o_ref in o_refs:
                if head_major:
                    o_ref[0, h] = y.astype(o_ref.dtype)
                else:
                    o_ref[:, cs] = y.astype(o_ref.dtype)
    else:
        for o_ref in o_refs:
            o_ref[...] = acc.astype(o_ref.dtype)


def norm_matmul(x, g, w, *, head_gain=None, out_dtypes=(BF16,), head_major_seq=None, tm_pref=1024,
                tn_pref=1024):
    m, d = x.shape
    n = w.shape[1]
    tm = _tile(m, tm_pref, SUBLANE)
    tn = _tile(n, tn_pref, 2 * LANE)
    if head_major_seq is None:
        out_specs = [pl.BlockSpec((tm, tn), lambda i, j: (i, j)) for _ in out_dtypes]
        out_shape = [jax.ShapeDtypeStruct((m, n), dt) for dt in out_dtypes]
    else:
        seq = head_major_seq
        assert seq % tm == 0 and m % seq == 0
        per_seq = seq // tm
        out_specs = [pl.BlockSpec((1, tn // HEAD, tm, HEAD), lambda i, j: (i // per_seq, j, i % per_seq, 0))
                     for _ in out_dtypes]
        out_shape = [jax.ShapeDtypeStruct((m // seq, n // HEAD, seq, HEAD), dt) for dt in out_dtypes]
    in_specs = [
        pl.BlockSpec((tm, d), lambda i, j: (i, 0)),
        pl.BlockSpec((1, d), lambda i, j: (0, 0)),
        pl.BlockSpec((d, tn), lambda i, j: (0, j)),
    ]
    args = [x, g.reshape(1, d).astype(F32), w]
    if head_gain is not None:
        in_specs.append(pl.BlockSpec((1, tn), lambda i, j: (0, j)))
        args.append(head_gain.reshape(1, n).astype(F32))
    outs = pl.pallas_call(
        functools.partial(_norm_matmul_kernel, head_norm=head_gain is not None,
                          head_major=head_major_seq is not None, n_out=len(out_dtypes)),
        grid=(m // tm, n // tn),
        in_specs=in_specs,
        out_specs=out_specs,
        out_shape=out_shape,
        scratch_shapes=[pltpu.VMEM((tm, d), BF16)],
        compiler_params=_params("parallel", "arbitrary"),
        name="norm_matmul",
    )(*args)
    return outs[0] if len(out_dtypes) == 1 else tuple(outs)


def _matmul_residual_kernel(*refs, n_pairs):
    x_ref = refs[2 * n_pairs]
    o_ref = refs[2 * n_pairs + 1]

    @pl.when(pl.program_id(2) == 0)
    def _():
        o_ref[...] = x_ref[...]

    acc = jnp.dot(refs[0][...], refs[1][...], preferred_element_type=F32)
    for p in range(1, n_pairs):
        acc += jnp.dot(refs[2 * p][...], refs[2 * p + 1][...], preferred_element_type=F32)
    o_ref[...] += acc


def matmul_residual(pairs, x, *, n_k=1, tm_pref=1024, tn_pref=1024):
    m, n = x.shape
    tm = _tile(m, tm_pref, SUBLANE)
    tn = _tile(n, tn_pref, 2 * LANE)
    in_specs, args = [], []
    for a, w in pairs:
        kp = a.shape[1]
        assert kp % n_k == 0 and w.shape == (kp, n)
        tk = kp // n_k
        in_specs.append(pl.BlockSpec((tm, tk), lambda i, j, k: (i, k)))
        in_specs.append(pl.BlockSpec((tk, tn), lambda i, j, k: (k, j)))
        args += [a, w]
    in_specs.append(pl.BlockSpec((tm, tn), lambda i, j, k: (i, j)))
    args.append(x)
    return pl.pallas_call(
        functools.partial(_matmul_residual_kernel, n_pairs=len(pairs)),
        grid=(m // tm, n // tn, n_k),
        in_specs=in_specs,
        out_specs=pl.BlockSpec((tm, tn), lambda i, j, k: (i, j)),
        out_shape=jax.ShapeDtypeStruct((m, n), F32),
        compiler_params=_params("parallel", "parallel", "arbitrary"),
        name="matmul_residual",
    )(*args)


def _rglru_kernel(p_ref, h0_ref, cb0_ref, cw_ref, cbias_ref, wgx_ref, bgx_ref, wga_ref, bga_ref, lam_ref,
                  y_ref, hl_ref, cbo_ref, ext_ref, hc_ref, *, tr, tile_last, r_last):
    t = pl.program_id(1)
    halo = SUBLANE

    @pl.when(t == 0)
    def _():
        ext_ref[0:halo, :] = jnp.zeros((halo, D_RNN), F32)
        ext_ref[halo - (RG_CONV - 1):halo, :] = cb0_ref[0]
        hc_ref[...] = h0_ref[0]

    ext_ref[halo:halo + tr, :] = p_ref[0, :, 0:D_RNN].astype(F32)
    sub_row = lax.broadcasted_iota(jnp.int32, (tr // SUBLANE, SUBLANE, HEAD), 1)
    for h in range(RG_HEADS):
        cs = slice(h * HEAD, (h + 1) * HEAD)
        xc = cbias_ref[:, cs] + ext_ref[halo - 3:halo - 3 + tr, cs] * cw_ref[0:1, cs]
        for j in range(1, RG_CONV):
            xc = xc + ext_ref[halo - 3 + j:halo - 3 + j + tr, cs] * cw_ref[j:j + 1, cs]
        xcb = xc.astype(BF16)
        gx = jax.nn.sigmoid(jnp.dot(xcb, wgx_ref[h], preferred_element_type=F32) + bgx_ref[:, cs])
        ga = jax.nn.sigmoid(jnp.dot(xcb, wga_ref[h], preferred_element_type=F32) + bga_ref[:, cs])
        log_a = (-RG_C) * ga * _softplus(-lam_ref[:, cs])
        a = jnp.exp(log_a)
        u = jnp.sqrt(-jnp.tanh(log_a) * (1.0 + a * a)) * (gx * xc)
        groups = tr // SUBLANE
        a3 = a.reshape(groups, SUBLANE, HEAD)
        u3 = u.reshape(groups, SUBLANE, HEAD)
        d = 1
        while d < SUBLANE:
            live = sub_row >= d
            u3 = jnp.where(live, a3 * pltpu.roll(u3, d, 1) + u3, u3)
            a3 = jnp.where(live, a3 * pltpu.roll(a3, d, 1), a3)
            d *= 2
        h_prev = jnp.broadcast_to(hc_ref[:, cs], (SUBLANE, HEAD))
        h_groups = []
        for k in range(groups):
            h_k = a3[k] * h_prev + u3[k]
            h_groups.append(h_k)
            h_prev = jnp.broadcast_to(h_k[SUBLANE - 1:SUBLANE, :], (SUBLANE, HEAD))
        u = jnp.concatenate(h_groups, axis=0)
        gate = jax.nn.gelu(p_ref[0, :, D_RNN + h * HEAD:D_RNN + (h + 1) * HEAD].astype(F32))
        y_ref[0, :, cs] = (u * gate).astype(y_ref.dtype)
        hc_ref[:, cs] = u[tr - 1:tr, :]

        @pl.when(t == tile_last)
        def _():
            hl_ref[0, :, cs] = u[r_last:r_last + 1, :]

    @pl.when(t == tile_last)
    def _():
        cbo_ref[0] = ext_ref[halo + r_last - 2:halo + r_last + 1, :]

    if tr >= halo:
        ext_ref[0:halo, :] = ext_ref[tr:tr + halo, :]


def rglru(p, h0, cb0, cw, cbias, wgx, bgx, wga, bga, lam, *, t_valid):
    b, t, _ = p.shape
    tr = _tile(t, 256, 16)
    row = lambda v: v.reshape(1, D_RNN).astype(F32)
    full = lambda shape: pl.BlockSpec(shape, lambda i, j: (0,) * len(shape))
    y, hl, cbo = pl.pallas_call(
        functools.partial(_rglru_kernel, tr=tr, tile_last=(t_valid - 1) // tr, r_last=(t_valid - 1) % tr),
        grid=(b, t // tr),
        in_specs=[
            pl.BlockSpec((1, tr, 2 * D_RNN), lambda i, j: (i, j, 0)),
            pl.BlockSpec((1, 1, D_RNN), lambda i, j: (i, 0, 0)),
            pl.BlockSpec((1, RG_CONV - 1, D_RNN), lambda i, j: (i, 0, 0)),
            full((RG_CONV, D_RNN)), full((1, D_RNN)),
            full((RG_HEADS, HEAD, HEAD)), full((1, D_RNN)),
            full((RG_HEADS, HEAD, HEAD)), full((1, D_RNN)),
            full((1, D_RNN)),
        ],
        out_specs=[
            pl.BlockSpec((1, tr, D_RNN), lambda i, j: (i, j, 0)),
            pl.BlockSpec((1, 1, D_RNN), lambda i, j: (i, 0, 0)),
            pl.BlockSpec((1, RG_CONV - 1, D_RNN), lambda i, j: (i, 0, 0)),
        ],
        out_shape=[
            jax.ShapeDtypeStruct((b, t, D_RNN), BF16),
            jax.ShapeDtypeStruct((b, 1, D_RNN), F32),
            jax.ShapeDtypeStruct((b, RG_CONV - 1, D_RNN), F32),
        ],
        scratch_shapes=[pltpu.VMEM((tr + SUBLANE, D_RNN), F32), pltpu.VMEM((1, D_RNN), F32)],
        compiler_params=_params("parallel", "arbitrary"),
        name="rglru",
    )(p, h0.reshape(b, 1, D_RNN), cb0, cw.astype(F32), row(cbias), wgx, row(bgx), wga, row(bga), row(lam))
    return y, hl.reshape(b, D_RNN), cbo


def _mem_attn_kernel(q_ref, k_ref, v_ref, o_ref):
    for h in range(MEM_HEADS):
        cs = slice(h * HEAD, (h + 1) * HEAD)
        q = q_ref[0, :, cs]
        k = k_ref[0, :, cs].astype(BF16)
        v = v_ref[0, :, cs].astype(BF16)
        s = lax.dot_general(q, k, (((1,), (1,)), ((), ())), preferred_element_type=F32)
        e = jnp.exp(s - jnp.max(s, axis=-1, keepdims=True))
        p = e / jnp.sum(e, axis=-1, keepdims=True)
        o_ref[0, :, cs] = jnp.dot(p.astype(BF16), v, preferred_element_type=F32).astype(o_ref.dtype)


def mem_attn(q, q_col_block, mk, mv):
    b, t, _ = q.shape
    tq = _tile(t, 512, 16)
    return pl.pallas_call(
        _mem_attn_kernel,
        grid=(b, t // tq),
        in_specs=[
            pl.BlockSpec((1, tq, D_MEM), lambda i, j: (i, j, q_col_block)),
            pl.BlockSpec((1, MEM_TOKENS, D_MEM), lambda i, j: (i, 0, 0)),
            pl.BlockSpec((1, MEM_TOKENS, D_MEM), lambda i, j: (i, 0, 0)),
        ],
        out_specs=pl.BlockSpec((1, tq, D_MEM), lambda i, j: (i, j, 0)),
        out_shape=jax.ShapeDtypeStruct((b, t, D_MEM), BF16),
        compiler_params=_params("parallel", "parallel"),
        name="mem_attn",
    )(q, mk, mv)


def _shift_rows(u, shift, head_rows):
    s = pltpu.roll(u, shift, 0)
    row = lax.broadcasted_iota(jnp.int32, (SUBLANE, u.shape[1]), 0)
    top = s[0:SUBLANE]
    for r, v in enumerate(head_rows):
        top = jnp.where(row == r, v, top)
    return jnp.concatenate([top, s[SUBLANE:]], axis=0)


def _conv_gelu_gate(u, gate, prev2, prev1, cw_ref, cb_ref):
    uc = (cb_ref[...] + _shift_rows(u, 2, [prev2, prev1]) * cw_ref[0:1, :]
          + _shift_rows(u, 1, [prev1]) * cw_ref[1:2, :] + u * cw_ref[2:3, :])
    return jax.nn.gelu(uc) * gate


def _ffn_up_act_kernel(x_ref, g_ref, wu_ref, wg_ref, b0_ref, cw_ref, cb_ref, act_ref, nb_ref, xn_ref, tail_ref,
                       *, seq, tiles_per_seq, t_valid):
    i = pl.program_id(0)
    j = pl.program_id(1)
    tm = x_ref.shape[0]

    @pl.when(j == 0)
    def _():
        x = x_ref[...]
        ms = jnp.mean(x * x, axis=-1, keepdims=True)
        xn_ref[...] = (x * lax.rsqrt(ms + EPS) * g_ref[...]).astype(BF16)

    if tiles_per_seq >= 1:
        @pl.when(i == 0)
        def _():
            tail_ref[j] = jnp.zeros(tail_ref.shape[1:], F32)

        first = (i % tiles_per_seq) == 0
        prev2 = jnp.where(first, b0_ref[0, 0:1, :], tail_ref[j, SUBLANE - 2:SUBLANE - 1, :])
        prev1 = jnp.where(first, b0_ref[0, 1:2, :], tail_ref[j, SUBLANE - 1:SUBLANE, :])
        rc = min(tm, FFN_ROW_CHUNK)
        for c in range(tm // rc):
            xs = xn_ref[c * rc:(c + 1) * rc, :]
            u = jnp.dot(xs, wu_ref[...], preferred_element_type=F32)
            gate = jnp.dot(xs, wg_ref[...], preferred_element_type=F32)
            act_ref[c * rc:(c + 1) * rc, :] = _conv_gelu_gate(u, gate, prev2, prev1, cw_ref, cb_ref
                                                              ).astype(act_ref.dtype)
            prev2, prev1 = u[rc - 2:rc - 1], u[rc - 1:rc]
        tail_ref[j] = u[rc - SUBLANE:rc]

        @pl.when((i % tiles_per_seq) == tiles_per_seq - 1)
        def _():
            nb_ref[i // tiles_per_seq, j] = u[rc - 2:rc]
    else:
        xn = xn_ref[...]
        u = jnp.dot(xn, wu_ref[...], preferred_element_type=F32)
        gate = jnp.dot(xn, wg_ref[...], preferred_element_type=F32)
        for b in range(tm // seq):
            rs = slice(b * seq, (b + 1) * seq)
            a = _conv_gelu_gate(u[rs], gate[rs], b0_ref[b, 0:1, :], b0_ref[b, 1:2, :], cw_ref, cb_ref)
            act_ref[rs, :] = a.astype(act_ref.dtype)
            nb_ref[(tm // seq) * i + b, j] = u[b * seq + t_valid - 2:b * seq + t_valid]


def ffn_up_act(x, g, w_up, buf0, cw, cb, *, seq, t_valid, tm_pref=1024, tn=512):
    m, d = x.shape
    b = m // seq
    tm = _tile(m, tm_pref, SUBLANE)
    assert t_valid >= FFN_CONV - 1 and (seq % tm == 0 or tm % seq == 0)
    tiles_per_seq = seq // tm if tm <= seq else 0
    assert tiles_per_seq == 0 or t_valid == seq
    seqs_per_tile = max(tm // seq, 1)
    n_j = D_FF // tn
    act, nb = pl.pallas_call(
        functools.partial(_ffn_up_act_kernel, seq=seq, tiles_per_seq=tiles_per_seq, t_valid=t_valid),
        grid=(m // tm, n_j),
        in_specs=[
            pl.BlockSpec((tm, d), lambda i, j: (i, 0)),
            pl.BlockSpec((1, d), lambda i, j: (0, 0)),
            pl.BlockSpec((d, tn), lambda i, j: (0, j)),
            pl.BlockSpec((d, tn), lambda i, j: (0, n_j + j)),
            pl.BlockSpec((seqs_per_tile, FFN_CONV - 1, tn),
                         lambda i, j: ((i // tiles_per_seq) if tiles_per_seq else i, 0, j)),
            pl.BlockSpec((FFN_CONV, tn), lambda i, j: (0, j)),
            pl.BlockSpec((1, tn), lambda i, j: (0, j)),
        ],
        out_specs=[
            pl.BlockSpec((tm, tn), lambda i, j: (i, j)),
            pl.BlockSpec((b, n_j, FFN_CONV - 1, tn), lambda i, j: (0, 0, 0, 0)),
        ],
        out_shape=[
            jax.ShapeDtypeStruct((m, D_FF), BF16),
            jax.ShapeDtypeStruct((b, n_j, FFN_CONV - 1, tn), F32),
        ],
        scratch_shapes=[pltpu.VMEM((tm, d), BF16), pltpu.VMEM((n_j, SUBLANE, tn), F32)],
        compiler_params=_params("arbitrary", "arbitrary"),
        name="ffn_up_act",
    )(x, g.reshape(1, d).astype(F32), w_up, w_up, buf0, cw.astype(F32), cb.reshape(1, D_FF).astype(F32))
    return act, nb.transpose(0, 2, 1, 3).reshape(b, FFN_CONV - 1, D_FF)


def _suffix_matrix():
    j = lax.broadcasted_iota(jnp.int32, (2 * HEAD, 2 * HEAD), 0) % HEAD
    s = lax.broadcasted_iota(jnp.int32, (2 * HEAD, 2 * HEAD), 1)
    return jnp.where((s >= HEAD) | (j >= s), 1.0, 0.0).astype(BF16)


def _sb_weights(s, neg_bias, carry, suffix, mask):
    nz, lk = _sb_log_keep(s, neg_bias, mask)
    return _sb_finish(nz, _sb_suffix_sums(lk, suffix), carry, mask)


def _sb_log_keep(s, neg_bias, mask):
    nz = neg_bias - s
    neg_abs = pltpu.bitcast(pltpu.bitcast(nz, jnp.uint32) | jnp.uint32(0x80000000), F32)
    lk = jnp.minimum(nz, 0.0) - jnp.log2(1.0 + jnp.exp2(neg_abs))
    if mask is not None:
        lk = jnp.where(mask, lk, 0.0)
    return nz, lk


def _sb_suffix_sums(lk, suffix):
    hi = lk.astype(BF16)
    lo = (lk - hi.astype(F32)).astype(BF16)
    return jnp.dot(jnp.concatenate([hi, lo], axis=1), suffix, preferred_element_type=F32)


def _sb_finish(nz, c2, carry, mask):
    w = jnp.exp2((carry + c2[:, :HEAD]) - nz)
    if mask is not None:
        w = jnp.where(mask, w, 0.0)
    return w, carry + c2[:, HEAD:]


def _nt_dot(a, b):
    return lax.dot_general(a, b, (((1,), (1,)), ((), ())), preferred_element_type=F32)


def _sb_prompt_kernel(nbias_ref, q_ref, k_ref, v_ref, o_ref, carry_ref, acc_ref, *, tq, hp):
    hg = pl.program_id(1)
    qi = pl.program_id(2)
    suffix = _suffix_matrix()
    n_sub = tq // HEAD
    t_loc = lax.broadcasted_iota(jnp.int32, (tq, HEAD), 0)
    s_loc = lax.broadcasted_iota(jnp.int32, (tq, HEAD), 1)
    carry_ref[...] = jnp.zeros(carry_ref.shape, F32)
    acc_ref[...] = jnp.zeros(acc_ref.shape, F32)

    def block(start, masked):
        for e in range(hp):
            neg_bias = nbias_ref[hg * hp + e]
            s = _nt_dot(q_ref[0, :, e * HEAD:(e + 1) * HEAD], k_ref[0, e, pl.ds(start, tq), :])
            carry = carry_ref[e]
            ws = [None] * n_sub
            for j in range(n_sub - 1, -1, -1):
                mask = ((s_loc + j * HEAD) < t_loc) if masked else None
                w, carry = _sb_weights(s[:, j * HEAD:(j + 1) * HEAD], neg_bias, carry, suffix, mask)
                ws[j] = w.astype(BF16)
            carry_ref[e] = carry
            acc_ref[e] += jnp.dot(jnp.concatenate(ws, axis=1), v_ref[0, e, pl.ds(start, tq), :],
                                  preferred_element_type=F32)

    block(pl.multiple_of(qi * tq, tq), True)

    @pl.loop(0, qi)
    def _(i):
        block(pl.multiple_of((qi - 1 - i) * tq, tq), False)

    for e in range(hp):
        o_ref[0, :, e * HEAD:(e + 1) * HEAD] = acc_ref[e].astype(o_ref.dtype)


def sb_prompt(q, k, v, bias):
    b, _, t, _ = k.shape
    tq = _tile(t, 512, HEAD)
    hp = SB_HEADS_PER_STEP
    return pl.pallas_call(
        functools.partial(_sb_prompt_kernel, tq=tq, hp=hp),
        grid_spec=pltpu.PrefetchScalarGridSpec(
            num_scalar_prefetch=1,
            grid=(b, SB_HEADS // hp, t // tq),
            in_specs=[
                pl.BlockSpec((1, tq, hp * HEAD), lambda i, h, j, nb: (i, j, h)),
                pl.BlockSpec((1, hp, t, HEAD), lambda i, h, j, nb: (i, h, 0, 0)),
                pl.BlockSpec((1, hp, t, HEAD), lambda i, h, j, nb: (i, h, 0, 0)),
            ],
            out_specs=pl.BlockSpec((1, tq, hp * HEAD), lambda i, h, j, nb: (i, j, h)),
            scratch_shapes=[pltpu.VMEM((hp, tq, HEAD), F32), pltpu.VMEM((hp, tq, HEAD), F32)],
        ),
        out_shape=jax.ShapeDtypeStruct((b, t, D_RNN), BF16),
        compiler_params=_params("parallel", "parallel", "arbitrary"),
        name="sb_prompt",
    )(-LOG2E * bias.astype(F32), q, k, v)


def _sb_sample_kernel(pt_ref, q_ref, kn_ref, vn_ref, bias_ref, *rest, pages_per_step, t_valid):
    kp_refs = rest[:pages_per_step]
    vp_refs = rest[pages_per_step:2 * pages_per_step]
    o_ref, carry_ref, acc_ref = rest[2 * pages_per_step:]
    g = pl.program_id(1)
    rows = SB_HEADS * Q_ROWS
    suffix = _suffix_matrix()
    neg_bias = bias_ref[...]
    hcols = [slice(h * HEAD, (h + 1) * HEAD) for h in range(SB_HEADS)]
    q_heads = [q_ref[0, :, hcols[h]] for h in range(SB_HEADS)]

    def update(blocks, mask):
        n = len(blocks)
        nzs, lks = [], []
        for k_heads, _ in blocks:
            s = jnp.concatenate([_nt_dot(q_heads[h], k_heads[h])[0:Q_ROWS] for h in range(SB_HEADS)], axis=0)
            nz, lk = _sb_log_keep(s, neg_bias, mask)
            nzs.append(nz)
            lks.append(lk)
        c2 = _sb_suffix_sums(jnp.concatenate(lks, axis=0), suffix)
        carry = carry_ref[...]
        ws = []
        for r in range(n):
            w, carry = _sb_finish(nzs[r], c2[r * rows:(r + 1) * rows], carry, mask)
            ws.append(w)
        carry_ref[...] = carry
        zpad = jnp.zeros((SAMPLE_T_PAD - Q_ROWS, HEAD), F32)
        for h in range(SB_HEADS):
            rs = slice(h * Q_ROWS, (h + 1) * Q_ROWS)
            pv = None
            for r in range(n):
                wh = jnp.concatenate([ws[r][rs], zpad], axis=0).astype(BF16)
                d = jnp.dot(wh, blocks[r][1][h], preferred_element_type=F32)
                pv = d if pv is None else pv + d
            acc_ref[rs, :] += pv[0:Q_ROWS]

    @pl.when(g == 0)
    def _():
        carry_ref[...] = jnp.zeros((rows, HEAD), F32)
        acc_ref[...] = jnp.zeros((rows, HEAD), F32)
        kz = jnp.zeros((HEAD - SAMPLE_T_PAD, HEAD), BF16)
        k_heads = [jnp.concatenate([kn_ref[0, :, hcols[h]], kz], axis=0) for h in range(SB_HEADS)]
        v_heads = [jnp.concatenate([vn_ref[0, :, hcols[h]], kz], axis=0) for h in range(SB_HEADS)]
        i_q = lax.broadcasted_iota(jnp.int32, (rows, HEAD), 0) % Q_ROWS
        j_k = lax.broadcasted_iota(jnp.int32, (rows, HEAD), 1)
        update([(k_heads, v_heads)], (j_k < i_q) & (j_k < t_valid))

    update([([kp_refs[r][0, h].astype(BF16) for h in range(SB_HEADS)],
             [vp_refs[r][0, h].astype(BF16) for h in range(SB_HEADS)]) for r in range(pages_per_step)], None)

    @pl.when(g == pl.num_programs(1) - 1)
    def _():
        z8 = jnp.zeros((SAMPLE_T_PAD - Q_ROWS, HEAD), F32)
        for h in range(SB_HEADS):
            blk = jnp.concatenate([acc_ref[h * Q_ROWS:(h + 1) * Q_ROWS, :], z8], axis=0)
            o_ref[0, :, h * HEAD:(h + 1) * HEAD] = blk.astype(o_ref.dtype)


def sb_sample(q, k_new, v_new, cache_k, cache_v, page_table, bias, *, t_valid, pages_per_step=8):
    b, n_pages = page_table.shape
    page = cache_k.shape[2]
    assert page == HEAD and n_pages % pages_per_step == 0 and t_valid <= Q_ROWS
    assert cache_k.shape[1:] == (SB_HEADS, page, HEAD)
    rows = SB_HEADS * Q_ROWS
    bias_rows = jnp.broadcast_to(jnp.repeat(-LOG2E * bias.astype(F32), Q_ROWS)[:, None], (rows, HEAD))

    def page_spec(r):
        return pl.BlockSpec(
            (1, SB_HEADS, page, HEAD),
            lambda i, g, pt: (pt[i, n_pages - 1 - (g * pages_per_step + r)], 0, 0, 0))

    return pl.pallas_call(
        functools.partial(_sb_sample_kernel, pages_per_step=pages_per_step, t_valid=t_valid),
        grid_spec=pltpu.PrefetchScalarGridSpec(
            num_scalar_prefetch=1,
            grid=(b, n_pages // pages_per_step),
            in_specs=[
                pl.BlockSpec((1, SAMPLE_T_PAD, D_RNN), lambda i, g, pt: (i, 0, 0)),
                pl.BlockSpec((1, SAMPLE_T_PAD, D_RNN), lambda i, g, pt: (i, 0, 0)),
                pl.BlockSpec((1, SAMPLE_T_PAD, D_RNN), lambda i, g, pt: (i, 0, 0)),
                pl.BlockSpec((rows, HEAD), lambda i, g, pt: (0, 0)),
            ] + [page_spec(r) for r in range(pages_per_step)] + [page_spec(r) for r in range(pages_per_step)],
            out_specs=pl.BlockSpec((1, SAMPLE_T_PAD, D_RNN), lambda i, g, pt: (i, 0, 0)),
            scratch_shapes=[pltpu.VMEM((rows, HEAD), F32), pltpu.VMEM((rows, HEAD), F32)],
        ),
        out_shape=jax.ShapeDtypeStruct((b, SAMPLE_T_PAD, D_RNN), BF16),
        compiler_params=_params("parallel", "arbitrary"),
        name="sb_sample",
    )(page_table, q, k_new, v_new, bias_rows, *([cache_k] * pages_per_step), *([cache_v] * pages_per_step))


def _trunk(x3, t_valid, rg_h0, rg_conv0, ffn_conv0, mem_k, mem_v, sb_attend, head_major_kv, P, W):
    b, t, d = x3.shape
    m = b * t
    x = x3.reshape(m, d)
    rg_h_out, rg_conv_out, ffn_conv_out = [], [], []
    k_f32 = v_f32 = k_bf = v_bf = None
    for l in range(DEPTH):
        mem_gain = jnp.tile(P['mem_q_norm'][l] * ATTN_SCALE, MEM_HEADS)
        if l < N_A_LAYERS:
            p_rg = norm_matmul(x, P['g_mix'][l], W['w_in_a_rg'][l])
            qm = norm_matmul(x, P['g_mix'][l], W['w_in_a_qm'][l], head_gain=mem_gain)
            y_tok, h_last, cbuf = rglru(
                p_rg.reshape(b, t, 2 * D_RNN), rg_h0[l], rg_conv0[l], P['rg_conv_w'][l], P['rg_conv_b'][l],
                W['rg_gate_x_w'][l], P['rg_gate_x_b'][l], W['rg_gate_a_w'][l], P['rg_gate_a_b'][l],
                P['rg_lambda'][l], t_valid=t_valid)
            rg_h_out.append(h_last)
            rg_conv_out.append(cbuf)
            q3, q_col = qm.reshape(b, t, D_MEM), 0
        else:
            j = l - N_A_LAYERS
            gain = jnp.concatenate([jnp.tile(P['sb_q_norm'][j] * (ATTN_SCALE * LOG2E), SB_HEADS), mem_gain])
            pq = norm_matmul(x, P['g_mix'][l], W['w_in_b'][j], head_gain=gain)
            q3, q_col = pq.reshape(b, t, D_RNN + D_MEM), D_RNN // D_MEM
            y_tok = sb_attend(q3, k_bf, v_bf, P['sb_beta_bias'][j])
        y_mem = mem_attn(q3, q_col, mem_k[l], mem_v[l])
        x = matmul_residual([(y_tok.reshape(m, D_RNN), W['w_out_tok'][l]),
                             (y_mem.reshape(m, D_MEM), W['w_out_mem'][l])], x)
        act, fbuf = ffn_up_act(x, P['g_ffn'][l], W['w_ffn_up'][l], ffn_conv0[l], P['ffn_conv_w'][l],
                               P['ffn_conv_b'][l], seq=t, t_valid=t_valid)
        x = matmul_residual([(act, W['w_ffn_down'][l])], x, n_k=3)
        ffn_conv_out.append(fbuf)
        if l == N_A_LAYERS - 1:
            k_gain = jnp.tile(P['kv_k_norm'], SB_HEADS)
            hm = t if head_major_kv else None
            k_f32, k_bf = norm_matmul(x, P['kv_norm'], W['w_k'], head_gain=k_gain, out_dtypes=(F32, BF16),
                                      head_major_seq=hm)
            v_f32, v_bf = norm_matmul(x, P['kv_norm'], W['w_v'], out_dtypes=(F32, BF16), head_major_seq=hm)
            if not head_major_kv:
                k_bf = k_bf.reshape(b, t, D_RNN)
                v_bf = v_bf.reshape(b, t, D_RNN)
    if head_major_kv:
        k_out, v_out = k_f32.transpose(0, 2, 1, 3), v_f32.transpose(0, 2, 1, 3)
    else:
        k_out, v_out = k_f32.reshape(b, t, SB_HEADS, HEAD), v_f32.reshape(b, t, SB_HEADS, HEAD)
    return (x.reshape(b, t, d), jnp.stack(rg_h_out), jnp.stack(rg_conv_out), jnp.stack(ffn_conv_out),
            k_out, v_out)


def kernel(x_prompt, x_sample, mem_prompt, state_rglru_h, state_rglru_conv, state_ffn_conv, cache_mem_k, cache_mem_v, cache_sb_k, cache_sb_v, page_table, g_mix, g_ffn, w_in_a, rg_conv_w, rg_conv_b, rg_gate_x_w, rg_gate_x_b, rg_gate_a_w, rg_gate_a_b, rg_lambda, w_in_b, sb_q_norm, sb_beta_bias, kv_norm, w_kv, kv_k_norm, mem_norm, w_mem_kv, mem_q_norm, mem_k_norm, w_out, w_ffn_up, ffn_conv_w, ffn_conv_b, w_ffn_down):
    P = {'g_mix': g_mix, 'g_ffn': g_ffn, 'rg_conv_w': rg_conv_w, 'rg_conv_b': rg_conv_b,
         'rg_gate_x_b': rg_gate_x_b, 'rg_gate_a_b': rg_gate_a_b, 'rg_lambda': rg_lambda,
         'sb_q_norm': sb_q_norm, 'sb_beta_bias': sb_beta_bias, 'kv_norm': kv_norm,
         'kv_k_norm': kv_k_norm, 'mem_q_norm': mem_q_norm,
         'ffn_conv_w': ffn_conv_w, 'ffn_conv_b': ffn_conv_b}
    W = {'w_in_a_rg': w_in_a[:, :, :2 * D_RNN].astype(BF16),
         'w_in_a_qm': w_in_a[:, :, 2 * D_RNN:].astype(BF16),
         'rg_gate_x_w': rg_gate_x_w.astype(BF16), 'rg_gate_a_w': rg_gate_a_w.astype(BF16),
         'w_in_b': w_in_b.astype(BF16),
         'w_k': w_kv[:, :D_RNN].astype(BF16), 'w_v': w_kv[:, D_RNN:].astype(BF16),
         'w_out_tok': w_out[:, :D_RNN].astype(BF16), 'w_out_mem': w_out[:, D_RNN:].astype(BF16),
         'w_ffn_up': w_ffn_up.astype(BF16), 'w_ffn_down': w_ffn_down.astype(BF16)}
    w_mem_k = w_mem_kv[:, :, :D_MEM].astype(BF16)
    w_mem_v = w_mem_kv[:, :, D_MEM:].astype(BF16)

    bp, seq, d = x_prompt.shape
    mem2 = mem_prompt.reshape(bp * MEM_TOKENS, d)
    mk_list, mv_list = [], []
    for l in range(DEPTH):
        mk_list.append(norm_matmul(mem2, mem_norm[l], w_mem_k[l], head_gain=jnp.tile(mem_k_norm[l], MEM_HEADS),
                                   out_dtypes=(F32,)))
        mv_list.append(norm_matmul(mem2, mem_norm[l], w_mem_v[l], out_dtypes=(F32,)))
    prompt_mem_k = jnp.stack(mk_list).reshape(DEPTH, bp, MEM_TOKENS, D_MEM)
    prompt_mem_v = jnp.stack(mv_list).reshape(DEPTH, bp, MEM_TOKENS, D_MEM)
    zeros_h = jnp.zeros((N_A_LAYERS, bp, D_RNN), F32)
    zeros_rc = jnp.zeros((N_A_LAYERS, bp, RG_CONV - 1, D_RNN), F32)
    zeros_fc = jnp.zeros((DEPTH, bp, FFN_CONV - 1, D_FF), F32)
    (y_prompt, prompt_rglru_h, prompt_rglru_conv, prompt_ffn_conv, prompt_sb_k, prompt_sb_v) = _trunk(
        x_prompt, seq, zeros_h, zeros_rc, zeros_fc, prompt_mem_k, prompt_mem_v, sb_prompt, True, P, W)

    db, dec_seq, _ = x_sample.shape
    xs = jnp.pad(x_sample, ((0, 0), (0, SAMPLE_T_PAD - dec_seq), (0, 0)))

    cache_k_hm = cache_sb_k.transpose(0, 2, 1, 3)
    cache_v_hm = cache_sb_v.transpose(0, 2, 1, 3)

    def sb_paged(q3, k_bf, v_bf, bias):
        return sb_sample(q3, k_bf, v_bf, cache_k_hm, cache_v_hm, page_table, bias, t_valid=dec_seq)

    (y_s, sample_rglru_h, sample_rglru_conv, sample_ffn_conv, s_k, s_v) = _trunk(
        xs, dec_seq, state_rglru_h, state_rglru_conv, state_ffn_conv,
        cache_mem_k.reshape(DEPTH, db, MEM_TOKENS, D_MEM), cache_mem_v.reshape(DEPTH, db, MEM_TOKENS, D_MEM),
        sb_paged, False, P, W)

    mem_shape = (DEPTH, bp, MEM_TOKENS, MEM_HEADS, HEAD)
    return (y_prompt, y_s[:, :dec_seq],
            prompt_rglru_h, prompt_rglru_conv, prompt_ffn_conv,
            prompt_sb_k, prompt_sb_v, prompt_mem_k.reshape(mem_shape), prompt_mem_v.reshape(mem_shape),
            sample_rglru_h, sample_rglru_conv, sample_ffn_conv,
            s_k[:, :dec_seq], s_v[:, :dec_seq])
```

```python
import functools
import math
from typing import NamedTuple

import jax
import jax.numpy as jnp
from jax import lax
from jax.experimental import pallas as pl
from jax.experimental.pallas import tpu as pltpu

F32 = jnp.float32
BF16 = jnp.bfloat16

LANE = 128
SUBLANE = 8
VMEM_LIMIT_BYTES = 56 * 1024 * 1024

D_MODEL = 2048
DEPTH = 4
N_A_LAYERS = 2
D_RNN = 1536
HEAD = 128
RG_HEADS = D_RNN // HEAD
RG_CONV = 4
RG_C = 8.0
SB_HEADS = D_RNN // HEAD
MEM_TOKENS = 256
MEM_HEADS = 4
D_MEM = MEM_HEADS * HEAD
D_FF = 3 * D_MODEL
FFN_CONV = 3
EPS = 1e-6
ATTN_SCALE = HEAD ** -0.5
LOG2E = math.log2(math.e)
SAMPLE_T_PAD = 16
Q_ROWS = 8
FFN_ROW_CHUNK = 256
SB_HEADS_PER_STEP = 3


def _params(*sem):
    return pltpu.CompilerParams(dimension_semantics=sem, vmem_limit_bytes=VMEM_LIMIT_BYTES)


def _tile(n, pref, mult):
    if n <= pref:
        return n
    t = (pref // mult) * mult
    while t > mult and n % t:
        t -= mult
    assert n % t == 0, (n, pref, mult)
    return t


class WeightView(NamedTuple):
    array: jax.Array
    layer: int | None
    row0: int
    rows: int
    col0: int
    cols: int


def _weight_spec(wv, tk, tn, row_block, col_block):
    assert wv.row0 % tk == 0 and wv.col0 % tn == 0 and wv.rows % tk == 0 and wv.cols % tn == 0
    r0, c0 = wv.row0 // tk, wv.col0 // tn
    if wv.layer is None:
        return pl.BlockSpec((tk, tn), lambda *g: (r0 + row_block(*g), c0 + col_block(*g)))
    layer = wv.layer
    return pl.BlockSpec((None, tk, tn), lambda *g: (layer, r0 + row_block(*g), c0 + col_block(*g)))


def _softplus(z):
    return jnp.maximum(z, 0.0) + jnp.log1p(jnp.exp(-jnp.abs(z)))


def _head_rmsnorm(blk, gain):
    ms = jnp.mean(blk * blk, axis=-1, keepdims=True)
    return blk * lax.rsqrt(ms + EPS) * gain


def _norm_matmul_kernel(*refs, head_norm, head_major, n_out):
    x_ref, g_ref, w_ref = refs[:3]
    pos = 3
    hg_ref = None
    if head_norm:
        hg_ref = refs[pos]
        pos += 1
    o_refs = refs[pos:pos + n_out]
    xn_ref = refs[pos + n_out]

    @pl.when(pl.program_id(1) == 0)
    def _():
        x = x_ref[...]
        ms = jnp.mean(x * x, axis=-1, keepdims=True)
        xn_ref[...] = (x * lax.rsqrt(ms + EPS) * g_ref[...]).astype(BF16)

    acc = jnp.dot(xn_ref[...], w_ref[...].astype(BF16), preferred_element_type=F32)
    tn = acc.shape[1]
    if head_norm or head_major:
        for h in range(tn // HEAD):
            cs = slice(h * HEAD, (h + 1) * HEAD)
            y = _head_rmsnorm(acc[:, cs], hg_ref[:, cs]) if head_norm else acc[:, cs]
            for o_ref in o_refs:
                if head_major:
                    o_ref[0, h] = y.astype(o_ref.dtype)
                else:
                    o_ref[:, cs] = y.astype(o_ref.dtype)
    else:
        for o_ref in o_refs:
            o_ref[...] = acc.astype(o_ref.dtype)


def norm_matmul(x, g, w, *, head_gain=None, out_dtypes=(BF16,), head_major_seq=None, tm_pref=1024,
                tn_pref=1024):
    m, d = x.shape
    n = w.cols
    assert w.rows == d
    tm = _tile(m, tm_pref, SUBLANE)
    tn = _tile(n, tn_pref, 2 * LANE)
    if head_major_seq is None:
        out_specs = [pl.BlockSpec((tm, tn), lambda i, j: (i, j)) for _ in out_dtypes]
        out_shape = [jax.ShapeDtypeStruct((m, n), dt) for dt in out_dtypes]
    else:
        seq = head_major_seq
        assert seq % tm == 0 and m % seq == 0
        per_seq = seq // tm
        out_specs = [pl.BlockSpec((1, tn // HEAD, tm, HEAD), lambda i, j: (i // per_seq, j, i % per_seq, 0))
                     for _ in out_dtypes]
        out_shape = [jax.ShapeDtypeStruct((m // seq, n // HEAD, seq, HEAD), dt) for dt in out_dtypes]
    in_specs = [
        pl.BlockSpec((tm, d), lambda i, j: (i, 0)),
        pl.BlockSpec((1, d), lambda i, j: (0, 0)),
        _weight_spec(w, d, tn, lambda i, j: 0, lambda i, j: j),
    ]
    args = [x, g.reshape(1, d).astype(F32), w.array]
    if head_gain is not None:
        in_specs.append(pl.BlockSpec((1, tn), lambda i, j: (0, j)))
        args.append(head_gain.reshape(1, n).astype(F32))
    outs = pl.pallas_call(
        functools.partial(_norm_matmul_kernel, head_norm=head_gain is not None,
                          head_major=head_major_seq is not None, n_out=len(out_dtypes)),
        grid=(m // tm, n // tn),
        in_specs=in_specs,
        out_specs=out_specs,
        out_shape=out_shape,
        scratch_shapes=[pltpu.VMEM((tm, d), BF16)],
        compiler_params=_params("parallel", "arbitrary"),
        name="norm_matmul",
    )(*args)
    return outs[0] if len(out_dtypes) == 1 else tuple(outs)


def _matmul_residual_kernel(*refs, n_pairs):
    x_ref = refs[2 * n_pairs]
    o_ref = refs[2 * n_pairs + 1]

    @pl.when(pl.program_id(2) == 0)
    def _():
        o_ref[...] = x_ref[...]

    acc = jnp.dot(refs[0][...], refs[1][...].astype(BF16), preferred_element_type=F32)
    for p in range(1, n_pairs):
        acc += jnp.dot(refs[2 * p][...], refs[2 * p + 1][...].astype(BF16), preferred_element_type=F32)
    o_ref[...] += acc


def matmul_residual(pairs, x, *, n_k=1, tm_pref=1024, tn_pref=1024):
    m, n = x.shape
    tm = _tile(m, tm_pref, SUBLANE)
    tn = _tile(n, tn_pref, 2 * LANE)
    in_specs, args = [], []
    for a, w in pairs:
        kp = a.shape[1]
        assert kp % n_k == 0 and (w.rows, w.cols) == (kp, n)
        tk = kp // n_k
        in_specs.append(pl.BlockSpec((tm, tk), lambda i, j, k: (i, k)))
        in_specs.append(_weight_spec(w, tk, tn, lambda i, j, k: k, lambda i, j, k: j))
        args += [a, w.array]
    in_specs.append(pl.BlockSpec((tm, tn), lambda i, j, k: (i, j)))
    args.append(x)
    return pl.pallas_call(
        functools.partial(_matmul_residual_kernel, n_pairs=len(pairs)),
        grid=(m // tm, n // tn, n_k),
        in_specs=in_specs,
        out_specs=pl.BlockSpec((tm, tn), lambda i, j, k: (i, j)),
        out_shape=jax.ShapeDtypeStruct((m, n), F32),
        compiler_params=_params("parallel", "parallel", "arbitrary"),
        name="matmul_residual",
    )(*args)


def _rglru_kernel(p_ref, h0_ref, cb0_ref, cw_ref, cbias_ref, wgx_ref, bgx_ref, wga_ref, bga_ref, lam_ref,
                  y_ref, hl_ref, cbo_ref, ext_ref, hc_ref, *, tr, tile_last, r_last):
    t = pl.program_id(1)
    halo = SUBLANE

    @pl.when(t == 0)
    def _():
        ext_ref[0:halo, :] = jnp.zeros((halo, D_RNN), F32)
        ext_ref[halo - (RG_CONV - 1):halo, :] = cb0_ref[0]
        hc_ref[...] = h0_ref[0]

    ext_ref[halo:halo + tr, :] = p_ref[0, :, 0:D_RNN].astype(F32)
    sub_row = lax.broadcasted_iota(jnp.int32, (tr // SUBLANE, SUBLANE, HEAD), 1)
    for h in range(RG_HEADS):
        cs = slice(h * HEAD, (h + 1) * HEAD)
        xc = cbias_ref[:, cs] + ext_ref[halo - 3:halo - 3 + tr, cs] * cw_ref[0:1, cs]
        for j in range(1, RG_CONV):
            xc = xc + ext_ref[halo - 3 + j:halo - 3 + j + tr, cs] * cw_ref[j:j + 1, cs]
        xcb = xc.astype(BF16)
        gx = jax.nn.sigmoid(jnp.dot(xcb, wgx_ref[h].astype(BF16), preferred_element_type=F32) + bgx_ref[:, cs])
        ga = jax.nn.sigmoid(jnp.dot(xcb, wga_ref[h].astype(BF16), preferred_element_type=F32) + bga_ref[:, cs])
        log_a = (-RG_C) * ga * _softplus(-lam_ref[:, cs])
        a = jnp.exp(log_a)
        u = jnp.sqrt(-jnp.tanh(log_a) * (1.0 + a * a)) * (gx * xc)
        groups = tr // SUBLANE
        a3 = a.reshape(groups, SUBLANE, HEAD)
        u3 = u.reshape(groups, SUBLANE, HEAD)
        d = 1
        while d < SUBLANE:
            live = sub_row >= d
            u3 = jnp.where(live, a3 * pltpu.roll(u3, d, 1) + u3, u3)
            a3 = jnp.where(live, a3 * pltpu.roll(a3, d, 1), a3)
            d *= 2
        h_prev = jnp.broadcast_to(hc_ref[:, cs], (SUBLANE, HEAD))
        h_groups = []
        for k in range(groups):
            h_k = a3[k] * h_prev + u3[k]
            h_groups.append(h_k)
            h_prev = jnp.broadcast_to(h_k[SUBLANE - 1:SUBLANE, :], (SUBLANE, HEAD))
        u = jnp.concatenate(h_groups, axis=0)
        gate = jax.nn.gelu(p_ref[0, :, D_RNN + h * HEAD:D_RNN + (h + 1) * HEAD].astype(F32))
        y_ref[0, :, cs] = (u * gate).astype(y_ref.dtype)
        hc_ref[:, cs] = u[tr - 1:tr, :]

        @pl.when(t == tile_last)
        def _():
            hl_ref[0, :, cs] = u[r_last:r_last + 1, :]

    @pl.when(t == tile_last)
    def _():
        cbo_ref[0] = ext_ref[halo + r_last - 2:halo + r_last + 1, :]

    if tr >= halo:
        ext_ref[0:halo, :] = ext_ref[tr:tr + halo, :]


def rglru(p, h0, cb0, cw, cbias, wgx, bgx, wga, bga, lam, *, t_valid):
    b, t, _ = p.shape
    tr = _tile(t, 256, 16)
    row = lambda v: v.reshape(1, D_RNN).astype(F32)
    full = lambda shape: pl.BlockSpec(shape, lambda i, j: (0,) * len(shape))
    y, hl, cbo = pl.pallas_call(
        functools.partial(_rglru_kernel, tr=tr, tile_last=(t_valid - 1) // tr, r_last=(t_valid - 1) % tr),
        grid=(b, t // tr),
        in_specs=[
            pl.BlockSpec((1, tr, 2 * D_RNN), lambda i, j: (i, j, 0)),
            pl.BlockSpec((1, 1, D_RNN), lambda i, j: (i, 0, 0)),
            pl.BlockSpec((1, RG_CONV - 1, D_RNN), lambda i, j: (i, 0, 0)),
            full((RG_CONV, D_RNN)), full((1, D_RNN)),
            full((RG_HEADS, HEAD, HEAD)), full((1, D_RNN)),
            full((RG_HEADS, HEAD, HEAD)), full((1, D_RNN)),
            full((1, D_RNN)),
        ],
        out_specs=[
            pl.BlockSpec((1, tr, D_RNN), lambda i, j: (i, j, 0)),
            pl.BlockSpec((1, 1, D_RNN), lambda i, j: (i, 0, 0)),
            pl.BlockSpec((1, RG_CONV - 1, D_RNN), lambda i, j: (i, 0, 0)),
        ],
        out_shape=[
            jax.ShapeDtypeStruct((b, t, D_RNN), BF16),
            jax.ShapeDtypeStruct((b, 1, D_RNN), F32),
            jax.ShapeDtypeStruct((b, RG_CONV - 1, D_RNN), F32),
        ],
        scratch_shapes=[pltpu.VMEM((tr + SUBLANE, D_RNN), F32), pltpu.VMEM((1, D_RNN), F32)],
        compiler_params=_params("parallel", "arbitrary"),
        name="rglru",
    )(p, h0.reshape(b, 1, D_RNN), cb0, cw.astype(F32), row(cbias), wgx, row(bgx), wga, row(bga), row(lam))
    return y, hl.reshape(b, D_RNN), cbo


def _mem_attn_kernel(q_ref, k_ref, v_ref, o_ref):
    for h in range(MEM_HEADS):
        cs = slice(h * HEAD, (h + 1) * HEAD)
        q = q_ref[0, :, cs]
        k = k_ref[0, :, cs].astype(BF16)
        v = v_ref[0, :, cs].astype(BF16)
        s = lax.dot_general(q, k, (((1,), (1,)), ((), ())), preferred_element_type=F32)
        e = jnp.exp(s - jnp.max(s, axis=-1, keepdims=True))
        p = e / jnp.sum(e, axis=-1, keepdims=True)
        o_ref[0, :, cs] = jnp.dot(p.astype(BF16), v, preferred_element_type=F32).astype(o_ref.dtype)


def mem_attn(q, q_col_block, mk, mv):
    b, t, _ = q.shape
    tq = _tile(t, 512, 16)
    return pl.pallas_call(
        _mem_attn_kernel,
        grid=(b, t // tq),
        in_specs=[
            pl.BlockSpec((1, tq, D_MEM), lambda i, j: (i, j, q_col_block)),
            pl.BlockSpec((1, MEM_TOKENS, D_MEM), lambda i, j: (i, 0, 0)),
            pl.BlockSpec((1, MEM_TOKENS, D_MEM), lambda i, j: (i, 0, 0)),
        ],
        out_specs=pl.BlockSpec((1, tq, D_MEM), lambda i, j: (i, j, 0)),
        out_shape=jax.ShapeDtypeStruct((b, t, D_MEM), BF16),
        compiler_params=_params("parallel", "parallel"),
        name="mem_attn",
    )(q, mk, mv)


def _shift_rows(u, shift, head_rows):
    s = pltpu.roll(u, shift, 0)
    row = lax.broadcasted_iota(jnp.int32, (SUBLANE, u.shape[1]), 0)
    top = s[0:SUBLANE]
    for r, v in enumerate(head_rows):
        top = jnp.where(row == r, v, top)
    return jnp.concatenate([top, s[SUBLANE:]], axis=0)


def _conv_gelu_gate(u, gate, prev2, prev1, cw_ref, cb_ref):
    uc = (cb_ref[...] + _shift_rows(u, 2, [prev2, prev1]) * cw_ref[0:1, :]
          + _shift_rows(u, 1, [prev1]) * cw_ref[1:2, :] + u * cw_ref[2:3, :])
    return jax.nn.gelu(uc) * gate


def _ffn_up_act_kernel(x_ref, g_ref, wu32_ref, wg32_ref, b0_ref, cw_ref, cb_ref, act_ref, nb_ref,
                       xn_ref, tail_ref, wu_ref, wg_ref, *, seq, tiles_per_seq, t_valid):
    i = pl.program_id(0)
    j = pl.program_id(1)
    tm = x_ref.shape[0]

    @pl.when(j == 0)
    def _():
        x = x_ref[...]
        ms = jnp.mean(x * x, axis=-1, keepdims=True)
        xn_ref[...] = (x * lax.rsqrt(ms + EPS) * g_ref[...]).astype(BF16)

    wu_ref[...] = wu32_ref[...].astype(BF16)
    wg_ref[...] = wg32_ref[...].astype(BF16)

    if tiles_per_seq >= 1:
        @pl.when(i == 0)
        def _():
            tail_ref[j] = jnp.zeros(tail_ref.shape[1:], F32)

        first = (i % tiles_per_seq) == 0
        prev2 = jnp.where(first, b0_ref[0, 0:1, :], tail_ref[j, SUBLANE - 2:SUBLANE - 1, :])
        prev1 = jnp.where(first, b0_ref[0, 1:2, :], tail_ref[j, SUBLANE - 1:SUBLANE, :])
        rc = min(tm, FFN_ROW_CHUNK)
        for c in range(tm // rc):
            xs = xn_ref[c * rc:(c + 1) * rc, :]
            u = jnp.dot(xs, wu_ref[...], preferred_element_type=F32)
            gate = jnp.dot(xs, wg_ref[...], preferred_element_type=F32)
            act_ref[c * rc:(c + 1) * rc, :] = _conv_gelu_gate(u, gate, prev2, prev1, cw_ref, cb_ref
                                                              ).astype(act_ref.dtype)
            prev2, prev1 = u[rc - 2:rc - 1], u[rc - 1:rc]
        tail_ref[j] = u[rc - SUBLANE:rc]

        @pl.when((i % tiles_per_seq) == tiles_per_seq - 1)
        def _():
            nb_ref[i // tiles_per_seq, j] = u[rc - 2:rc]
    else:
        xn = xn_ref[...]
        u = jnp.dot(xn, wu_ref[...], preferred_element_type=F32)
        gate = jnp.dot(xn, wg_ref[...], preferred_element_type=F32)
        for b in range(tm // seq):
            rs = slice(b * seq, (b + 1) * seq)
            a = _conv_gelu_gate(u[rs], gate[rs], b0_ref[b, 0:1, :], b0_ref[b, 1:2, :], cw_ref, cb_ref)
            act_ref[rs, :] = a.astype(act_ref.dtype)
            nb_ref[(tm // seq) * i + b, j] = u[b * seq + t_valid - 2:b * seq + t_valid]


def ffn_up_act(x, g, w_up, buf0, cw, cb, *, seq, t_valid, tm_pref=1024, tn=512):
    m, d = x.shape
    assert (w_up.rows, w_up.cols) == (d, 2 * D_FF)
    b = m // seq
    tm = _tile(m, tm_pref, SUBLANE)
    assert t_valid >= FFN_CONV - 1 and (seq % tm == 0 or tm % seq == 0)
    tiles_per_seq = seq // tm if tm <= seq else 0
    assert tiles_per_seq == 0 or t_valid == seq
    seqs_per_tile = max(tm // seq, 1)
    n_j = D_FF // tn
    act, nb = pl.pallas_call(
        functools.partial(_ffn_up_act_kernel, seq=seq, tiles_per_seq=tiles_per_seq, t_valid=t_valid),
        grid=(m // tm, n_j),
        in_specs=[
            pl.BlockSpec((tm, d), lambda i, j: (i, 0)),
            pl.BlockSpec((1, d), lambda i, j: (0, 0)),
            _weight_spec(w_up, d, tn, lambda i, j: 0, lambda i, j: j),
            _weight_spec(w_up, d, tn, lambda i, j: 0, lambda i, j: n_j + j),
            pl.BlockSpec((seqs_per_tile, FFN_CONV - 1, tn),
                         lambda i, j: ((i // tiles_per_seq) if tiles_per_seq else i, 0, j)),
            pl.BlockSpec((FFN_CONV, tn), lambda i, j: (0, j)),
            pl.BlockSpec((1, tn), lambda i, j: (0, j)),
        ],
        out_specs=[
            pl.BlockSpec((tm, tn), lambda i, j: (i, j)),
            pl.BlockSpec((b, n_j, FFN_CONV - 1, tn), lambda i, j: (0, 0, 0, 0)),
        ],
        out_shape=[
            jax.ShapeDtypeStruct((m, D_FF), BF16),
            jax.ShapeDtypeStruct((b, n_j, FFN_CONV - 1, tn), F32),
        ],
        scratch_shapes=[pltpu.VMEM((tm, d), BF16), pltpu.VMEM((n_j, SUBLANE, tn), F32),
                        pltpu.VMEM((d, tn), BF16), pltpu.VMEM((d, tn), BF16)],
        compiler_params=_params("arbitrary", "arbitrary"),
        name="ffn_up_act",
    )(x, g.reshape(1, d).astype(F32), w_up.array, w_up.array, buf0, cw.astype(F32),
      cb.reshape(1, D_FF).astype(F32))
    return act, nb.transpose(0, 2, 1, 3).reshape(b, FFN_CONV - 1, D_FF)


def _suffix_matrix():
    j = lax.broadcasted_iota(jnp.int32, (2 * HEAD, 2 * HEAD), 0) % HEAD
    s = lax.broadcasted_iota(jnp.int32, (2 * HEAD, 2 * HEAD), 1)
    return jnp.where((s >= HEAD) | (j >= s), 1.0, 0.0).astype(BF16)


def _sb_weights(s, neg_bias, carry, suffix, mask):
    nz, lk = _sb_log_keep(s, neg_bias, mask)
    return _sb_finish(nz, _sb_suffix_sums(lk, suffix), carry, mask)


def _sb_log_keep(s, neg_bias, mask):
    nz = neg_bias - s
    neg_abs = pltpu.bitcast(pltpu.bitcast(nz, jnp.uint32) | jnp.uint32(0x80000000), F32)
    lk = jnp.minimum(nz, 0.0) - jnp.log2(1.0 + jnp.exp2(neg_abs))
    if mask is not None:
        lk = jnp.where(mask, lk, 0.0)
    return nz, lk


def _sb_suffix_sums(lk, suffix):
    hi = lk.astype(BF16)
    lo = (lk - hi.astype(F32)).astype(BF16)
    return jnp.dot(jnp.concatenate([hi, lo], axis=1), suffix, preferred_element_type=F32)


def _sb_finish(nz, c2, carry, mask):
    w = jnp.exp2((carry + c2[:, :HEAD]) - nz)
    if mask is not None:
        w = jnp.where(mask, w, 0.0)
    return w, carry + c2[:, HEAD:]


def _nt_dot(a, b):
    return lax.dot_general(a, b, (((1,), (1,)), ((), ())), preferred_element_type=F32)


def _sb_prompt_kernel(nbias_ref, q_ref, k_ref, v_ref, o_ref, carry_ref, acc_ref, *, tq, hp):
    hg = pl.program_id(1)
    qi = pl.program_id(2)
    suffix = _suffix_matrix()
    n_sub = tq // HEAD
    t_loc = lax.broadcasted_iota(jnp.int32, (tq, HEAD), 0)
    s_loc = lax.broadcasted_iota(jnp.int32, (tq, HEAD), 1)
    carry_ref[...] = jnp.zeros(carry_ref.shape, F32)
    acc_ref[...] = jnp.zeros(acc_ref.shape, F32)

    def block(start, masked):
        for e in range(hp):
            neg_bias = nbias_ref[hg * hp + e]
            s = _nt_dot(q_ref[0, :, e * HEAD:(e + 1) * HEAD], k_ref[0, e, pl.ds(start, tq), :])
            carry = carry_ref[e]
            ws = [None] * n_sub
            for j in range(n_sub - 1, -1, -1):
                mask = ((s_loc + j * HEAD) < t_loc) if masked else None
                w, carry = _sb_weights(s[:, j * HEAD:(j + 1) * HEAD], neg_bias, carry, suffix, mask)
                ws[j] = w.astype(BF16)
            carry_ref[e] = carry
            acc_ref[e] += jnp.dot(jnp.concatenate(ws, axis=1), v_ref[0, e, pl.ds(start, tq), :],
                                  preferred_element_type=F32)

    block(pl.multiple_of(qi * tq, tq), True)

    @pl.loop(0, qi)
    def _(i):
        block(pl.multiple_of((qi - 1 - i) * tq, tq), False)

    for e in range(hp):
        o_ref[0, :, e * HEAD:(e + 1) * HEAD] = acc_ref[e].astype(o_ref.dtype)


def sb_prompt(q, k, v, bias):
    b, _, t, _ = k.shape
    tq = _tile(t, 512, HEAD)
    hp = SB_HEADS_PER_STEP
    return pl.pallas_call(
        functools.partial(_sb_prompt_kernel, tq=tq, hp=hp),
        grid_spec=pltpu.PrefetchScalarGridSpec(
            num_scalar_prefetch=1,
            grid=(b, SB_HEADS // hp, t // tq),
            in_specs=[
                pl.BlockSpec((1, tq, hp * HEAD), lambda i, h, j, nb: (i, j, h)),
                pl.BlockSpec((1, hp, t, HEAD), lambda i, h, j, nb: (i, h, 0, 0)),
                pl.BlockSpec((1, hp, t, HEAD), lambda i, h, j, nb: (i, h, 0, 0)),
            ],
            out_specs=pl.BlockSpec((1, tq, hp * HEAD), lambda i, h, j, nb: (i, j, h)),
            scratch_shapes=[pltpu.VMEM((hp, tq, HEAD), F32), pltpu.VMEM((hp, tq, HEAD), F32)],
        ),
        out_shape=jax.ShapeDtypeStruct((b, t, D_RNN), BF16),
        compiler_params=_params("parallel", "parallel", "arbitrary"),
        name="sb_prompt",
    )(-LOG2E * bias.astype(F32), q, k, v)


def _sb_sample_kernel(pt_ref, q_ref, kn_ref, vn_ref, bias_ref, *rest, pages_per_step, t_valid):
    kp_refs = rest[:pages_per_step]
    vp_refs = rest[pages_per_step:2 * pages_per_step]
    o_ref, carry_ref, acc_ref = rest[2 * pages_per_step:]
    g = pl.program_id(1)
    rows = SB_HEADS * Q_ROWS
    suffix = _suffix_matrix()
    neg_bias = bias_ref[...]
    hcols = [slice(h * HEAD, (h + 1) * HEAD) for h in range(SB_HEADS)]
    q_heads = [q_ref[0, :, hcols[h]] for h in range(SB_HEADS)]

    def update(blocks, mask):
        n = len(blocks)
        nzs, lks = [], []
        for k_heads, _ in blocks:
            s = jnp.concatenate([_nt_dot(q_heads[h], k_heads[h])[0:Q_ROWS] for h in range(SB_HEADS)], axis=0)
            nz, lk = _sb_log_keep(s, neg_bias, mask)
            nzs.append(nz)
            lks.append(lk)
        c2 = _sb_suffix_sums(jnp.concatenate(lks, axis=0), suffix)
        carry = carry_ref[...]
        ws = []
        for r in range(n):
            w, carry = _sb_finish(nzs[r], c2[r * rows:(r + 1) * rows], carry, mask)
            ws.append(w)
        carry_ref[...] = carry
        zpad = jnp.zeros((SAMPLE_T_PAD - Q_ROWS, HEAD), F32)
        for h in range(SB_HEADS):
            rs = slice(h * Q_ROWS, (h + 1) * Q_ROWS)
            pv = None
            for r in range(n):
                wh = jnp.concatenate([ws[r][rs], zpad], axis=0).astype(BF16)
                d = jnp.dot(wh, blocks[r][1][h], preferred_element_type=F32)
                pv = d if pv is None else pv + d
            acc_ref[rs, :] += pv[0:Q_ROWS]

    @pl.when(g == 0)
    def _():
        carry_ref[...] = jnp.zeros((rows, HEAD), F32)
        acc_ref[...] = jnp.zeros((rows, HEAD), F32)
        kz = jnp.zeros((HEAD - SAMPLE_T_PAD, HEAD), BF16)
        k_heads = [jnp.concatenate([kn_ref[0, :, hcols[h]], kz], axis=0) for h in range(SB_HEADS)]
        v_heads = [jnp.concatenate([vn_ref[0, :, hcols[h]], kz], axis=0) for h in range(SB_HEADS)]
        i_q = lax.broadcasted_iota(jnp.int32, (rows, HEAD), 0) % Q_ROWS
        j_k = lax.broadcasted_iota(jnp.int32, (rows, HEAD), 1)
        update([(k_heads, v_heads)], (j_k < i_q) & (j_k < t_valid))

    update([([kp_refs[r][0, h].astype(BF16) for h in range(SB_HEADS)],
             [vp_refs[r][0, h].astype(BF16) for h in range(SB_HEADS)]) for r in range(pages_per_step)], None)

    @pl.when(g == pl.num_programs(1) - 1)
    def _():
        z8 = jnp.zeros((SAMPLE_T_PAD - Q_ROWS, HEAD), F32)
        for h in range(SB_HEADS):
            blk = jnp.concatenate([acc_ref[h * Q_ROWS:(h + 1) * Q_ROWS, :], z8], axis=0)
            o_ref[0, :, h * HEAD:(h + 1) * HEAD] = blk.astype(o_ref.dtype)


def sb_sample(q, k_new, v_new, cache_k, cache_v, page_table, bias, *, t_valid, pages_per_step=8):
    b, n_pages = page_table.shape
    page = cache_k.shape[2]
    assert page == HEAD and n_pages % pages_per_step == 0 and t_valid <= Q_ROWS
    assert cache_k.shape[1:] == (SB_HEADS, page, HEAD)
    rows = SB_HEADS * Q_ROWS
    bias_rows = jnp.broadcast_to(jnp.repeat(-LOG2E * bias.astype(F32), Q_ROWS)[:, None], (rows, HEAD))

    def page_spec(r):
        return pl.BlockSpec(
            (1, SB_HEADS, page, HEAD),
            lambda i, g, pt: (pt[i, n_pages - 1 - (g * pages_per_step + r)], 0, 0, 0))

    return pl.pallas_call(
        functools.partial(_sb_sample_kernel, pages_per_step=pages_per_step, t_valid=t_valid),
        grid_spec=pltpu.PrefetchScalarGridSpec(
            num_scalar_prefetch=1,
            grid=(b, n_pages // pages_per_step),
            in_specs=[
                pl.BlockSpec((1, SAMPLE_T_PAD, D_RNN), lambda i, g, pt: (i, 0, 0)),
                pl.BlockSpec((1, SAMPLE_T_PAD, D_RNN), lambda i, g, pt: (i, 0, 0)),
                pl.BlockSpec((1, SAMPLE_T_PAD, D_RNN), lambda i, g, pt: (i, 0, 0)),
                pl.BlockSpec((rows, HEAD), lambda i, g, pt: (0, 0)),
            ] + [page_spec(r) for r in range(pages_per_step)] + [page_spec(r) for r in range(pages_per_step)],
            out_specs=pl.BlockSpec((1, SAMPLE_T_PAD, D_RNN), lambda i, g, pt: (i, 0, 0)),
            scratch_shapes=[pltpu.VMEM((rows, HEAD), F32), pltpu.VMEM((rows, HEAD), F32)],
        ),
        out_shape=jax.ShapeDtypeStruct((b, SAMPLE_T_PAD, D_RNN), BF16),
        compiler_params=_params("parallel", "arbitrary"),
        name="sb_sample",
    )(page_table, q, k_new, v_new, bias_rows, *([cache_k] * pages_per_step), *([cache_v] * pages_per_step))


def _trunk(x3, t_valid, rg_h0, rg_conv0, ffn_conv0, mem_k, mem_v, sb_attend, head_major_kv, P, W):
    b, t, d = x3.shape
    m = b * t
    x = x3.reshape(m, d)
    rg_h_out, rg_conv_out, ffn_conv_out = [], [], []
    k_f32 = v_f32 = k_bf = v_bf = None
    for l in range(DEPTH):
        mem_gain = jnp.tile(P['mem_q_norm'][l] * ATTN_SCALE, MEM_HEADS)
        if l < N_A_LAYERS:
            p_rg = norm_matmul(x, P['g_mix'][l], W['w_in_a_rg'](l))
            qm = norm_matmul(x, P['g_mix'][l], W['w_in_a_qm'](l), head_gain=mem_gain)
            y_tok, h_last, cbuf = rglru(
                p_rg.reshape(b, t, 2 * D_RNN), rg_h0[l], rg_conv0[l], P['rg_conv_w'][l], P['rg_conv_b'][l],
                W['rg_gate_x_w'][l], P['rg_gate_x_b'][l], W['rg_gate_a_w'][l], P['rg_gate_a_b'][l],
                P['rg_lambda'][l], t_valid=t_valid)
            rg_h_out.append(h_last)
            rg_conv_out.append(cbuf)
            q3, q_col = qm.reshape(b, t, D_MEM), 0
        else:
            j = l - N_A_LAYERS
            gain = jnp.concatenate([jnp.tile(P['sb_q_norm'][j] * (ATTN_SCALE * LOG2E), SB_HEADS), mem_gain])
            pq = norm_matmul(x, P['g_mix'][l], W['w_in_b'](j), head_gain=gain)
            q3, q_col = pq.reshape(b, t, D_RNN + D_MEM), D_RNN // D_MEM
            y_tok = sb_attend(q3, k_bf, v_bf, P['sb_beta_bias'][j])
        y_mem = mem_attn(q3, q_col, mem_k[l], mem_v[l])
        x = matmul_residual([(y_tok.reshape(m, D_RNN), W['w_out_tok'](l)),
                             (y_mem.reshape(m, D_MEM), W['w_out_mem'](l))], x)
        act, fbuf = ffn_up_act(x, P['g_ffn'][l], W['w_ffn_up'](l), ffn_conv0[l], P['ffn_conv_w'][l],
                               P['ffn_conv_b'][l], seq=t, t_valid=t_valid)
        x = matmul_residual([(act, W['w_ffn_down'](l))], x, n_k=3)
        ffn_conv_out.append(fbuf)
        if l == N_A_LAYERS - 1:
            k_gain = jnp.tile(P['kv_k_norm'], SB_HEADS)
            hm = t if head_major_kv else None
            k_f32, k_bf = norm_matmul(x, P['kv_norm'], W['w_k'], head_gain=k_gain, out_dtypes=(F32, BF16),
                                      head_major_seq=hm)
            v_f32, v_bf = norm_matmul(x, P['kv_norm'], W['w_v'], out_dtypes=(F32, BF16), head_major_seq=hm)
            if not head_major_kv:
                k_bf = k_bf.reshape(b, t, D_RNN)
                v_bf = v_bf.reshape(b, t, D_RNN)
    if head_major_kv:
        k_out, v_out = k_f32.transpose(0, 2, 1, 3), v_f32.transpose(0, 2, 1, 3)
    else:
        k_out, v_out = k_f32.reshape(b, t, SB_HEADS, HEAD), v_f32.reshape(b, t, SB_HEADS, HEAD)
    return (x.reshape(b, t, d), jnp.stack(rg_h_out), jnp.stack(rg_conv_out), jnp.stack(ffn_conv_out),
            k_out, v_out)


def kernel(x_prompt, x_sample, mem_prompt, state_rglru_h, state_rglru_conv, state_ffn_conv, cache_mem_k, cache_mem_v, cache_sb_k, cache_sb_v, page_table, g_mix, g_ffn, w_in_a, rg_conv_w, rg_conv_b, rg_gate_x_w, rg_gate_x_b, rg_gate_a_w, rg_gate_a_b, rg_lambda, w_in_b, sb_q_norm, sb_beta_bias, kv_norm, w_kv, kv_k_norm, mem_norm, w_mem_kv, mem_q_norm, mem_k_norm, w_out, w_ffn_up, ffn_conv_w, ffn_conv_b, w_ffn_down):
    P = {'g_mix': g_mix, 'g_ffn': g_ffn, 'rg_conv_w': rg_conv_w, 'rg_conv_b': rg_conv_b,
         'rg_gate_x_b': rg_gate_x_b, 'rg_gate_a_b': rg_gate_a_b, 'rg_lambda': rg_lambda,
         'sb_q_norm': sb_q_norm, 'sb_beta_bias': sb_beta_bias, 'kv_norm': kv_norm,
         'kv_k_norm': kv_k_norm, 'mem_q_norm': mem_q_norm,
         'ffn_conv_w': ffn_conv_w, 'ffn_conv_b': ffn_conv_b}
    dm = D_MODEL
    W = {'w_in_a_rg': lambda l: WeightView(w_in_a, l, 0, dm, 0, 2 * D_RNN),
         'w_in_a_qm': lambda l: WeightView(w_in_a, l, 0, dm, 2 * D_RNN, D_MEM),
         'rg_gate_x_w': rg_gate_x_w, 'rg_gate_a_w': rg_gate_a_w,
         'w_in_b': lambda j: WeightView(w_in_b, j, 0, dm, 0, D_RNN + D_MEM),
         'w_k': WeightView(w_kv, None, 0, dm, 0, D_RNN), 'w_v': WeightView(w_kv, None, 0, dm, D_RNN, D_RNN),
         'w_out_tok': lambda l: WeightView(w_out, l, 0, D_RNN, 0, dm),
         'w_out_mem': lambda l: WeightView(w_out, l, D_RNN, D_MEM, 0, dm),
         'w_ffn_up': lambda l: WeightView(w_ffn_up, l, 0, dm, 0, 2 * D_FF),
         'w_ffn_down': lambda l: WeightView(w_ffn_down, l, 0, D_FF, 0, dm)}

    bp, seq, d = x_prompt.shape
    mem2 = mem_prompt.reshape(bp * MEM_TOKENS, d)
    mk_list, mv_list = [], []
    for l in range(DEPTH):
        mk_list.append(norm_matmul(mem2, mem_norm[l], WeightView(w_mem_kv, l, 0, dm, 0, D_MEM),
                                   head_gain=jnp.tile(mem_k_norm[l], MEM_HEADS), out_dtypes=(F32,)))
        mv_list.append(norm_matmul(mem2, mem_norm[l], WeightView(w_mem_kv, l, 0, dm, D_MEM, D_MEM),
                                   out_dtypes=(F32,)))
    prompt_mem_k = jnp.stack(mk_list).reshape(DEPTH, bp, MEM_TOKENS, D_MEM)
    prompt_mem_v = jnp.stack(mv_list).reshape(DEPTH, bp, MEM_TOKENS, D_MEM)
    zeros_h = jnp.zeros((N_A_LAYERS, bp, D_RNN), F32)
    zeros_rc = jnp.zeros((N_A_LAYERS, bp, RG_CONV - 1, D_RNN), F32)
    zeros_fc = jnp.zeros((DEPTH, bp, FFN_CONV - 1, D_FF), F32)
    (y_prompt, prompt_rglru_h, prompt_rglru_conv, prompt_ffn_conv, prompt_sb_k, prompt_sb_v) = _trunk(
        x_prompt, seq, zeros_h, zeros_rc, zeros_fc, prompt_mem_k, prompt_mem_v, sb_prompt, True, P, W)

    db, dec_seq, _ = x_sample.shape
    xs = jnp.pad(x_sample, ((0, 0), (0, SAMPLE_T_PAD - dec_seq), (0, 0)))

    cache_k_hm = cache_sb_k.transpose(0, 2, 1, 3)
    cache_v_hm = cache_sb_v.transpose(0, 2, 1, 3)

    def sb_paged(q3, k_bf, v_bf, bias):
        return sb_sample(q3, k_bf, v_bf, cache_k_hm, cache_v_hm, page_table, bias, t_valid=dec_seq)

    (y_s, sample_rglru_h, sample_rglru_conv, sample_ffn_conv, s_k, s_v) = _trunk(
        xs, dec_seq, state_rglru_h, state_rglru_conv, state_ffn_conv,
        cache_mem_k.reshape(DEPTH, db, MEM_TOKENS, D_MEM), cache_mem_v.reshape(DEPTH, db, MEM_TOKENS, D_MEM),
        sb_paged, False, P, W)

    mem_shape = (DEPTH, bp, MEM_TOKENS, MEM_HEADS, HEAD)
    return (y_prompt, y_s[:, :dec_seq],
            prompt_rglru_h, prompt_rglru_conv, prompt_ffn_conv,
            prompt_sb_k, prompt_sb_v, prompt_mem_k.reshape(mem_shape), prompt_mem_v.reshape(mem_shape),
            sample_rglru_h, sample_rglru_conv, sample_ffn_conv,
            s_k[:, :dec_seq], s_v[:, :dec_seq])
```

```python
import functools
import math
from typing import NamedTuple

import jax
import jax.numpy as jnp
from jax import lax
from jax.experimental import pallas as pl
from jax.experimental.pallas import tpu as pltpu

F32 = jnp.float32
BF16 = jnp.bfloat16

LANE = 128
SUBLANE = 8
VMEM_LIMIT_BYTES = 56 * 1024 * 1024

D_MODEL = 2048
DEPTH = 4
N_A_LAYERS = 2
D_RNN = 1536
HEAD = 128
RG_HEADS = D_RNN // HEAD
RG_CONV = 4
RG_C = 8.0
SB_HEADS = D_RNN // HEAD
MEM_TOKENS = 256
MEM_HEADS = 4
D_MEM = MEM_HEADS * HEAD
D_FF = 3 * D_MODEL
FFN_CONV = 3
EPS = 1e-6
ATTN_SCALE = HEAD ** -0.5
LOG2E = math.log2(math.e)
SAMPLE_T_PAD = 16
Q_ROWS = 8
FFN_ROW_CHUNK = 256
SB_HEADS_PER_STEP = 3


def _params(*sem):
    return pltpu.CompilerParams(dimension_semantics=sem, vmem_limit_bytes=VMEM_LIMIT_BYTES)


def _tile(n, pref, mult):
    if n <= pref:
        return n
    t = (pref // mult) * mult
    while t > mult and n % t:
        t -= mult
    assert n % t == 0, (n, pref, mult)
    return t


class WeightView(NamedTuple):
    array: jax.Array
    layer: int | None
    row0: int
    rows: int
    col0: int
    cols: int


def _weight_spec(wv, tk, tn, row_block, col_block):
    assert wv.row0 % tk == 0 and wv.col0 % tn == 0 and wv.rows % tk == 0 and wv.cols % tn == 0
    r0, c0 = wv.row0 // tk, wv.col0 // tn
    if wv.layer is None:
        return pl.BlockSpec((tk, tn), lambda *g: (r0 + row_block(*g), c0 + col_block(*g)))
    layer = wv.layer
    return pl.BlockSpec((None, tk, tn), lambda *g: (layer, r0 + row_block(*g), c0 + col_block(*g)))


def _softplus(z):
    return jnp.maximum(z, 0.0) + jnp.log1p(jnp.exp(-jnp.abs(z)))


def _head_rmsnorm(blk, gain):
    ms = jnp.mean(blk * blk, axis=-1, keepdims=True)
    return blk * lax.rsqrt(ms + EPS) * gain


def _norm_matmul_kernel(*refs, head_norm, head_major, n_out, emit_w):
    x_ref, g_ref, w_ref = refs[:3]
    pos = 3
    hg_ref = None
    if head_norm:
        hg_ref = refs[pos]
        pos += 1
    o_refs = refs[pos:pos + n_out]
    pos += n_out
    wb_ref = None
    if emit_w:
        wb_ref = refs[pos]
        pos += 1
    xn_ref = refs[pos]

    @pl.when(pl.program_id(1) == 0)
    def _():
        x = x_ref[...]
        ms = jnp.mean(x * x, axis=-1, keepdims=True)
        xn_ref[...] = (x * lax.rsqrt(ms + EPS) * g_ref[...]).astype(BF16)

    wb = w_ref[...].astype(BF16)
    if emit_w:
        wb_ref[...] = wb
    acc = jnp.dot(xn_ref[...], wb, preferred_element_type=F32)
    tn = acc.shape[1]
    if head_norm or head_major:
        for h in range(tn // HEAD):
            cs = slice(h * HEAD, (h + 1) * HEAD)
            y = _head_rmsnorm(acc[:, cs], hg_ref[:, cs]) if head_norm else acc[:, cs]
            for o_ref in o_refs:
                if head_major:
                    o_ref[0, h] = y.astype(o_ref.dtype)
                else:
                    o_ref[:, cs] = y.astype(o_ref.dtype)
    else:
        for o_ref in o_refs:
            o_ref[...] = acc.astype(o_ref.dtype)


def _bf16_view(arr):
    return WeightView(arr, None, 0, arr.shape[0], 0, arr.shape[1])


def norm_matmul(x, g, w, *, head_gain=None, out_dtypes=(BF16,), head_major_seq=None, emit_w=False,
                tm_pref=1024, tn_pref=1024):
    m, d = x.shape
    n = w.cols
    assert w.rows == d
    tm = _tile(m, tm_pref, SUBLANE)
    tn = _tile(n, tn_pref, 2 * LANE)
    assert not emit_w or (m == tm and head_major_seq is None)
    if head_major_seq is None:
        out_specs = [pl.BlockSpec((tm, tn), lambda i, j: (i, j)) for _ in out_dtypes]
        out_shape = [jax.ShapeDtypeStruct((m, n), dt) for dt in out_dtypes]
    else:
        seq = head_major_seq
        assert seq % tm == 0 and m % seq == 0
        per_seq = seq // tm
        out_specs = [pl.BlockSpec((1, tn // HEAD, tm, HEAD), lambda i, j: (i // per_seq, j, i % per_seq, 0))
                     for _ in out_dtypes]
        out_shape = [jax.ShapeDtypeStruct((m // seq, n // HEAD, seq, HEAD), dt) for dt in out_dtypes]
    in_specs = [
        pl.BlockSpec((tm, d), lambda i, j: (i, 0)),
        pl.BlockSpec((1, d), lambda i, j: (0, 0)),
        _weight_spec(w, d, tn, lambda i, j: 0, lambda i, j: j),
    ]
    args = [x, g.reshape(1, d).astype(F32), w.array]
    if head_gain is not None:
        in_specs.append(pl.BlockSpec((1, tn), lambda i, j: (0, j)))
        args.append(head_gain.reshape(1, n).astype(F32))
    if emit_w:
        out_specs.append(pl.BlockSpec((d, tn), lambda i, j: (0, j)))
        out_shape.append(jax.ShapeDtypeStruct((d, n), BF16))
    outs = pl.pallas_call(
        functools.partial(_norm_matmul_kernel, head_norm=head_gain is not None,
                          head_major=head_major_seq is not None, n_out=len(out_dtypes), emit_w=emit_w),
        grid=(m // tm, n // tn),
        in_specs=in_specs,
        out_specs=out_specs,
        out_shape=out_shape,
        scratch_shapes=[pltpu.VMEM((tm, d), BF16)],
        compiler_params=_params("parallel", "arbitrary"),
        name="norm_matmul",
    )(*args)
    return outs[0] if len(outs) == 1 else tuple(outs)


def _matmul_residual_kernel(*refs, n_pairs, emit_w):
    x_ref = refs[2 * n_pairs]
    o_ref = refs[2 * n_pairs + 1]
    wb_refs = refs[2 * n_pairs + 2:]

    @pl.when(pl.program_id(2) == 0)
    def _():
        o_ref[...] = x_ref[...]

    acc = None
    for p in range(n_pairs):
        wb = refs[2 * p + 1][...].astype(BF16)
        if emit_w:
            wb_refs[p][...] = wb
        d = jnp.dot(refs[2 * p][...], wb, preferred_element_type=F32)
        acc = d if acc is None else acc + d
    o_ref[...] += acc


def matmul_residual(pairs, x, *, n_k=1, emit_w=False, tm_pref=1024, tn_pref=1024):
    m, n = x.shape
    tm = _tile(m, tm_pref, SUBLANE)
    tn = _tile(n, tn_pref, 2 * LANE)
    assert not emit_w or m == tm
    out_specs = [pl.BlockSpec((tm, tn), lambda i, j, k: (i, j))]
    out_shape = [jax.ShapeDtypeStruct((m, n), F32)]
    in_specs, args = [], []
    for a, w in pairs:
        kp = a.shape[1]
        assert kp % n_k == 0 and (w.rows, w.cols) == (kp, n)
        tk = kp // n_k
        in_specs.append(pl.BlockSpec((tm, tk), lambda i, j, k: (i, k)))
        in_specs.append(_weight_spec(w, tk, tn, lambda i, j, k: k, lambda i, j, k: j))
        args += [a, w.array]
        if emit_w:
            out_specs.append(pl.BlockSpec((tk, tn), lambda i, j, k: (k, j)))
            out_shape.append(jax.ShapeDtypeStruct((kp, n), BF16))
    in_specs.append(pl.BlockSpec((tm, tn), lambda i, j, k: (i, j)))
    args.append(x)
    outs = pl.pallas_call(
        functools.partial(_matmul_residual_kernel, n_pairs=len(pairs), emit_w=emit_w),
        grid=(m // tm, n // tn, n_k),
        in_specs=in_specs,
        out_specs=out_specs,
        out_shape=out_shape,
        compiler_params=_params("parallel", "parallel", "arbitrary"),
        name="matmul_residual",
    )(*args)
    return tuple(outs) if emit_w else outs[0]


def _rglru_kernel(p_ref, h0_ref, cb0_ref, cw_ref, cbias_ref, wgx_ref, bgx_ref, wga_ref, bga_ref, lam_ref,
                  y_ref, hl_ref, cbo_ref, ext_ref, hc_ref, *, tr, tile_last, r_last):
    t = pl.program_id(1)
    halo = SUBLANE

    @pl.when(t == 0)
    def _():
        ext_ref[0:halo, :] = jnp.zeros((halo, D_RNN), F32)
        ext_ref[halo - (RG_CONV - 1):halo, :] = cb0_ref[0]
        hc_ref[...] = h0_ref[0]

    ext_ref[halo:halo + tr, :] = p_ref[0, :, 0:D_RNN].astype(F32)
    sub_row = lax.broadcasted_iota(jnp.int32, (tr // SUBLANE, SUBLANE, HEAD), 1)
    for h in range(RG_HEADS):
        cs = slice(h * HEAD, (h + 1) * HEAD)
        xc = cbias_ref[:, cs] + ext_ref[halo - 3:halo - 3 + tr, cs] * cw_ref[0:1, cs]
        for j in range(1, RG_CONV):
            xc = xc + ext_ref[halo - 3 + j:halo - 3 + j + tr, cs] * cw_ref[j:j + 1, cs]
        xcb = xc.astype(BF16)
        gx = jax.nn.sigmoid(jnp.dot(xcb, wgx_ref[h].astype(BF16), preferred_element_type=F32) + bgx_ref[:, cs])
        ga = jax.nn.sigmoid(jnp.dot(xcb, wga_ref[h].astype(BF16), preferred_element_type=F32) + bga_ref[:, cs])
        log_a = (-RG_C) * ga * _softplus(-lam_ref[:, cs])
        a = jnp.exp(log_a)
        u = jnp.sqrt(-jnp.tanh(log_a) * (1.0 + a * a)) * (gx * xc)
        groups = tr // SUBLANE
        a3 = a.reshape(groups, SUBLANE, HEAD)
        u3 = u.reshape(groups, SUBLANE, HEAD)
        d = 1
        while d < SUBLANE:
            live = sub_row >= d
            u3 = jnp.where(live, a3 * pltpu.roll(u3, d, 1) + u3, u3)
            a3 = jnp.where(live, a3 * pltpu.roll(a3, d, 1), a3)
            d *= 2
        h_prev = jnp.broadcast_to(hc_ref[:, cs], (SUBLANE, HEAD))
        h_groups = []
        for k in range(groups):
            h_k = a3[k] * h_prev + u3[k]
            h_groups.append(h_k)
            h_prev = jnp.broadcast_to(h_k[SUBLANE - 1:SUBLANE, :], (SUBLANE, HEAD))
        u = jnp.concatenate(h_groups, axis=0)
        gate = jax.nn.gelu(p_ref[0, :, D_RNN + h * HEAD:D_RNN + (h + 1) * HEAD].astype(F32))
        y_ref[0, :, cs] = (u * gate).astype(y_ref.dtype)
        hc_ref[:, cs] = u[tr - 1:tr, :]

        @pl.when(t == tile_last)
        def _():
            hl_ref[0, :, cs] = u[r_last:r_last + 1, :]

    @pl.when(t == tile_last)
    def _():
        cbo_ref[0] = ext_ref[halo + r_last - 2:halo + r_last + 1, :]

    if tr >= halo:
        ext_ref[0:halo, :] = ext_ref[tr:tr + halo, :]


def rglru(p, h0, cb0, cw, cbias, wgx, bgx, wga, bga, lam, *, t_valid):
    b, t, _ = p.shape
    tr = _tile(t, 256, 16)
    row = lambda v: v.reshape(1, D_RNN).astype(F32)
    full = lambda shape: pl.BlockSpec(shape, lambda i, j: (0,) * len(shape))
    y, hl, cbo = pl.pallas_call(
        functools.partial(_rglru_kernel, tr=tr, tile_last=(t_valid - 1) // tr, r_last=(t_valid - 1) % tr),
        grid=(b, t // tr),
        in_specs=[
            pl.BlockSpec((1, tr, 2 * D_RNN), lambda i, j: (i, j, 0)),
            pl.BlockSpec((1, 1, D_RNN), lambda i, j: (i, 0, 0)),
            pl.BlockSpec((1, RG_CONV - 1, D_RNN), lambda i, j: (i, 0, 0)),
            full((RG_CONV, D_RNN)), full((1, D_RNN)),
            full((RG_HEADS, HEAD, HEAD)), full((1, D_RNN)),
            full((RG_HEADS, HEAD, HEAD)), full((1, D_RNN)),
            full((1, D_RNN)),
        ],
        out_specs=[
            pl.BlockSpec((1, tr, D_RNN), lambda i, j: (i, j, 0)),
            pl.BlockSpec((1, 1, D_RNN), lambda i, j: (i, 0, 0)),
            pl.BlockSpec((1, RG_CONV - 1, D_RNN), lambda i, j: (i, 0, 0)),
        ],
        out_shape=[
            jax.ShapeDtypeStruct((b, t, D_RNN), BF16),
            jax.ShapeDtypeStruct((b, 1, D_RNN), F32),
            jax.ShapeDtypeStruct((b, RG_CONV - 1, D_RNN), F32),
        ],
        scratch_shapes=[pltpu.VMEM((tr + SUBLANE, D_RNN), F32), pltpu.VMEM((1, D_RNN), F32)],
        compiler_params=_params("parallel", "arbitrary"),
        name="rglru",
    )(p, h0.reshape(b, 1, D_RNN), cb0, cw.astype(F32), row(cbias), wgx, row(bgx), wga, row(bga), row(lam))
    return y, hl.reshape(b, D_RNN), cbo


def _mem_attn_kernel(q_ref, k_ref, v_ref, o_ref):
    for h in range(MEM_HEADS):
        cs = slice(h * HEAD, (h + 1) * HEAD)
        q = q_ref[0, :, cs]
        k = k_ref[0, :, cs].astype(BF16)
        v = v_ref[0, :, cs].astype(BF16)
        s = lax.dot_general(q, k, (((1,), (1,)), ((), ())), preferred_element_type=F32)
        e = jnp.exp(s - jnp.max(s, axis=-1, keepdims=True))
        p = e / jnp.sum(e, axis=-1, keepdims=True)
        o_ref[0, :, cs] = jnp.dot(p.astype(BF16), v, preferred_element_type=F32).astype(o_ref.dtype)


def mem_attn(q, q_col_block, mk, mv):
    b, t, _ = q.shape
    tq = _tile(t, 512, 16)
    return pl.pallas_call(
        _mem_attn_kernel,
        grid=(b, t // tq),
        in_specs=[
            pl.BlockSpec((1, tq, D_MEM), lambda i, j: (i, j, q_col_block)),
            pl.BlockSpec((1, MEM_TOKENS, D_MEM), lambda i, j: (i, 0, 0)),
            pl.BlockSpec((1, MEM_TOKENS, D_MEM), lambda i, j: (i, 0, 0)),
        ],
        out_specs=pl.BlockSpec((1, tq, D_MEM), lambda i, j: (i, j, 0)),
        out_shape=jax.ShapeDtypeStruct((b, t, D_MEM), BF16),
        compiler_params=_params("parallel", "parallel"),
        name="mem_attn",
    )(q, mk, mv)


def _shift_rows(u, shift, head_rows):
    s = pltpu.roll(u, shift, 0)
    row = lax.broadcasted_iota(jnp.int32, (SUBLANE, u.shape[1]), 0)
    top = s[0:SUBLANE]
    for r, v in enumerate(head_rows):
        top = jnp.where(row == r, v, top)
    return jnp.concatenate([top, s[SUBLANE:]], axis=0)


def _conv_gelu_gate(u, gate, prev2, prev1, cw_ref, cb_ref):
    uc = (cb_ref[...] + _shift_rows(u, 2, [prev2, prev1]) * cw_ref[0:1, :]
          + _shift_rows(u, 1, [prev1]) * cw_ref[1:2, :] + u * cw_ref[2:3, :])
    return jax.nn.gelu(uc) * gate


def _ffn_up_act_kernel(x_ref, g_ref, wu_in_ref, wg_in_ref, b0_ref, cw_ref, cb_ref, act_ref, nb_ref, *rest,
                       seq, tiles_per_seq, t_valid, cast_w):
    i = pl.program_id(0)
    j = pl.program_id(1)
    tm = x_ref.shape[0]
    if cast_w:
        wu_ref, wg_ref, xn_ref, tail_ref = rest
        wu_ref[...] = wu_in_ref[...].astype(BF16)
        wg_ref[...] = wg_in_ref[...].astype(BF16)
    else:
        xn_ref, tail_ref = rest
        wu_ref, wg_ref = wu_in_ref, wg_in_ref

    @pl.when(j == 0)
    def _():
        x = x_ref[...]
        ms = jnp.mean(x * x, axis=-1, keepdims=True)
        xn_ref[...] = (x * lax.rsqrt(ms + EPS) * g_ref[...]).astype(BF16)

    if tiles_per_seq >= 1:
        @pl.when(i == 0)
        def _():
            tail_ref[j] = jnp.zeros(tail_ref.shape[1:], F32)

        first = (i % tiles_per_seq) == 0
        prev2 = jnp.where(first, b0_ref[0, 0:1, :], tail_ref[j, SUBLANE - 2:SUBLANE - 1, :])
        prev1 = jnp.where(first, b0_ref[0, 1:2, :], tail_ref[j, SUBLANE - 1:SUBLANE, :])
        rc = min(tm, FFN_ROW_CHUNK)
        for c in range(tm // rc):
            xs = xn_ref[c * rc:(c + 1) * rc, :]
            u = jnp.dot(xs, wu_ref[...], preferred_element_type=F32)
            gate = jnp.dot(xs, wg_ref[...], preferred_element_type=F32)
            act_ref[c * rc:(c + 1) * rc, :] = _conv_gelu_gate(u, gate, prev2, prev1, cw_ref, cb_ref
                                                              ).astype(act_ref.dtype)
            prev2, prev1 = u[rc - 2:rc - 1], u[rc - 1:rc]
        tail_ref[j] = u[rc - SUBLANE:rc]

        @pl.when((i % tiles_per_seq) == tiles_per_seq - 1)
        def _():
            nb_ref[i // tiles_per_seq, j] = u[rc - 2:rc]
    else:
        xn = xn_ref[...]
        u = jnp.dot(xn, wu_ref[...], preferred_element_type=F32)
        gate = jnp.dot(xn, wg_ref[...], preferred_element_type=F32)
        for b in range(tm // seq):
            rs = slice(b * seq, (b + 1) * seq)
            a = _conv_gelu_gate(u[rs], gate[rs], b0_ref[b, 0:1, :], b0_ref[b, 1:2, :], cw_ref, cb_ref)
            act_ref[rs, :] = a.astype(act_ref.dtype)
            nb_ref[(tm // seq) * i + b, j] = u[b * seq + t_valid - 2:b * seq + t_valid]


def ffn_up_act(x, g, w_u, w_g, buf0, cw, cb, *, seq, t_valid, emit_w=False, tm_pref=1024, tn=512):
    m, d = x.shape
    assert (w_u.rows, w_u.cols) == (d, D_FF) and (w_g.rows, w_g.cols) == (d, D_FF)
    b = m // seq
    tm = _tile(m, tm_pref, SUBLANE)
    assert t_valid >= FFN_CONV - 1 and (seq % tm == 0 or tm % seq == 0)
    tiles_per_seq = seq // tm if tm <= seq else 0
    assert tiles_per_seq == 0 or t_valid == seq
    seqs_per_tile = max(tm // seq, 1)
    n_j = D_FF // tn
    cast_w = w_u.array.dtype != BF16
    assert not emit_w or (cast_w and m == tm)
    out_specs = [
        pl.BlockSpec((tm, tn), lambda i, j: (i, j)),
        pl.BlockSpec((b, n_j, FFN_CONV - 1, tn), lambda i, j: (0, 0, 0, 0)),
    ]
    out_shape = [
        jax.ShapeDtypeStruct((m, D_FF), BF16),
        jax.ShapeDtypeStruct((b, n_j, FFN_CONV - 1, tn), F32),
    ]
    scratch = [pltpu.VMEM((tm, d), BF16), pltpu.VMEM((n_j, SUBLANE, tn), F32)]
    if emit_w:
        out_specs += [pl.BlockSpec((d, tn), lambda i, j: (0, j)), pl.BlockSpec((d, tn), lambda i, j: (0, j))]
        out_shape += [jax.ShapeDtypeStruct((d, D_FF), BF16), jax.ShapeDtypeStruct((d, D_FF), BF16)]
    elif cast_w:
        scratch = [pltpu.VMEM((d, tn), BF16), pltpu.VMEM((d, tn), BF16)] + scratch
    outs = pl.pallas_call(
        functools.partial(_ffn_up_act_kernel, seq=seq, tiles_per_seq=tiles_per_seq, t_valid=t_valid,
                          cast_w=cast_w),
        grid=(m // tm, n_j),
        in_specs=[
            pl.BlockSpec((tm, d), lambda i, j: (i, 0)),
            pl.BlockSpec((1, d), lambda i, j: (0, 0)),
            _weight_spec(w_u, d, tn, lambda i, j: 0, lambda i, j: j),
            _weight_spec(w_g, d, tn, lambda i, j: 0, lambda i, j: j),
            pl.BlockSpec((seqs_per_tile, FFN_CONV - 1, tn),
                         lambda i, j: ((i // tiles_per_seq) if tiles_per_seq else i, 0, j)),
            pl.BlockSpec((FFN_CONV, tn), lambda i, j: (0, j)),
            pl.BlockSpec((1, tn), lambda i, j: (0, j)),
        ],
        out_specs=out_specs,
        out_shape=out_shape,
        scratch_shapes=scratch,
        compiler_params=_params("arbitrary", "arbitrary"),
        name="ffn_up_act",
    )(x, g.reshape(1, d).astype(F32), w_u.array, w_g.array, buf0, cw.astype(F32),
      cb.reshape(1, D_FF).astype(F32))
    nb = outs[1].transpose(0, 2, 1, 3).reshape(b, FFN_CONV - 1, D_FF)
    return (outs[0], nb) + tuple(outs[2:])


def _suffix_matrix():
    j = lax.broadcasted_iota(jnp.int32, (2 * HEAD, 2 * HEAD), 0) % HEAD
    s = lax.broadcasted_iota(jnp.int32, (2 * HEAD, 2 * HEAD), 1)
    return jnp.where((s >= HEAD) | (j >= s), 1.0, 0.0).astype(BF16)


def _sb_weights(s, neg_bias, carry, suffix, mask):
    nz, lk = _sb_log_keep(s, neg_bias, mask)
    return _sb_finish(nz, _sb_suffix_sums(lk, suffix), carry, mask)


def _sb_log_keep(s, neg_bias, mask):
    nz = neg_bias - s
    neg_abs = pltpu.bitcast(pltpu.bitcast(nz, jnp.uint32) | jnp.uint32(0x80000000), F32)
    lk = jnp.minimum(nz, 0.0) - jnp.log2(1.0 + jnp.exp2(neg_abs))
    if mask is not None:
        lk = jnp.where(mask, lk, 0.0)
    return nz, lk


def _sb_suffix_sums(lk, suffix):
    hi = lk.astype(BF16)
    lo = (lk - hi.astype(F32)).astype(BF16)
    return jnp.dot(jnp.concatenate([hi, lo], axis=1), suffix, preferred_element_type=F32)


def _sb_finish(nz, c2, carry, mask):
    w = jnp.exp2((carry + c2[:, :HEAD]) - nz)
    if mask is not None:
        w = jnp.where(mask, w, 0.0)
    return w, carry + c2[:, HEAD:]


def _nt_dot(a, b):
    return lax.dot_general(a, b, (((1,), (1,)), ((), ())), preferred_element_type=F32)


def _sb_prompt_kernel(nbias_ref, q_ref, k_ref, v_ref, o_ref, carry_ref, acc_ref, *, tq, hp):
    hg = pl.program_id(1)
    qi = pl.program_id(2)
    suffix = _suffix_matrix()
    n_sub = tq // HEAD
    t_loc = lax.broadcasted_iota(jnp.int32, (tq, HEAD), 0)
    s_loc = lax.broadcasted_iota(jnp.int32, (tq, HEAD), 1)
    carry_ref[...] = jnp.zeros(carry_ref.shape, F32)
    acc_ref[...] = jnp.zeros(acc_ref.shape, F32)

    def block(start, masked):
        for e in range(hp):
            neg_bias = nbias_ref[hg * hp + e]
            s = _nt_dot(q_ref[0, :, e * HEAD:(e + 1) * HEAD], k_ref[0, e, pl.ds(start, tq), :])
            carry = carry_ref[e]
            ws = [None] * n_sub
            for j in range(n_sub - 1, -1, -1):
                mask = ((s_loc + j * HEAD) < t_loc) if masked else None
                w, carry = _sb_weights(s[:, j * HEAD:(j + 1) * HEAD], neg_bias, carry, suffix, mask)
                ws[j] = w.astype(BF16)
            carry_ref[e] = carry
            acc_ref[e] += jnp.dot(jnp.concatenate(ws, axis=1), v_ref[0, e, pl.ds(start, tq), :],
                                  preferred_element_type=F32)

    block(pl.multiple_of(qi * tq, tq), True)

    @pl.loop(0, qi)
    def _(i):
        block(pl.multiple_of((qi - 1 - i) * tq, tq), False)

    for e in range(hp):
        o_ref[0, :, e * HEAD:(e + 1) * HEAD] = acc_ref[e].astype(o_ref.dtype)


def sb_prompt(q, k, v, bias):
    b, _, t, _ = k.shape
    tq = _tile(t, 512, HEAD)
    hp = SB_HEADS_PER_STEP
    return pl.pallas_call(
        functools.partial(_sb_prompt_kernel, tq=tq, hp=hp),
        grid_spec=pltpu.PrefetchScalarGridSpec(
            num_scalar_prefetch=1,
            grid=(b, SB_HEADS // hp, t // tq),
            in_specs=[
                pl.BlockSpec((1, tq, hp * HEAD), lambda i, h, j, nb: (i, j, h)),
                pl.BlockSpec((1, hp, t, HEAD), lambda i, h, j, nb: (i, h, 0, 0)),
                pl.BlockSpec((1, hp, t, HEAD), lambda i, h, j, nb: (i, h, 0, 0)),
            ],
            out_specs=pl.BlockSpec((1, tq, hp * HEAD), lambda i, h, j, nb: (i, j, h)),
            scratch_shapes=[pltpu.VMEM((hp, tq, HEAD), F32), pltpu.VMEM((hp, tq, HEAD), F32)],
        ),
        out_shape=jax.ShapeDtypeStruct((b, t, D_RNN), BF16),
        compiler_params=_params("parallel", "parallel", "arbitrary"),
        name="sb_prompt",
    )(-LOG2E * bias.astype(F32), q, k, v)


def _sb_sample_kernel(pt_ref, q_ref, kn_ref, vn_ref, bias_ref, *rest, pages_per_step, t_valid):
    kp_refs = rest[:pages_per_step]
    vp_refs = rest[pages_per_step:2 * pages_per_step]
    o_ref, carry_ref, acc_ref = rest[2 * pages_per_step:]
    g = pl.program_id(1)
    rows = SB_HEADS * Q_ROWS
    suffix = _suffix_matrix()
    neg_bias = bias_ref[...]
    hcols = [slice(h * HEAD, (h + 1) * HEAD) for h in range(SB_HEADS)]
    q_heads = [q_ref[0, :, hcols[h]] for h in range(SB_HEADS)]

    def update(blocks, mask):
        n = len(blocks)
        nzs, lks = [], []
        for k_heads, _ in blocks:
            s = jnp.concatenate([_nt_dot(q_heads[h], k_heads[h])[0:Q_ROWS] for h in range(SB_HEADS)], axis=0)
            nz, lk = _sb_log_keep(s, neg_bias, mask)
            nzs.append(nz)
            lks.append(lk)
        c2 = _sb_suffix_sums(jnp.concatenate(lks, axis=0), suffix)
        carry = carry_ref[...]
        ws = []
        for r in range(n):
            w, carry = _sb_finish(nzs[r], c2[r * rows:(r + 1) * rows], carry, mask)
            ws.append(w)
        carry_ref[...] = carry
        zpad = jnp.zeros((SAMPLE_T_PAD - Q_ROWS, HEAD), F32)
        for h in range(SB_HEADS):
            rs = slice(h * Q_ROWS, (h + 1) * Q_ROWS)
            pv = None
            for r in range(n):
                wh = jnp.concatenate([ws[r][rs], zpad], axis=0).astype(BF16)
                d = jnp.dot(wh, blocks[r][1][h], preferred_element_type=F32)
                pv = d if pv is None else pv + d
            acc_ref[rs, :] += pv[0:Q_ROWS]

    @pl.when(g == 0)
    def _():
        carry_ref[...] = jnp.zeros((rows, HEAD), F32)
        acc_ref[...] = jnp.zeros((rows, HEAD), F32)
        kz = jnp.zeros((HEAD - SAMPLE_T_PAD, HEAD), BF16)
        k_heads = [jnp.concatenate([kn_ref[0, :, hcols[h]], kz], axis=0) for h in range(SB_HEADS)]
        v_heads = [jnp.concatenate([vn_ref[0, :, hcols[h]], kz], axis=0) for h in range(SB_HEADS)]
        i_q = lax.broadcasted_iota(jnp.int32, (rows, HEAD), 0) % Q_ROWS
        j_k = lax.broadcasted_iota(jnp.int32, (rows, HEAD), 1)
        update([(k_heads, v_heads)], (j_k < i_q) & (j_k < t_valid))

    update([([kp_refs[r][0, h].astype(BF16) for h in range(SB_HEADS)],
             [vp_refs[r][0, h].astype(BF16) for h in range(SB_HEADS)]) for r in range(pages_per_step)], None)

    @pl.when(g == pl.num_programs(1) - 1)
    def _():
        z8 = jnp.zeros((SAMPLE_T_PAD - Q_ROWS, HEAD), F32)
        for h in range(SB_HEADS):
            blk = jnp.concatenate([acc_ref[h * Q_ROWS:(h + 1) * Q_ROWS, :], z8], axis=0)
            o_ref[0, :, h * HEAD:(h + 1) * HEAD] = blk.astype(o_ref.dtype)


def sb_sample(q, k_new, v_new, cache_k, cache_v, page_table, bias, *, t_valid, pages_per_step=8):
    b, n_pages = page_table.shape
    page = cache_k.shape[2]
    assert page == HEAD and n_pages % pages_per_step == 0 and t_valid <= Q_ROWS
    assert cache_k.shape[1:] == (SB_HEADS, page, HEAD)
    rows = SB_HEADS * Q_ROWS
    bias_rows = jnp.broadcast_to(jnp.repeat(-LOG2E * bias.astype(F32), Q_ROWS)[:, None], (rows, HEAD))

    def page_spec(r):
        return pl.BlockSpec(
            (1, SB_HEADS, page, HEAD),
            lambda i, g, pt: (pt[i, n_pages - 1 - (g * pages_per_step + r)], 0, 0, 0))

    return pl.pallas_call(
        functools.partial(_sb_sample_kernel, pages_per_step=pages_per_step, t_valid=t_valid),
        grid_spec=pltpu.PrefetchScalarGridSpec(
            num_scalar_prefetch=1,
            grid=(b, n_pages // pages_per_step),
            in_specs=[
                pl.BlockSpec((1, SAMPLE_T_PAD, D_RNN), lambda i, g, pt: (i, 0, 0)),
                pl.BlockSpec((1, SAMPLE_T_PAD, D_RNN), lambda i, g, pt: (i, 0, 0)),
                pl.BlockSpec((1, SAMPLE_T_PAD, D_RNN), lambda i, g, pt: (i, 0, 0)),
                pl.BlockSpec((rows, HEAD), lambda i, g, pt: (0, 0)),
            ] + [page_spec(r) for r in range(pages_per_step)] + [page_spec(r) for r in range(pages_per_step)],
            out_specs=pl.BlockSpec((1, SAMPLE_T_PAD, D_RNN), lambda i, g, pt: (i, 0, 0)),
            scratch_shapes=[pltpu.VMEM((rows, HEAD), F32), pltpu.VMEM((rows, HEAD), F32)],
        ),
        out_shape=jax.ShapeDtypeStruct((b, SAMPLE_T_PAD, D_RNN), BF16),
        compiler_params=_params("parallel", "arbitrary"),
        name="sb_sample",
    )(page_table, q, k_new, v_new, bias_rows, *([cache_k] * pages_per_step), *([cache_v] * pages_per_step))


def _trunk(x3, t_valid, rg_h0, rg_conv0, ffn_conv0, mem_k, mem_v, sb_attend, head_major_kv, P, W, emit_w):
    b, t, d = x3.shape
    m = b * t
    x = x3.reshape(m, d)
    rg_h_out, rg_conv_out, ffn_conv_out = [], [], []
    k_f32 = v_f32 = k_bf = v_bf = None
    emitted = {}

    def nmm(key, x, g, **kw):
        out = norm_matmul(x, g, W[key], emit_w=emit_w, **kw)
        if not emit_w:
            return out
        *out, emitted[key] = out
        return out[0] if len(out) == 1 else tuple(out)

    def mmr(keys, acts, x, **kw):
        out = matmul_residual([(a, W[k]) for a, k in zip(acts, keys)], x, emit_w=emit_w, **kw)
        if not emit_w:
            return out
        for k, wb in zip(keys, out[1:]):
            emitted[k] = wb
        return out[0]

    for l in range(DEPTH):
        mem_gain = jnp.tile(P['mem_q_norm'][l] * ATTN_SCALE, MEM_HEADS)
        if l < N_A_LAYERS:
            p_rg = nmm(('in_a_rg', l), x, P['g_mix'][l])
            qm = nmm(('in_a_qm', l), x, P['g_mix'][l], head_gain=mem_gain)
            y_tok, h_last, cbuf = rglru(
                p_rg.reshape(b, t, 2 * D_RNN), rg_h0[l], rg_conv0[l], P['rg_conv_w'][l], P['rg_conv_b'][l],
                P['rg_gate_x_w'][l], P['rg_gate_x_b'][l], P['rg_gate_a_w'][l], P['rg_gate_a_b'][l],
                P['rg_lambda'][l], t_valid=t_valid)
            rg_h_out.append(h_last)
            rg_conv_out.append(cbuf)
            q3, q_col = qm.reshape(b, t, D_MEM), 0
        else:
            j = l - N_A_LAYERS
            gain = jnp.concatenate([jnp.tile(P['sb_q_norm'][j] * (ATTN_SCALE * LOG2E), SB_HEADS), mem_gain])
            pq = nmm(('in_b', l), x, P['g_mix'][l], head_gain=gain)
            q3, q_col = pq.reshape(b, t, D_RNN + D_MEM), D_RNN // D_MEM
            y_tok = sb_attend(q3, k_bf, v_bf, P['sb_beta_bias'][j])
        y_mem = mem_attn(q3, q_col, mem_k[l], mem_v[l])
        x = mmr([('out_tok', l), ('out_mem', l)], [y_tok.reshape(m, D_RNN), y_mem.reshape(m, D_MEM)], x)
        act, fbuf, *wbs = ffn_up_act(x, P['g_ffn'][l], W[('ffn_u', l)], W[('ffn_g', l)], ffn_conv0[l],
                                     P['ffn_conv_w'][l], P['ffn_conv_b'][l], seq=t, t_valid=t_valid,
                                     emit_w=emit_w)
        if emit_w:
            emitted[('ffn_u', l)], emitted[('ffn_g', l)] = wbs
        x = mmr([('ffn_down', l)], [act], x, n_k=3)
        ffn_conv_out.append(fbuf)
        if l == N_A_LAYERS - 1:
            k_gain = jnp.tile(P['kv_k_norm'], SB_HEADS)
            hm = t if head_major_kv else None
            k_f32, k_bf = nmm(('k', l), x, P['kv_norm'], head_gain=k_gain, out_dtypes=(F32, BF16),
                              head_major_seq=hm)
            v_f32, v_bf = nmm(('v', l), x, P['kv_norm'], out_dtypes=(F32, BF16), head_major_seq=hm)
            if not head_major_kv:
                k_bf = k_bf.reshape(b, t, D_RNN)
                v_bf = v_bf.reshape(b, t, D_RNN)
    if head_major_kv:
        k_out, v_out = k_f32.transpose(0, 2, 1, 3), v_f32.transpose(0, 2, 1, 3)
    else:
        k_out, v_out = k_f32.reshape(b, t, SB_HEADS, HEAD), v_f32.reshape(b, t, SB_HEADS, HEAD)
    return (x.reshape(b, t, d), jnp.stack(rg_h_out), jnp.stack(rg_conv_out), jnp.stack(ffn_conv_out),
            k_out, v_out, emitted)


def kernel(x_prompt, x_sample, mem_prompt, state_rglru_h, state_rglru_conv, state_ffn_conv, cache_mem_k, cache_mem_v, cache_sb_k, cache_sb_v, page_table, g_mix, g_ffn, w_in_a, rg_conv_w, rg_conv_b, rg_gate_x_w, rg_gate_x_b, rg_gate_a_w, rg_gate_a_b, rg_lambda, w_in_b, sb_q_norm, sb_beta_bias, kv_norm, w_kv, kv_k_norm, mem_norm, w_mem_kv, mem_q_norm, mem_k_norm, w_out, w_ffn_up, ffn_conv_w, ffn_conv_b, w_ffn_down):
    P = {'g_mix': g_mix, 'g_ffn': g_ffn, 'rg_conv_w': rg_conv_w, 'rg_conv_b': rg_conv_b,
         'rg_gate_x_b': rg_gate_x_b, 'rg_gate_a_b': rg_gate_a_b, 'rg_lambda': rg_lambda,
         'sb_q_norm': sb_q_norm, 'sb_beta_bias': sb_beta_bias, 'kv_norm': kv_norm,
         'kv_k_norm': kv_k_norm, 'mem_q_norm': mem_q_norm,
         'ffn_conv_w': ffn_conv_w, 'ffn_conv_b': ffn_conv_b,
         'rg_gate_x_w': rg_gate_x_w, 'rg_gate_a_w': rg_gate_a_w}
    dm = D_MODEL
    W = {}
    for l in range(DEPTH):
        if l < N_A_LAYERS:
            W[('in_a_rg', l)] = WeightView(w_in_a, l, 0, dm, 0, 2 * D_RNN)
            W[('in_a_qm', l)] = WeightView(w_in_a, l, 0, dm, 2 * D_RNN, D_MEM)
        else:
            W[('in_b', l)] = WeightView(w_in_b, l - N_A_LAYERS, 0, dm, 0, D_RNN + D_MEM)
        W[('out_tok', l)] = WeightView(w_out, l, 0, D_RNN, 0, dm)
        W[('out_mem', l)] = WeightView(w_out, l, D_RNN, D_MEM, 0, dm)
        W[('ffn_u', l)] = WeightView(w_ffn_up, l, 0, dm, 0, D_FF)
        W[('ffn_g', l)] = WeightView(w_ffn_up, l, 0, dm, D_FF, D_FF)
        W[('ffn_down', l)] = WeightView(w_ffn_down, l, 0, D_FF, 0, dm)
    W[('k', N_A_LAYERS - 1)] = WeightView(w_kv, None, 0, dm, 0, D_RNN)
    W[('v', N_A_LAYERS - 1)] = WeightView(w_kv, None, 0, dm, D_RNN, D_RNN)

    db, dec_seq, _ = x_sample.shape
    xs = jnp.pad(x_sample, ((0, 0), (0, SAMPLE_T_PAD - dec_seq), (0, 0)))

    cache_k_hm = cache_sb_k.transpose(0, 2, 1, 3)
    cache_v_hm = cache_sb_v.transpose(0, 2, 1, 3)

    def sb_paged(q3, k_bf, v_bf, bias):
        return sb_sample(q3, k_bf, v_bf, cache_k_hm, cache_v_hm, page_table, bias, t_valid=dec_seq)

    (y_s, sample_rglru_h, sample_rglru_conv, sample_ffn_conv, s_k, s_v, w_bf16) = _trunk(
        xs, dec_seq, state_rglru_h, state_rglru_conv, state_ffn_conv,
        cache_mem_k.reshape(DEPTH, db, MEM_TOKENS, D_MEM), cache_mem_v.reshape(DEPTH, db, MEM_TOKENS, D_MEM),
        sb_paged, False, P, W, True)
    W_prompt = {key: _bf16_view(arr) for key, arr in w_bf16.items()}

    bp, seq, d = x_prompt.shape
    mem2 = mem_prompt.reshape(bp * MEM_TOKENS, d)
    mk_list, mv_list = [], []
    for l in range(DEPTH):
        mk_list.append(norm_matmul(mem2, mem_norm[l], WeightView(w_mem_kv, l, 0, dm, 0, D_MEM),
                                   head_gain=jnp.tile(mem_k_norm[l], MEM_HEADS), out_dtypes=(F32,)))
        mv_list.append(norm_matmul(mem2, mem_norm[l], WeightView(w_mem_kv, l, 0, dm, D_MEM, D_MEM),
                                   out_dtypes=(F32,)))
    prompt_mem_k = jnp.stack(mk_list).reshape(DEPTH, bp, MEM_TOKENS, D_MEM)
    prompt_mem_v = jnp.stack(mv_list).reshape(DEPTH, bp, MEM_TOKENS, D_MEM)
    zeros_h = jnp.zeros((N_A_LAYERS, bp, D_RNN), F32)
    zeros_rc = jnp.zeros((N_A_LAYERS, bp, RG_CONV - 1, D_RNN), F32)
    zeros_fc = jnp.zeros((DEPTH, bp, FFN_CONV - 1, D_FF), F32)
    (y_prompt, prompt_rglru_h, prompt_rglru_conv, prompt_ffn_conv, prompt_sb_k, prompt_sb_v, _) = _trunk(
        x_prompt, seq, zeros_h, zeros_rc, zeros_fc, prompt_mem_k, prompt_mem_v, sb_prompt, True, P, W_prompt,
        False)

    mem_shape = (DEPTH, bp, MEM_TOKENS, MEM_HEADS, HEAD)
    return (y_prompt, y_s[:, :dec_seq],
            prompt_rglru_h, prompt_rglru_conv, prompt_ffn_conv,
            prompt_sb_k, prompt_sb_v, prompt_mem_k.reshape(mem_shape), prompt_mem_v.reshape(mem_shape),
            sample_rglru_h, sample_rglru_conv, sample_ffn_conv,
            s_k[:, :dec_seq], s_v[:, :dec_seq])
```

```python
import functools
import math
from typing import NamedTuple

import jax
import jax.numpy as jnp
from jax import lax
from jax.experimental import pallas as pl
from jax.experimental.pallas import tpu as pltpu

F32 = jnp.float32
BF16 = jnp.bfloat16

LANE = 128
SUBLANE = 8
VMEM_LIMIT_BYTES = 56 * 1024 * 1024

D_MODEL = 2048
DEPTH = 4
N_A_LAYERS = 2
D_RNN = 1536
HEAD = 128
RG_HEADS = D_RNN // HEAD
RG_CONV = 4
RG_C = 8.0
SB_HEADS = D_RNN // HEAD
MEM_TOKENS = 256
MEM_HEADS = 4
D_MEM = MEM_HEADS * HEAD
D_FF = 3 * D_MODEL
FFN_CONV = 3
EPS = 1e-6
ATTN_SCALE = HEAD ** -0.5
LOG2E = math.log2(math.e)
SAMPLE_T_PAD = 16
Q_ROWS = 8
FFN_ROW_CHUNK = 256
SB_HEADS_PER_STEP = 3


def _params(*sem):
    return pltpu.CompilerParams(dimension_semantics=sem, vmem_limit_bytes=VMEM_LIMIT_BYTES)


def _tile(n, pref, mult):
    if n <= pref:
        return n
    t = (pref // mult) * mult
    while t > mult and n % t:
        t -= mult
    assert n % t == 0, (n, pref, mult)
    return t


class WeightView(NamedTuple):
    array: jax.Array
    layer: int | None
    row0: int
    rows: int
    col0: int
    cols: int


def _weight_spec(wv, tk, tn, row_block, col_block):
    assert wv.row0 % tk == 0 and wv.col0 % tn == 0 and wv.rows % tk == 0 and wv.cols % tn == 0
    r0, c0 = wv.row0 // tk, wv.col0 // tn
    if wv.layer is None:
        return pl.BlockSpec((tk, tn), lambda *g: (r0 + row_block(*g), c0 + col_block(*g)))
    layer = wv.layer
    return pl.BlockSpec((None, tk, tn), lambda *g: (layer, r0 + row_block(*g), c0 + col_block(*g)))


def _softplus(z):
    return jnp.maximum(z, 0.0) + jnp.log1p(jnp.exp(-jnp.abs(z)))


def _head_rmsnorm(blk, gain):
    ms = jnp.mean(blk * blk, axis=-1, keepdims=True)
    return blk * lax.rsqrt(ms + EPS) * gain


def _norm_matmul_kernel(*refs, head_norm, head_major, n_out, emit_w):
    x_ref, g_ref, w_ref = refs[:3]
    pos = 3
    hg_ref = None
    if head_norm:
        hg_ref = refs[pos]
        pos += 1
    o_refs = refs[pos:pos + n_out]
    pos += n_out
    wb_ref = None
    if emit_w:
        wb_ref = refs[pos]
        pos += 1
    xn_ref = refs[pos]

    @pl.when(pl.program_id(1) == 0)
    def _():
        x = x_ref[...]
        ms = jnp.mean(x * x, axis=-1, keepdims=True)
        xn_ref[...] = (x * lax.rsqrt(ms + EPS) * g_ref[...]).astype(BF16)

    wb = w_ref[...].astype(BF16)
    if emit_w:
        wb_ref[...] = wb
    acc = jnp.dot(xn_ref[...], wb, preferred_element_type=F32)
    tn = acc.shape[1]
    if head_norm or head_major:
        for h in range(tn // HEAD):
            cs = slice(h * HEAD, (h + 1) * HEAD)
            y = _head_rmsnorm(acc[:, cs], hg_ref[:, cs]) if head_norm else acc[:, cs]
            for o_ref in o_refs:
                if head_major:
                    o_ref[0, h] = y.astype(o_ref.dtype)
                else:
                    o_ref[:, cs] = y.astype(o_ref.dtype)
    else:
        for o_ref in o_refs:
            o_ref[...] = acc.astype(o_ref.dtype)


def _bf16_view(arr):
    return WeightView(arr, None, 0, arr.shape[0], 0, arr.shape[1])


def norm_matmul(x, g, w, *, head_gain=None, out_dtypes=(BF16,), head_major_seq=None, emit_w=False,
                tm_pref=1024, tn_pref=1024):
    m, d = x.shape
    n = w.cols
    assert w.rows == d
    tm = _tile(m, tm_pref, SUBLANE)
    tn = _tile(n, tn_pref, 2 * LANE)
    assert not emit_w or (m == tm and head_major_seq is None)
    if head_major_seq is None:
        out_specs = [pl.BlockSpec((tm, tn), lambda i, j: (i, j)) for _ in out_dtypes]
        out_shape = [jax.ShapeDtypeStruct((m, n), dt) for dt in out_dtypes]
    else:
        seq = head_major_seq
        assert seq % tm == 0 and m % seq == 0
        per_seq = seq // tm
        out_specs = [pl.BlockSpec((1, tn // HEAD, tm, HEAD), lambda i, j: (i // per_seq, j, i % per_seq, 0))
                     for _ in out_dtypes]
        out_shape = [jax.ShapeDtypeStruct((m // seq, n // HEAD, seq, HEAD), dt) for dt in out_dtypes]
    in_specs = [
        pl.BlockSpec((tm, d), lambda i, j: (i, 0)),
        pl.BlockSpec((1, d), lambda i, j: (0, 0)),
        _weight_spec(w, d, tn, lambda i, j: 0, lambda i, j: j),
    ]
    args = [x, g.reshape(1, d).astype(F32), w.array]
    if head_gain is not None:
        in_specs.append(pl.BlockSpec((1, tn), lambda i, j: (0, j)))
        args.append(head_gain.reshape(1, n).astype(F32))
    if emit_w:
        out_specs.append(pl.BlockSpec((d, tn), lambda i, j: (0, j)))
        out_shape.append(jax.ShapeDtypeStruct((d, n), BF16))
    outs = pl.pallas_call(
        functools.partial(_norm_matmul_kernel, head_norm=head_gain is not None,
                          head_major=head_major_seq is not None, n_out=len(out_dtypes), emit_w=emit_w),
        grid=(m // tm, n // tn),
        in_specs=in_specs,
        out_specs=out_specs,
        out_shape=out_shape,
        scratch_shapes=[pltpu.VMEM((tm, d), BF16)],
        compiler_params=_params("parallel", "arbitrary"),
        name="norm_matmul",
    )(*args)
    return outs[0] if len(outs) == 1 else tuple(outs)


def _matmul_residual_kernel(*refs, n_pairs, emit_w):
    x_ref = refs[2 * n_pairs]
    o_ref = refs[2 * n_pairs + 1]
    wb_refs = refs[2 * n_pairs + 2:]

    @pl.when(pl.program_id(2) == 0)
    def _():
        o_ref[...] = x_ref[...]

    acc = None
    for p in range(n_pairs):
        wb = refs[2 * p + 1][...].astype(BF16)
        if emit_w:
            wb_refs[p][...] = wb
        d = jnp.dot(refs[2 * p][...], wb, preferred_element_type=F32)
        acc = d if acc is None else acc + d
    o_ref[...] += acc


def matmul_residual(pairs, x, *, n_k=1, emit_w=False, tm_pref=1024, tn_pref=1024):
    m, n = x.shape
    tm = _tile(m, tm_pref, SUBLANE)
    tn = _tile(n, tn_pref, 2 * LANE)
    assert not emit_w or m == tm
    out_specs = [pl.BlockSpec((tm, tn), lambda i, j, k: (i, j))]
    out_shape = [jax.ShapeDtypeStruct((m, n), F32)]
    in_specs, args = [], []
    for a, w in pairs:
        kp = a.shape[1]
        assert kp % n_k == 0 and (w.rows, w.cols) == (kp, n)
        tk = kp // n_k
        in_specs.append(pl.BlockSpec((tm, tk), lambda i, j, k: (i, k)))
        in_specs.append(_weight_spec(w, tk, tn, lambda i, j, k: k, lambda i, j, k: j))
        args += [a, w.array]
        if emit_w:
            out_specs.append(pl.BlockSpec((tk, tn), lambda i, j, k: (k, j)))
            out_shape.append(jax.ShapeDtypeStruct((kp, n), BF16))
    in_specs.append(pl.BlockSpec((tm, tn), lambda i, j, k: (i, j)))
    args.append(x)
    outs = pl.pallas_call(
        functools.partial(_matmul_residual_kernel, n_pairs=len(pairs), emit_w=emit_w),
        grid=(m // tm, n // tn, n_k),
        in_specs=in_specs,
        out_specs=out_specs,
        out_shape=out_shape,
        compiler_params=_params("parallel", "parallel", "arbitrary"),
        name="matmul_residual",
    )(*args)
    return tuple(outs) if emit_w else outs[0]


def _rglru_kernel(p_ref, h0_ref, cb0_ref, cw_ref, cbias_ref, wgx_ref, bgx_ref, wga_ref, bga_ref, lam_ref,
                  y_ref, hl_ref, cbo_ref, ext_ref, hc_ref, *, tr, tile_last, r_last):
    t = pl.program_id(1)
    halo = SUBLANE

    @pl.when(t == 0)
    def _():
        ext_ref[0:halo, :] = jnp.zeros((halo, D_RNN), F32)
        ext_ref[halo - (RG_CONV - 1):halo, :] = cb0_ref[0]
        hc_ref[...] = h0_ref[0]

    ext_ref[halo:halo + tr, :] = p_ref[0, :, 0:D_RNN].astype(F32)
    sub_row = lax.broadcasted_iota(jnp.int32, (tr // SUBLANE, SUBLANE, HEAD), 1)
    for h in range(RG_HEADS):
        cs = slice(h * HEAD, (h + 1) * HEAD)
        xc = cbias_ref[:, cs] + ext_ref[halo - 3:halo - 3 + tr, cs] * cw_ref[0:1, cs]
        for j in range(1, RG_CONV):
            xc = xc + ext_ref[halo - 3 + j:halo - 3 + j + tr, cs] * cw_ref[j:j + 1, cs]
        xcb = xc.astype(BF16)
        gx = jax.nn.sigmoid(jnp.dot(xcb, wgx_ref[h].astype(BF16), preferred_element_type=F32) + bgx_ref[:, cs])
        ga = jax.nn.sigmoid(jnp.dot(xcb, wga_ref[h].astype(BF16), preferred_element_type=F32) + bga_ref[:, cs])
        log_a = (-RG_C) * ga * _softplus(-lam_ref[:, cs])
        a = jnp.exp(log_a)
        u = jnp.sqrt(-jnp.tanh(log_a) * (1.0 + a * a)) * (gx * xc)
        groups = tr // SUBLANE
        a3 = a.reshape(groups, SUBLANE, HEAD)
        u3 = u.reshape(groups, SUBLANE, HEAD)
        d = 1
        while d < SUBLANE:
            live = sub_row >= d
            u3 = jnp.where(live, a3 * pltpu.roll(u3, d, 1) + u3, u3)
            a3 = jnp.where(live, a3 * pltpu.roll(a3, d, 1), a3)
            d *= 2
        h_prev = jnp.broadcast_to(hc_ref[:, cs], (SUBLANE, HEAD))
        h_groups = []
        for k in range(groups):
            h_k = a3[k] * h_prev + u3[k]
            h_groups.append(h_k)
            h_prev = jnp.broadcast_to(h_k[SUBLANE - 1:SUBLANE, :], (SUBLANE, HEAD))
        u = jnp.concatenate(h_groups, axis=0)
        gate = jax.nn.gelu(p_ref[0, :, D_RNN + h * HEAD:D_RNN + (h + 1) * HEAD].astype(F32))
        y_ref[0, :, cs] = (u * gate).astype(y_ref.dtype)
        hc_ref[:, cs] = u[tr - 1:tr, :]

        @pl.when(t == tile_last)
        def _():
            hl_ref[0, :, cs] = u[r_last:r_last + 1, :]

    @pl.when(t == tile_last)
    def _():
        cbo_ref[0] = ext_ref[halo + r_last - 2:halo + r_last + 1, :]

    if tr >= halo:
        ext_ref[0:halo, :] = ext_ref[tr:tr + halo, :]


def rglru(p, h0, cb0, cw, cbias, wgx, bgx, wga, bga, lam, *, t_valid):
    b, t, _ = p.shape
    tr = _tile(t, 256, 16)
    row = lambda v: v.reshape(1, D_RNN).astype(F32)
    full = lambda shape: pl.BlockSpec(shape, lambda i, j: (0,) * len(shape))
    y, hl, cbo = pl.pallas_call(
        functools.partial(_rglru_kernel, tr=tr, tile_last=(t_valid - 1) // tr, r_last=(t_valid - 1) % tr),
        grid=(b, t // tr),
        in_specs=[
            pl.BlockSpec((1, tr, 2 * D_RNN), lambda i, j: (i, j, 0)),
            pl.BlockSpec((1, 1, D_RNN), lambda i, j: (i, 0, 0)),
            pl.BlockSpec((1, RG_CONV - 1, D_RNN), lambda i, j: (i, 0, 0)),
            full((RG_CONV, D_RNN)), full((1, D_RNN)),
            full((RG_HEADS, HEAD, HEAD)), full((1, D_RNN)),
            full((RG_HEADS, HEAD, HEAD)), full((1, D_RNN)),
            full((1, D_RNN)),
        ],
        out_specs=[
            pl.BlockSpec((1, tr, D_RNN), lambda i, j: (i, j, 0)),
            pl.BlockSpec((1, 1, D_RNN), lambda i, j: (i, 0, 0)),
            pl.BlockSpec((1, RG_CONV - 1, D_RNN), lambda i, j: (i, 0, 0)),
        ],
        out_shape=[
            jax.ShapeDtypeStruct((b, t, D_RNN), BF16),
            jax.ShapeDtypeStruct((b, 1, D_RNN), F32),
            jax.ShapeDtypeStruct((b, RG_CONV - 1, D_RNN), F32),
        ],
        scratch_shapes=[pltpu.VMEM((tr + SUBLANE, D_RNN), F32), pltpu.VMEM((1, D_RNN), F32)],
        compiler_params=_params("parallel", "arbitrary"),
        name="rglru",
    )(p, h0.reshape(b, 1, D_RNN), cb0, cw.astype(F32), row(cbias), wgx, row(bgx), wga, row(bga), row(lam))
    return y, hl.reshape(b, D_RNN), cbo


def _mem_attn_kernel(q_ref, k_ref, v_ref, o_ref):
    for h in range(MEM_HEADS):
        cs = slice(h * HEAD, (h + 1) * HEAD)
        q = q_ref[0, :, cs]
        k = k_ref[0, :, cs].astype(BF16)
        v = v_ref[0, :, cs].astype(BF16)
        s = lax.dot_general(q, k, (((1,), (1,)), ((), ())), preferred_element_type=F32)
        e = jnp.exp(s - jnp.max(s, axis=-1, keepdims=True))
        p = e / jnp.sum(e, axis=-1, keepdims=True)
        o_ref[0, :, cs] = jnp.dot(p.astype(BF16), v, preferred_element_type=F32).astype(o_ref.dtype)


def mem_attn(q, q_col_block, mk, mv):
    b, t, _ = q.shape
    tq = _tile(t, 512, 16)
    return pl.pallas_call(
        _mem_attn_kernel,
        grid=(b, t // tq),
        in_specs=[
            pl.BlockSpec((1, tq, D_MEM), lambda i, j: (i, j, q_col_block)),
            pl.BlockSpec((1, MEM_TOKENS, D_MEM), lambda i, j: (i, 0, 0)),
            pl.BlockSpec((1, MEM_TOKENS, D_MEM), lambda i, j: (i, 0, 0)),
        ],
        out_specs=pl.BlockSpec((1, tq, D_MEM), lambda i, j: (i, j, 0)),
        out_shape=jax.ShapeDtypeStruct((b, t, D_MEM), BF16),
        compiler_params=_params("parallel", "parallel"),
        name="mem_attn",
    )(q, mk, mv)


def _shift_rows(u, shift, head_rows):
    s = pltpu.roll(u, shift, 0)
    row = lax.broadcasted_iota(jnp.int32, (SUBLANE, u.shape[1]), 0)
    top = s[0:SUBLANE]
    for r, v in enumerate(head_rows):
        top = jnp.where(row == r, v, top)
    return jnp.concatenate([top, s[SUBLANE:]], axis=0)


def _conv_gelu_gate(u, gate, prev2, prev1, cw_ref, cb_ref):
    uc = (cb_ref[...] + _shift_rows(u, 2, [prev2, prev1]) * cw_ref[0:1, :]
          + _shift_rows(u, 1, [prev1]) * cw_ref[1:2, :] + u * cw_ref[2:3, :])
    return jax.nn.gelu(uc) * gate


def _ffn_up_act_kernel(x_ref, g_ref, wu_in_ref, wg_in_ref, b0_ref, cw_ref, cb_ref, act_ref, nb_ref, *rest,
                       seq, tiles_per_seq, t_valid, cast_w):
    i = pl.program_id(0)
    j = pl.program_id(1)
    tm = x_ref.shape[0]
    if cast_w:
        wu_ref, wg_ref, xn_ref, tail_ref = rest
        wu_ref[...] = wu_in_ref[...].astype(BF16)
        wg_ref[...] = wg_in_ref[...].astype(BF16)
    else:
        xn_ref, tail_ref = rest
        wu_ref, wg_ref = wu_in_ref, wg_in_ref

    @pl.when(j == 0)
    def _():
        x = x_ref[...]
        ms = jnp.mean(x * x, axis=-1, keepdims=True)
        xn_ref[...] = (x * lax.rsqrt(ms + EPS) * g_ref[...]).astype(BF16)

    if tiles_per_seq >= 1:
        @pl.when(i == 0)
        def _():
            tail_ref[j] = jnp.zeros(tail_ref.shape[1:], F32)

        first = (i % tiles_per_seq) == 0
        prev2 = jnp.where(first, b0_ref[0, 0:1, :], tail_ref[j, SUBLANE - 2:SUBLANE - 1, :])
        prev1 = jnp.where(first, b0_ref[0, 1:2, :], tail_ref[j, SUBLANE - 1:SUBLANE, :])
        rc = min(tm, FFN_ROW_CHUNK)
        for c in range(tm // rc):
            xs = xn_ref[c * rc:(c + 1) * rc, :]
            u = jnp.dot(xs, wu_ref[...], preferred_element_type=F32)
            gate = jnp.dot(xs, wg_ref[...], preferred_element_type=F32)
            act_ref[c * rc:(c + 1) * rc, :] = _conv_gelu_gate(u, gate, prev2, prev1, cw_ref, cb_ref
                                                              ).astype(act_ref.dtype)
            prev2, prev1 = u[rc - 2:rc - 1], u[rc - 1:rc]
        tail_ref[j] = u[rc - SUBLANE:rc]

        @pl.when((i % tiles_per_seq) == tiles_per_seq - 1)
        def _():
            nb_ref[i // tiles_per_seq, j] = u[rc - 2:rc]
    else:
        xn = xn_ref[...]
        u = jnp.dot(xn, wu_ref[...], preferred_element_type=F32)
        gate = jnp.dot(xn, wg_ref[...], preferred_element_type=F32)
        for b in range(tm // seq):
            rs = slice(b * seq, (b + 1) * seq)
            a = _conv_gelu_gate(u[rs], gate[rs], b0_ref[b, 0:1, :], b0_ref[b, 1:2, :], cw_ref, cb_ref)
            act_ref[rs, :] = a.astype(act_ref.dtype)
            nb_ref[(tm // seq) * i + b, j] = u[b * seq + t_valid - 2:b * seq + t_valid]


def ffn_up_act(x, g, w_u, w_g, buf0, cw, cb, *, seq, t_valid, emit_w=False, tm_pref=1024, tn=512):
    m, d = x.shape
    assert (w_u.rows, w_u.cols) == (d, D_FF) and (w_g.rows, w_g.cols) == (d, D_FF)
    b = m // seq
    tm = _tile(m, tm_pref, SUBLANE)
    assert t_valid >= FFN_CONV - 1 and (seq % tm == 0 or tm % seq == 0)
    tiles_per_seq = seq // tm if tm <= seq else 0
    assert tiles_per_seq == 0 or t_valid == seq
    seqs_per_tile = max(tm // seq, 1)
    n_j = D_FF // tn
    cast_w = w_u.array.dtype != BF16
    assert not emit_w or (cast_w and m == tm)
    out_specs = [
        pl.BlockSpec((tm, tn), lambda i, j: (i, j)),
        pl.BlockSpec((b, n_j, FFN_CONV - 1, tn), lambda i, j: (0, 0, 0, 0)),
    ]
    out_shape = [
        jax.ShapeDtypeStruct((m, D_FF), BF16),
        jax.ShapeDtypeStruct((b, n_j, FFN_CONV - 1, tn), F32),
    ]
    scratch = [pltpu.VMEM((tm, d), BF16), pltpu.VMEM((n_j, SUBLANE, tn), F32)]
    if emit_w:
        out_specs += [pl.BlockSpec((d, tn), lambda i, j: (0, j)), pl.BlockSpec((d, tn), lambda i, j: (0, j))]
        out_shape += [jax.ShapeDtypeStruct((d, D_FF), BF16), jax.ShapeDtypeStruct((d, D_FF), BF16)]
    elif cast_w:
        scratch = [pltpu.VMEM((d, tn), BF16), pltpu.VMEM((d, tn), BF16)] + scratch
    outs = pl.pallas_call(
        functools.partial(_ffn_up_act_kernel, seq=seq, tiles_per_seq=tiles_per_seq, t_valid=t_valid,
                          cast_w=cast_w),
        grid=(m // tm, n_j),
        in_specs=[
            pl.BlockSpec((tm, d), lambda i, j: (i, 0)),
            pl.BlockSpec((1, d), lambda i, j: (0, 0)),
            _weight_spec(w_u, d, tn, lambda i, j: 0, lambda i, j: j),
            _weight_spec(w_g, d, tn, lambda i, j: 0, lambda i, j: j),
            pl.BlockSpec((seqs_per_tile, FFN_CONV - 1, tn),
                         lambda i, j: ((i // tiles_per_seq) if tiles_per_seq else i, 0, j)),
            pl.BlockSpec((FFN_CONV, tn), lambda i, j: (0, j)),
            pl.BlockSpec((1, tn), lambda i, j: (0, j)),
        ],
        out_specs=out_specs,
        out_shape=out_shape,
        scratch_shapes=scratch,
        compiler_params=_params("arbitrary", "arbitrary"),
        name="ffn_up_act",
    )(x, g.reshape(1, d).astype(F32), w_u.array, w_g.array, buf0, cw.astype(F32),
      cb.reshape(1, D_FF).astype(F32))
    nb = outs[1].transpose(0, 2, 1, 3).reshape(b, FFN_CONV - 1, D_FF)
    return (outs[0], nb) + tuple(outs[2:])


def _suffix_matrix():
    j = lax.broadcasted_iota(jnp.int32, (2 * HEAD, 2 * HEAD), 0) % HEAD
    s = lax.broadcasted_iota(jnp.int32, (2 * HEAD, 2 * HEAD), 1)
    return jnp.where((s >= HEAD) | (j >= s), 1.0, 0.0).astype(BF16)


def _sb_weights(s, neg_bias, carry, suffix, mask):
    nz, lk = _sb_log_keep(s, neg_bias, mask)
    return _sb_finish(nz, _sb_suffix_sums(lk, suffix), carry, mask)


def _sb_log_keep(s, neg_bias, mask):
    nz = neg_bias - s
    neg_abs = pltpu.bitcast(pltpu.bitcast(nz, jnp.uint32) | jnp.uint32(0x80000000), F32)
    lk = jnp.minimum(nz, 0.0) - jnp.log2(1.0 + jnp.exp2(neg_abs))
    if mask is not None:
        lk = jnp.where(mask, lk, 0.0)
    return nz, lk


def _sb_suffix_sums(lk, suffix):
    hi = lk.astype(BF16)
    lo = (lk - hi.astype(F32)).astype(BF16)
    return jnp.dot(jnp.concatenate([hi, lo], axis=1), suffix, preferred_element_type=F32)


def _sb_finish(nz, c2, carry, mask):
    w = jnp.exp2((carry + c2[:, :HEAD]) - nz)
    if mask is not None:
        w = jnp.where(mask, w, 0.0)
    return w, carry + c2[:, HEAD:]


def _nt_dot(a, b):
    return lax.dot_general(a, b, (((1,), (1,)), ((), ())), preferred_element_type=F32)


def _sb_prompt_kernel(nbias_ref, q_ref, k_ref, v_ref, o_ref, carry_ref, acc_ref, *, tq, hp):
    hg = pl.program_id(1)
    qi = pl.program_id(2)
    suffix = _suffix_matrix()
    n_sub = tq // HEAD
    carry_ref[...] = jnp.zeros(carry_ref.shape, F32)
    acc_ref[...] = jnp.zeros(acc_ref.shape, F32)

    def block(start, masked):
        for e in range(hp):
            neg_bias = nbias_ref[hg * hp + e]
            s = _nt_dot(q_ref[0, :, e * HEAD:(e + 1) * HEAD], k_ref[0, e, pl.ds(start, tq), :])
            carry = carry_ref[e]
            ws = [None] * n_sub
            for j in range(n_sub - 1, -1, -1):
                if masked:
                    r0 = j * HEAD
                    mask = (lax.broadcasted_iota(jnp.int32, (tq - r0, HEAD), 1)
                            < lax.broadcasted_iota(jnp.int32, (tq - r0, HEAD), 0))
                    w, c = _sb_weights(s[r0:, j * HEAD:(j + 1) * HEAD], neg_bias, carry[r0:], suffix, mask)
                    if r0:
                        w = jnp.concatenate([jnp.zeros((r0, HEAD), F32), w], axis=0)
                        c = jnp.concatenate([carry[:r0], c], axis=0)
                    carry = c
                else:
                    w, carry = _sb_weights(s[:, j * HEAD:(j + 1) * HEAD], neg_bias, carry, suffix, None)
                ws[j] = w.astype(BF16)
            carry_ref[e] = carry
            acc_ref[e] += jnp.dot(jnp.concatenate(ws, axis=1), v_ref[0, e, pl.ds(start, tq), :],
                                  preferred_element_type=F32)

    block(pl.multiple_of(qi * tq, tq), True)

    @pl.loop(0, qi)
    def _(i):
        block(pl.multiple_of((qi - 1 - i) * tq, tq), False)

    for e in range(hp):
        o_ref[0, :, e * HEAD:(e + 1) * HEAD] = acc_ref[e].astype(o_ref.dtype)


def sb_prompt(q, k, v, bias):
    b, _, t, _ = k.shape
    tq = _tile(t, 512, HEAD)
    hp = SB_HEADS_PER_STEP
    return pl.pallas_call(
        functools.partial(_sb_prompt_kernel, tq=tq, hp=hp),
        grid_spec=pltpu.PrefetchScalarGridSpec(
            num_scalar_prefetch=1,
            grid=(b, SB_HEADS // hp, t // tq),
            in_specs=[
                pl.BlockSpec((1, tq, hp * HEAD), lambda i, h, j, nb: (i, j, h)),
                pl.BlockSpec((1, hp, t, HEAD), lambda i, h, j, nb: (i, h, 0, 0)),
                pl.BlockSpec((1, hp, t, HEAD), lambda i, h, j, nb: (i, h, 0, 0)),
            ],
            out_specs=pl.BlockSpec((1, tq, hp * HEAD), lambda i, h, j, nb: (i, j, h)),
            scratch_shapes=[pltpu.VMEM((hp, tq, HEAD), F32), pltpu.VMEM((hp, tq, HEAD), F32)],
        ),
        out_shape=jax.ShapeDtypeStruct((b, t, D_RNN), BF16),
        compiler_params=_params("parallel", "parallel", "arbitrary"),
        name="sb_prompt",
    )(-LOG2E * bias.astype(F32), q, k, v)


def _sb_sample_kernel(pt_ref, q_ref, kn_ref, vn_ref, bias_ref, *rest, pages_per_step, t_valid):
    kp_refs = rest[:pages_per_step]
    vp_refs = rest[pages_per_step:2 * pages_per_step]
    o_ref, carry_ref, acc_ref = rest[2 * pages_per_step:]
    g = pl.program_id(1)
    rows = SB_HEADS * Q_ROWS
    suffix = _suffix_matrix()
    neg_bias = bias_ref[...]
    hcols = [slice(h * HEAD, (h + 1) * HEAD) for h in range(SB_HEADS)]
    q_heads = [q_ref[0, :, hcols[h]] for h in range(SB_HEADS)]

    def update(blocks, mask):
        n = len(blocks)
        nzs, lks = [], []
        for k_heads, _ in blocks:
            s = jnp.concatenate([_nt_dot(q_heads[h], k_heads[h])[0:Q_ROWS] for h in range(SB_HEADS)], axis=0)
            nz, lk = _sb_log_keep(s, neg_bias, mask)
            nzs.append(nz)
            lks.append(lk)
        c2 = _sb_suffix_sums(jnp.concatenate(lks, axis=0), suffix)
        carry = carry_ref[...]
        ws = []
        for r in range(n):
            w, carry = _sb_finish(nzs[r], c2[r * rows:(r + 1) * rows], carry, mask)
            ws.append(w)
        carry_ref[...] = carry
        zpad = jnp.zeros((SAMPLE_T_PAD - Q_ROWS, HEAD), F32)
        for h in range(SB_HEADS):
            rs = slice(h * Q_ROWS, (h + 1) * Q_ROWS)
            pv = None
            for r in range(n):
                wh = jnp.concatenate([ws[r][rs], zpad], axis=0).astype(BF16)
                d = jnp.dot(wh, blocks[r][1][h], preferred_element_type=F32)
                pv = d if pv is None else pv + d
            acc_ref[rs, :] += pv[0:Q_ROWS]

    @pl.when(g == 0)
    def _():
        carry_ref[...] = jnp.zeros((rows, HEAD), F32)
        acc_ref[...] = jnp.zeros((rows, HEAD), F32)
        kz = jnp.zeros((HEAD - SAMPLE_T_PAD, HEAD), BF16)
        k_heads = [jnp.concatenate([kn_ref[0, :, hcols[h]], kz], axis=0) for h in range(SB_HEADS)]
        v_heads = [jnp.concatenate([vn_ref[0, :, hcols[h]], kz], axis=0) for h in range(SB_HEADS)]
        i_q = lax.broadcasted_iota(jnp.int32, (rows, HEAD), 0) % Q_ROWS
        j_k = lax.broadcasted_iota(jnp.int32, (rows, HEAD), 1)
        update([(k_heads, v_heads)], (j_k < i_q) & (j_k < t_valid))

    update([([kp_refs[r][0, h].astype(BF16) for h in range(SB_HEADS)],
             [vp_refs[r][0, h].astype(BF16) for h in range(SB_HEADS)]) for r in range(pages_per_step)], None)

    @pl.when(g == pl.num_programs(1) - 1)
    def _():
        z8 = jnp.zeros((SAMPLE_T_PAD - Q_ROWS, HEAD), F32)
        for h in range(SB_HEADS):
            blk = jnp.concatenate([acc_ref[h * Q_ROWS:(h + 1) * Q_ROWS, :], z8], axis=0)
            o_ref[0, :, h * HEAD:(h + 1) * HEAD] = blk.astype(o_ref.dtype)


def sb_sample(q, k_new, v_new, cache_k, cache_v, page_table, bias, *, t_valid, pages_per_step=8):
    b, n_pages = page_table.shape
    page = cache_k.shape[2]
    assert page == HEAD and n_pages % pages_per_step == 0 and t_valid <= Q_ROWS
    assert cache_k.shape[1:] == (SB_HEADS, page, HEAD)
    rows = SB_HEADS * Q_ROWS
    bias_rows = jnp.broadcast_to(jnp.repeat(-LOG2E * bias.astype(F32), Q_ROWS)[:, None], (rows, HEAD))

    def page_spec(r):
        return pl.BlockSpec(
            (1, SB_HEADS, page, HEAD),
            lambda i, g, pt: (pt[i, n_pages - 1 - (g * pages_per_step + r)], 0, 0, 0))

    return pl.pallas_call(
        functools.partial(_sb_sample_kernel, pages_per_step=pages_per_step, t_valid=t_valid),
        grid_spec=pltpu.PrefetchScalarGridSpec(
            num_scalar_prefetch=1,
            grid=(b, n_pages // pages_per_step),
            in_specs=[
                pl.BlockSpec((1, SAMPLE_T_PAD, D_RNN), lambda i, g, pt: (i, 0, 0)),
                pl.BlockSpec((1, SAMPLE_T_PAD, D_RNN), lambda i, g, pt: (i, 0, 0)),
                pl.BlockSpec((1, SAMPLE_T_PAD, D_RNN), lambda i, g, pt: (i, 0, 0)),
                pl.BlockSpec((rows, HEAD), lambda i, g, pt: (0, 0)),
            ] + [page_spec(r) for r in range(pages_per_step)] + [page_spec(r) for r in range(pages_per_step)],
            out_specs=pl.BlockSpec((1, SAMPLE_T_PAD, D_RNN), lambda i, g, pt: (i, 0, 0)),
            scratch_shapes=[pltpu.VMEM((rows, HEAD), F32), pltpu.VMEM((rows, HEAD), F32)],
        ),
        out_shape=jax.ShapeDtypeStruct((b, SAMPLE_T_PAD, D_RNN), BF16),
        compiler_params=_params("parallel", "arbitrary"),
        name="sb_sample",
    )(page_table, q, k_new, v_new, bias_rows, *([cache_k] * pages_per_step), *([cache_v] * pages_per_step))


def _trunk(x3, t_valid, rg_h0, rg_conv0, ffn_conv0, mem_k, mem_v, sb_attend, head_major_kv, P, W, emit_w):
    b, t, d = x3.shape
    m = b * t
    x = x3.reshape(m, d)
    rg_h_out, rg_conv_out, ffn_conv_out = [], [], []
    k_f32 = v_f32 = k_bf = v_bf = None
    emitted = {}

    def nmm(key, x, g, **kw):
        out = norm_matmul(x, g, W[key], emit_w=emit_w, **kw)
        if not emit_w:
            return out
        *out, emitted[key] = out
        return out[0] if len(out) == 1 else tuple(out)

    def mmr(keys, acts, x, **kw):
        out = matmul_residual([(a, W[k]) for a, k in zip(acts, keys)], x, emit_w=emit_w, **kw)
        if not emit_w:
            return out
        for k, wb in zip(keys, out[1:]):
            emitted[k] = wb
        return out[0]

    for l in range(DEPTH):
        mem_gain = jnp.tile(P['mem_q_norm'][l] * ATTN_SCALE, MEM_HEADS)
        if l < N_A_LAYERS:
            p_rg = nmm(('in_a_rg', l), x, P['g_mix'][l])
            qm = nmm(('in_a_qm', l), x, P['g_mix'][l], head_gain=mem_gain)
            y_tok, h_last, cbuf = rglru(
                p_rg.reshape(b, t, 2 * D_RNN), rg_h0[l], rg_conv0[l], P['rg_conv_w'][l], P['rg_conv_b'][l],
                P['rg_gate_x_w'][l], P['rg_gate_x_b'][l], P['rg_gate_a_w'][l], P['rg_gate_a_b'][l],
                P['rg_lambda'][l], t_valid=t_valid)
            rg_h_out.append(h_last)
            rg_conv_out.append(cbuf)
            q3, q_col = qm.reshape(b, t, D_MEM), 0
        else:
            j = l - N_A_LAYERS
            gain = jnp.concatenate([jnp.tile(P['sb_q_norm'][j] * (ATTN_SCALE * LOG2E), SB_HEADS), mem_gain])
            pq = nmm(('in_b', l), x, P['g_mix'][l], head_gain=gain)
            q3, q_col = pq.reshape(b, t, D_RNN + D_MEM), D_RNN // D_MEM
            y_tok = sb_attend(q3, k_bf, v_bf, P['sb_beta_bias'][j])
        y_mem = mem_attn(q3, q_col, mem_k[l], mem_v[l])
        x = mmr([('out_tok', l), ('out_mem', l)], [y_tok.reshape(m, D_RNN), y_mem.reshape(m, D_MEM)], x)
        act, fbuf, *wbs = ffn_up_act(x, P['g_ffn'][l], W[('ffn_u', l)], W[('ffn_g', l)], ffn_conv0[l],
                                     P['ffn_conv_w'][l], P['ffn_conv_b'][l], seq=t, t_valid=t_valid,
                                     emit_w=emit_w)
        if emit_w:
            emitted[('ffn_u', l)], emitted[('ffn_g', l)] = wbs
        x = mmr([('ffn_down', l)], [act], x, tn_pref=512)
        ffn_conv_out.append(fbuf)
        if l == N_A_LAYERS - 1:
            k_gain = jnp.tile(P['kv_k_norm'], SB_HEADS)
            hm = t if head_major_kv else None
            k_f32, k_bf = nmm(('k', l), x, P['kv_norm'], head_gain=k_gain, out_dtypes=(F32, BF16),
                              head_major_seq=hm)
            v_f32, v_bf = nmm(('v', l), x, P['kv_norm'], out_dtypes=(F32, BF16), head_major_seq=hm)
            if not head_major_kv:
                k_bf = k_bf.reshape(b, t, D_RNN)
                v_bf = v_bf.reshape(b, t, D_RNN)
    if head_major_kv:
        k_out, v_out = k_f32.transpose(0, 2, 1, 3), v_f32.transpose(0, 2, 1, 3)
    else:
        k_out, v_out = k_f32.reshape(b, t, SB_HEADS, HEAD), v_f32.reshape(b, t, SB_HEADS, HEAD)
    return (x.reshape(b, t, d), jnp.stack(rg_h_out), jnp.stack(rg_conv_out), jnp.stack(ffn_conv_out),
            k_out, v_out, emitted)


def kernel(x_prompt, x_sample, mem_prompt, state_rglru_h, state_rglru_conv, state_ffn_conv, cache_mem_k, cache_mem_v, cache_sb_k, cache_sb_v, page_table, g_mix, g_ffn, w_in_a, rg_conv_w, rg_conv_b, rg_gate_x_w, rg_gate_x_b, rg_gate_a_w, rg_gate_a_b, rg_lambda, w_in_b, sb_q_norm, sb_beta_bias, kv_norm, w_kv, kv_k_norm, mem_norm, w_mem_kv, mem_q_norm, mem_k_norm, w_out, w_ffn_up, ffn_conv_w, ffn_conv_b, w_ffn_down):
    P = {'g_mix': g_mix, 'g_ffn': g_ffn, 'rg_conv_w': rg_conv_w, 'rg_conv_b': rg_conv_b,
         'rg_gate_x_b': rg_gate_x_b, 'rg_gate_a_b': rg_gate_a_b, 'rg_lambda': rg_lambda,
         'sb_q_norm': sb_q_norm, 'sb_beta_bias': sb_beta_bias, 'kv_norm': kv_norm,
         'kv_k_norm': kv_k_norm, 'mem_q_norm': mem_q_norm,
         'ffn_conv_w': ffn_conv_w, 'ffn_conv_b': ffn_conv_b,
         'rg_gate_x_w': rg_gate_x_w, 'rg_gate_a_w': rg_gate_a_w}
    dm = D_MODEL
    W = {}
    for l in range(DEPTH):
        if l < N_A_LAYERS:
            W[('in_a_rg', l)] = WeightView(w_in_a, l, 0, dm, 0, 2 * D_RNN)
            W[('in_a_qm', l)] = WeightView(w_in_a, l, 0, dm, 2 * D_RNN, D_MEM)
        else:
            W[('in_b', l)] = WeightView(w_in_b, l - N_A_LAYERS, 0, dm, 0, D_RNN + D_MEM)
        W[('out_tok', l)] = WeightView(w_out, l, 0, D_RNN, 0, dm)
        W[('out_mem', l)] = WeightView(w_out, l, D_RNN, D_MEM, 0, dm)
        W[('ffn_u', l)] = WeightView(w_ffn_up, l, 0, dm, 0, D_FF)
        W[('ffn_g', l)] = WeightView(w_ffn_up, l, 0, dm, D_FF, D_FF)
        W[('ffn_down', l)] = WeightView(w_ffn_down, l, 0, D_FF, 0, dm)
    W[('k', N_A_LAYERS - 1)] = WeightView(w_kv, None, 0, dm, 0, D_RNN)
    W[('v', N_A_LAYERS - 1)] = WeightView(w_kv, None, 0, dm, D_RNN, D_RNN)

    db, dec_seq, _ = x_sample.shape
    xs = jnp.pad(x_sample, ((0, 0), (0, SAMPLE_T_PAD - dec_seq), (0, 0)))

    cache_k_hm = cache_sb_k.transpose(0, 2, 1, 3)
    cache_v_hm = cache_sb_v.transpose(0, 2, 1, 3)

    def sb_paged(q3, k_bf, v_bf, bias):
        return sb_sample(q3, k_bf, v_bf, cache_k_hm, cache_v_hm, page_table, bias, t_valid=dec_seq)

    (y_s, sample_rglru_h, sample_rglru_conv, sample_ffn_conv, s_k, s_v, w_bf16) = _trunk(
        xs, dec_seq, state_rglru_h, state_rglru_conv, state_ffn_conv,
        cache_mem_k.reshape(DEPTH, db, MEM_TOKENS, D_MEM), cache_mem_v.reshape(DEPTH, db, MEM_TOKENS, D_MEM),
        sb_paged, False, P, W, True)
    W_prompt = {key: _bf16_view(arr) for key, arr in w_bf16.items()}

    bp, seq, d = x_prompt.shape
    mem2 = mem_prompt.reshape(bp * MEM_TOKENS, d)
    mk_list, mv_list = [], []
    for l in range(DEPTH):
        mk_list.append(norm_matmul(mem2, mem_norm[l], WeightView(w_mem_kv, l, 0, dm, 0, D_MEM),
                                   head_gain=jnp.tile(mem_k_norm[l], MEM_HEADS), out_dtypes=(F32,)))
        mv_list.append(norm_matmul(mem2, mem_norm[l], WeightView(w_mem_kv, l, 0, dm, D_MEM, D_MEM),
                                   out_dtypes=(F32,)))
    prompt_mem_k = jnp.stack(mk_list).reshape(DEPTH, bp, MEM_TOKENS, D_MEM)
    prompt_mem_v = jnp.stack(mv_list).reshape(DEPTH, bp, MEM_TOKENS, D_MEM)
    zeros_h = jnp.zeros((N_A_LAYERS, bp, D_RNN), F32)
    zeros_rc = jnp.zeros((N_A_LAYERS, bp, RG_CONV - 1, D_RNN), F32)
    zeros_fc = jnp.zeros((DEPTH, bp, FFN_CONV - 1, D_FF), F32)
    (y_prompt, prompt_rglru_h, prompt_rglru_conv, prompt_ffn_conv, prompt_sb_k, prompt_sb_v, _) = _trunk(
        x_prompt, seq, zeros_h, zeros_rc, zeros_fc, prompt_mem_k, prompt_mem_v, sb_prompt, True, P, W_prompt,
        False)

    mem_shape = (DEPTH, bp, MEM_TOKENS, MEM_HEADS, HEAD)
    return (y_prompt, y_s[:, :dec_seq],
            prompt_rglru_h, prompt_rglru_conv, prompt_ffn_conv,
            prompt_sb_k, prompt_sb_v, prompt_mem_k.reshape(mem_shape), prompt_mem_v.reshape(mem_shape),
            sample_rglru_h, sample_rglru_conv, sample_ffn_conv,
            s_k[:, :dec_seq], s_v[:, :dec_seq])
```

```python
import functools
import math
from typing import NamedTuple

import jax
import jax.numpy as jnp
from jax import lax
from jax.experimental import pallas as pl
from jax.experimental.pallas import tpu as pltpu

F32 = jnp.float32
BF16 = jnp.bfloat16

LANE = 128
SUBLANE = 8
VMEM_LIMIT_BYTES = 56 * 1024 * 1024

D_MODEL = 2048
DEPTH = 4
N_A_LAYERS = 2
D_RNN = 1536
HEAD = 128
RG_HEADS = D_RNN // HEAD
RG_CONV = 4
RG_C = 8.0
SB_HEADS = D_RNN // HEAD
MEM_TOKENS = 256
MEM_HEADS = 4
D_MEM = MEM_HEADS * HEAD
D_FF = 3 * D_MODEL
FFN_CONV = 3
EPS = 1e-6
ATTN_SCALE = HEAD ** -0.5
LOG2E = math.log2(math.e)
SAMPLE_T_PAD = 16
Q_ROWS = 8
FFN_ROW_CHUNK = 256
SB_HEADS_PER_STEP = 3


def _params(*sem):
    return pltpu.CompilerParams(dimension_semantics=sem, vmem_limit_bytes=VMEM_LIMIT_BYTES)


def _tile(n, pref, mult):
    if n <= pref:
        return n
    t = (pref // mult) * mult
    while t > mult and n % t:
        t -= mult
    assert n % t == 0, (n, pref, mult)
    return t


class WeightView(NamedTuple):
    array: jax.Array
    layer: int | None
    row0: int
    rows: int
    col0: int
    cols: int


def _weight_spec(wv, tk, tn, row_block, col_block):
    assert wv.row0 % tk == 0 and wv.col0 % tn == 0 and wv.rows % tk == 0 and wv.cols % tn == 0
    r0, c0 = wv.row0 // tk, wv.col0 // tn
    if wv.layer is None:
        return pl.BlockSpec((tk, tn), lambda *g: (r0 + row_block(*g), c0 + col_block(*g)))
    layer = wv.layer
    return pl.BlockSpec((None, tk, tn), lambda *g: (layer, r0 + row_block(*g), c0 + col_block(*g)))


def _softplus(z):
    return jnp.maximum(z, 0.0) + jnp.log1p(jnp.exp(-jnp.abs(z)))


def _head_rmsnorm(blk, gain):
    ms = jnp.mean(blk * blk, axis=-1, keepdims=True)
    return blk * lax.rsqrt(ms + EPS) * gain


def _norm_matmul_kernel(*refs, head_norm, head_major, n_out, emit_w):
    x_ref, g_ref, w_ref = refs[:3]
    pos = 3
    hg_ref = None
    if head_norm:
        hg_ref = refs[pos]
        pos += 1
    o_refs = refs[pos:pos + n_out]
    pos += n_out
    wb_ref = None
    if emit_w:
        wb_ref = refs[pos]
        pos += 1
    xn_ref = refs[pos]

    @pl.when(pl.program_id(1) == 0)
    def _():
        x = x_ref[...]
        ms = jnp.mean(x * x, axis=-1, keepdims=True)
        xn_ref[...] = (x * lax.rsqrt(ms + EPS) * g_ref[...]).astype(BF16)

    wb = w_ref[...].astype(BF16)
    if emit_w:
        wb_ref[...] = wb
    acc = jnp.dot(xn_ref[...], wb, preferred_element_type=F32)
    tn = acc.shape[1]
    if head_norm or head_major:
        for h in range(tn // HEAD):
            cs = slice(h * HEAD, (h + 1) * HEAD)
            y = _head_rmsnorm(acc[:, cs], hg_ref[:, cs]) if head_norm else acc[:, cs]
            for o_ref in o_refs:
                if head_major:
                    o_ref[0, h] = y.astype(o_ref.dtype)
                else:
                    o_ref[:, cs] = y.astype(o_ref.dtype)
    else:
        for o_ref in o_refs:
            o_ref[...] = acc.astype(o_ref.dtype)


def _bf16_view(arr):
    return WeightView(arr, None, 0, arr.shape[0], 0, arr.shape[1])


def norm_matmul(x, g, w, *, head_gain=None, out_dtypes=(BF16,), head_major_seq=None, emit_w=False,
                tm_pref=1024, tn_pref=1024):
    m, d = x.shape
    n = w.cols
    assert w.rows == d
    tm = _tile(m, tm_pref, SUBLANE)
    tn = _tile(n, tn_pref, 2 * LANE)
    assert not emit_w or (m == tm and head_major_seq is None)
    if head_major_seq is None:
        out_specs = [pl.BlockSpec((tm, tn), lambda i, j: (i, j)) for _ in out_dtypes]
        out_shape = [jax.ShapeDtypeStruct((m, n), dt) for dt in out_dtypes]
    else:
        seq = head_major_seq
        assert seq % tm == 0 and m % seq == 0
        per_seq = seq // tm
        out_specs = [pl.BlockSpec((1, tn // HEAD, tm, HEAD), lambda i, j: (i // per_seq, j, i % per_seq, 0))
                     for _ in out_dtypes]
        out_shape = [jax.ShapeDtypeStruct((m // seq, n // HEAD, seq, HEAD), dt) for dt in out_dtypes]
    in_specs = [
        pl.BlockSpec((tm, d), lambda i, j: (i, 0)),
        pl.BlockSpec((1, d), lambda i, j: (0, 0)),
        _weight_spec(w, d, tn, lambda i, j: 0, lambda i, j: j),
    ]
    args = [x, g.reshape(1, d).astype(F32), w.array]
    if head_gain is not None:
        in_specs.append(pl.BlockSpec((1, tn), lambda i, j: (0, j)))
        args.append(head_gain.reshape(1, n).astype(F32))
    if emit_w:
        out_specs.append(pl.BlockSpec((d, tn), lambda i, j: (0, j)))
        out_shape.append(jax.ShapeDtypeStruct((d, n), BF16))
    outs = pl.pallas_call(
        functools.partial(_norm_matmul_kernel, head_norm=head_gain is not None,
                          head_major=head_major_seq is not None, n_out=len(out_dtypes), emit_w=emit_w),
        grid=(m // tm, n // tn),
        in_specs=in_specs,
        out_specs=out_specs,
        out_shape=out_shape,
        scratch_shapes=[pltpu.VMEM((tm, d), BF16)],
        compiler_params=_params("parallel", "arbitrary"),
        name="norm_matmul",
    )(*args)
    return outs[0] if len(outs) == 1 else tuple(outs)


def _matmul_residual_kernel(*refs, n_pairs, emit_w):
    x_ref = refs[2 * n_pairs]
    o_ref = refs[2 * n_pairs + 1]
    wb_refs = refs[2 * n_pairs + 2:]

    @pl.when(pl.program_id(2) == 0)
    def _():
        o_ref[...] = x_ref[...]

    acc = None
    for p in range(n_pairs):
        wb = refs[2 * p + 1][...].astype(BF16)
        if emit_w:
            wb_refs[p][...] = wb
        d = jnp.dot(refs[2 * p][...], wb, preferred_element_type=F32)
        acc = d if acc is None else acc + d
    o_ref[...] += acc


def matmul_residual(pairs, x, *, n_k=1, emit_w=False, tm_pref=1024, tn_pref=1024):
    m, n = x.shape
    tm = _tile(m, tm_pref, SUBLANE)
    tn = _tile(n, tn_pref, 2 * LANE)
    assert not emit_w or m == tm
    out_specs = [pl.BlockSpec((tm, tn), lambda i, j, k: (i, j))]
    out_shape = [jax.ShapeDtypeStruct((m, n), F32)]
    in_specs, args = [], []
    for a, w in pairs:
        kp = a.shape[1]
        assert kp % n_k == 0 and (w.rows, w.cols) == (kp, n)
        tk = kp // n_k
        in_specs.append(pl.BlockSpec((tm, tk), lambda i, j, k: (i, k)))
        in_specs.append(_weight_spec(w, tk, tn, lambda i, j, k: k, lambda i, j, k: j))
        args += [a, w.array]
        if emit_w:
            out_specs.append(pl.BlockSpec((tk, tn), lambda i, j, k: (k, j)))
            out_shape.append(jax.ShapeDtypeStruct((kp, n), BF16))
    in_specs.append(pl.BlockSpec((tm, tn), lambda i, j, k: (i, j)))
    args.append(x)
    outs = pl.pallas_call(
        functools.partial(_matmul_residual_kernel, n_pairs=len(pairs), emit_w=emit_w),
        grid=(m // tm, n // tn, n_k),
        in_specs=in_specs,
        out_specs=out_specs,
        out_shape=out_shape,
        compiler_params=_params("parallel", "parallel", "arbitrary"),
        name="matmul_residual",
    )(*args)
    return tuple(outs) if emit_w else outs[0]


def _rglru_kernel(p_ref, h0_ref, cb0_ref, cw_ref, cbias_ref, wgx_ref, bgx_ref, wga_ref, bga_ref, lam_ref,
                  y_ref, hl_ref, cbo_ref, ext_ref, hc_ref, *, tr, tile_last, r_last):
    t = pl.program_id(1)
    halo = SUBLANE

    @pl.when(t == 0)
    def _():
        ext_ref[0:halo, :] = jnp.zeros((halo, D_RNN), F32)
        ext_ref[halo - (RG_CONV - 1):halo, :] = cb0_ref[0]
        hc_ref[...] = h0_ref[0]

    ext_ref[halo:halo + tr, :] = p_ref[0, :, 0:D_RNN].astype(F32)
    sub_row = lax.broadcasted_iota(jnp.int32, (tr // SUBLANE, SUBLANE, HEAD), 1)
    for h in range(RG_HEADS):
        cs = slice(h * HEAD, (h + 1) * HEAD)
        xc = cbias_ref[:, cs] + ext_ref[halo - 3:halo - 3 + tr, cs] * cw_ref[0:1, cs]
        for j in range(1, RG_CONV):
            xc = xc + ext_ref[halo - 3 + j:halo - 3 + j + tr, cs] * cw_ref[j:j + 1, cs]
        xcb = xc.astype(BF16)
        gx = jax.nn.sigmoid(jnp.dot(xcb, wgx_ref[h].astype(BF16), preferred_element_type=F32) + bgx_ref[:, cs])
        ga = jax.nn.sigmoid(jnp.dot(xcb, wga_ref[h].astype(BF16), preferred_element_type=F32) + bga_ref[:, cs])
        log_a = (-RG_C) * ga * _softplus(-lam_ref[:, cs])
        a = jnp.exp(log_a)
        u = jnp.sqrt(-jnp.tanh(log_a) * (1.0 + a * a)) * (gx * xc)
        groups = tr // SUBLANE
        a3 = a.reshape(groups, SUBLANE, HEAD)
        u3 = u.reshape(groups, SUBLANE, HEAD)
        d = 1
        while d < SUBLANE:
            live = sub_row >= d
            u3 = jnp.where(live, a3 * pltpu.roll(u3, d, 1) + u3, u3)
            a3 = jnp.where(live, a3 * pltpu.roll(a3, d, 1), a3)
            d *= 2
        h_prev = jnp.broadcast_to(hc_ref[:, cs], (SUBLANE, HEAD))
        h_groups = []
        for k in range(groups):
            h_k = a3[k] * h_prev + u3[k]
            h_groups.append(h_k)
            h_prev = jnp.broadcast_to(h_k[SUBLANE - 1:SUBLANE, :], (SUBLANE, HEAD))
        u = jnp.concatenate(h_groups, axis=0)
        gate = jax.nn.gelu(p_ref[0, :, D_RNN + h * HEAD:D_RNN + (h + 1) * HEAD].astype(F32))
        y_ref[0, :, cs] = (u * gate).astype(y_ref.dtype)
        hc_ref[:, cs] = u[tr - 1:tr, :]

        @pl.when(t == tile_last)
        def _():
            hl_ref[0, :, cs] = u[r_last:r_last + 1, :]

    @pl.when(t == tile_last)
    def _():
        cbo_ref[0] = ext_ref[halo + r_last - 2:halo + r_last + 1, :]

    if tr >= halo:
        ext_ref[0:halo, :] = ext_ref[tr:tr + halo, :]


def rglru(p, h0, cb0, cw, cbias, wgx, bgx, wga, bga, lam, *, t_valid):
    b, t, _ = p.shape
    tr = _tile(t, 256, 16)
    row = lambda v: v.reshape(1, D_RNN).astype(F32)
    full = lambda shape: pl.BlockSpec(shape, lambda i, j: (0,) * len(shape))
    y, hl, cbo = pl.pallas_call(
        functools.partial(_rglru_kernel, tr=tr, tile_last=(t_valid - 1) // tr, r_last=(t_valid - 1) % tr),
        grid=(b, t // tr),
        in_specs=[
            pl.BlockSpec((1, tr, 2 * D_RNN), lambda i, j: (i, j, 0)),
            pl.BlockSpec((1, 1, D_RNN), lambda i, j: (i, 0, 0)),
            pl.BlockSpec((1, RG_CONV - 1, D_RNN), lambda i, j: (i, 0, 0)),
            full((RG_CONV, D_RNN)), full((1, D_RNN)),
            full((RG_HEADS, HEAD, HEAD)), full((1, D_RNN)),
            full((RG_HEADS, HEAD, HEAD)), full((1, D_RNN)),
            full((1, D_RNN)),
        ],
        out_specs=[
            pl.BlockSpec((1, tr, D_RNN), lambda i, j: (i, j, 0)),
            pl.BlockSpec((1, 1, D_RNN), lambda i, j: (i, 0, 0)),
            pl.BlockSpec((1, RG_CONV - 1, D_RNN), lambda i, j: (i, 0, 0)),
        ],
        out_shape=[
            jax.ShapeDtypeStruct((b, t, D_RNN), BF16),
            jax.ShapeDtypeStruct((b, 1, D_RNN), F32),
            jax.ShapeDtypeStruct((b, RG_CONV - 1, D_RNN), F32),
        ],
        scratch_shapes=[pltpu.VMEM((tr + SUBLANE, D_RNN), F32), pltpu.VMEM((1, D_RNN), F32)],
        compiler_params=_params("parallel", "arbitrary"),
        name="rglru",
    )(p, h0.reshape(b, 1, D_RNN), cb0, cw.astype(F32), row(cbias), wgx, row(bgx), wga, row(bga), row(lam))
    return y, hl.reshape(b, D_RNN), cbo


def _mem_attn_kernel(q_ref, k_ref, v_ref, o_ref):
    for h in range(MEM_HEADS):
        cs = slice(h * HEAD, (h + 1) * HEAD)
        q = q_ref[0, :, cs]
        k = k_ref[0, :, cs].astype(BF16)
        v = v_ref[0, :, cs].astype(BF16)
        s = lax.dot_general(q, k, (((1,), (1,)), ((), ())), preferred_element_type=F32)
        e = jnp.exp(s - jnp.max(s, axis=-1, keepdims=True))
        p = e / jnp.sum(e, axis=-1, keepdims=True)
        o_ref[0, :, cs] = jnp.dot(p.astype(BF16), v, preferred_element_type=F32).astype(o_ref.dtype)


def mem_attn(q, q_col_block, mk, mv):
    b, t, _ = q.shape
    tq = _tile(t, 512, 16)
    return pl.pallas_call(
        _mem_attn_kernel,
        grid=(b, t // tq),
        in_specs=[
            pl.BlockSpec((1, tq, D_MEM), lambda i, j: (i, j, q_col_block)),
            pl.BlockSpec((1, MEM_TOKENS, D_MEM), lambda i, j: (i, 0, 0)),
            pl.BlockSpec((1, MEM_TOKENS, D_MEM), lambda i, j: (i, 0, 0)),
        ],
        out_specs=pl.BlockSpec((1, tq, D_MEM), lambda i, j: (i, j, 0)),
        out_shape=jax.ShapeDtypeStruct((b, t, D_MEM), BF16),
        compiler_params=_params("parallel", "parallel"),
        name="mem_attn",
    )(q, mk, mv)


def _shift_rows(u, shift, head_rows):
    s = pltpu.roll(u, shift, 0)
    row = lax.broadcasted_iota(jnp.int32, (SUBLANE, u.shape[1]), 0)
    top = s[0:SUBLANE]
    for r, v in enumerate(head_rows):
        top = jnp.where(row == r, v, top)
    return jnp.concatenate([top, s[SUBLANE:]], axis=0)


def _conv_gelu_gate(u, gate, prev2, prev1, cw_ref, cb_ref):
    uc = (cb_ref[...] + _shift_rows(u, 2, [prev2, prev1]) * cw_ref[0:1, :]
          + _shift_rows(u, 1, [prev1]) * cw_ref[1:2, :] + u * cw_ref[2:3, :])
    return jax.nn.gelu(uc) * gate


def _ffn_up_act_kernel(x_ref, g_ref, wu_in_ref, wg_in_ref, b0_ref, cw_ref, cb_ref, act_ref, nb_ref, *rest,
                       seq, tiles_per_seq, t_valid, cast_w):
    i = pl.program_id(0)
    j = pl.program_id(1)
    tm = x_ref.shape[0]
    if cast_w:
        wu_ref, wg_ref, xn_ref, tail_ref = rest
        wu_ref[...] = wu_in_ref[...].astype(BF16)
        wg_ref[...] = wg_in_ref[...].astype(BF16)
    else:
        xn_ref, tail_ref = rest
        wu_ref, wg_ref = wu_in_ref, wg_in_ref

    @pl.when(j == 0)
    def _():
        x = x_ref[...]
        ms = jnp.mean(x * x, axis=-1, keepdims=True)
        xn_ref[...] = (x * lax.rsqrt(ms + EPS) * g_ref[...]).astype(BF16)

    if tiles_per_seq >= 1:
        @pl.when(i == 0)
        def _():
            tail_ref[j] = jnp.zeros(tail_ref.shape[1:], F32)

        first = (i % tiles_per_seq) == 0
        prev2 = jnp.where(first, b0_ref[0, 0:1, :], tail_ref[j, SUBLANE - 2:SUBLANE - 1, :])
        prev1 = jnp.where(first, b0_ref[0, 1:2, :], tail_ref[j, SUBLANE - 1:SUBLANE, :])
        rc = min(tm, FFN_ROW_CHUNK)
        for c in range(tm // rc):
            xs = xn_ref[c * rc:(c + 1) * rc, :]
            u = jnp.dot(xs, wu_ref[...], preferred_element_type=F32)
            gate = jnp.dot(xs, wg_ref[...], preferred_element_type=F32)
            act_ref[c * rc:(c + 1) * rc, :] = _conv_gelu_gate(u, gate, prev2, prev1, cw_ref, cb_ref
                                                              ).astype(act_ref.dtype)
            prev2, prev1 = u[rc - 2:rc - 1], u[rc - 1:rc]
        tail_ref[j] = u[rc - SUBLANE:rc]

        @pl.when((i % tiles_per_seq) == tiles_per_seq - 1)
        def _():
            nb_ref[i // tiles_per_seq, j] = u[rc - 2:rc]
    else:
        xn = xn_ref[...]
        u = jnp.dot(xn, wu_ref[...], preferred_element_type=F32)
        gate = jnp.dot(xn, wg_ref[...], preferred_element_type=F32)
        for b in range(tm // seq):
            rs = slice(b * seq, (b + 1) * seq)
            a = _conv_gelu_gate(u[rs], gate[rs], b0_ref[b, 0:1, :], b0_ref[b, 1:2, :], cw_ref, cb_ref)
            act_ref[rs, :] = a.astype(act_ref.dtype)
            nb_ref[(tm // seq) * i + b, j] = u[b * seq + t_valid - 2:b * seq + t_valid]


def ffn_up_act(x, g, w_u, w_g, buf0, cw, cb, *, seq, t_valid, emit_w=False, tm_pref=1024):
    m, d = x.shape
    assert (w_u.rows, w_u.cols) == (d, D_FF) and (w_g.rows, w_g.cols) == (d, D_FF)
    b = m // seq
    tm = _tile(m, tm_pref, SUBLANE)
    assert t_valid >= FFN_CONV - 1 and (seq % tm == 0 or tm % seq == 0)
    tiles_per_seq = seq // tm if tm <= seq else 0
    assert tiles_per_seq == 0 or t_valid == seq
    seqs_per_tile = max(tm // seq, 1)
    cast_w = w_u.array.dtype != BF16
    tn = 512 if cast_w else 1024
    n_j = D_FF // tn
    assert not emit_w or (cast_w and m == tm)
    out_specs = [
        pl.BlockSpec((tm, tn), lambda i, j: (i, j)),
        pl.BlockSpec((b, n_j, FFN_CONV - 1, tn), lambda i, j: (0, 0, 0, 0)),
    ]
    out_shape = [
        jax.ShapeDtypeStruct((m, D_FF), BF16),
        jax.ShapeDtypeStruct((b, n_j, FFN_CONV - 1, tn), F32),
    ]
    scratch = [pltpu.VMEM((tm, d), BF16), pltpu.VMEM((n_j, SUBLANE, tn), F32)]
    if emit_w:
        out_specs += [pl.BlockSpec((d, tn), lambda i, j: (0, j)), pl.BlockSpec((d, tn), lambda i, j: (0, j))]
        out_shape += [jax.ShapeDtypeStruct((d, D_FF), BF16), jax.ShapeDtypeStruct((d, D_FF), BF16)]
    elif cast_w:
        scratch = [pltpu.VMEM((d, tn), BF16), pltpu.VMEM((d, tn), BF16)] + scratch
    outs = pl.pallas_call(
        functools.partial(_ffn_up_act_kernel, seq=seq, tiles_per_seq=tiles_per_seq, t_valid=t_valid,
                          cast_w=cast_w),
        grid=(m // tm, n_j),
        in_specs=[
            pl.BlockSpec((tm, d), lambda i, j: (i, 0)),
            pl.BlockSpec((1, d), lambda i, j: (0, 0)),
            _weight_spec(w_u, d, tn, lambda i, j: 0, lambda i, j: j),
            _weight_spec(w_g, d, tn, lambda i, j: 0, lambda i, j: j),
            pl.BlockSpec((seqs_per_tile, FFN_CONV - 1, tn),
                         lambda i, j: ((i // tiles_per_seq) if tiles_per_seq else i, 0, j)),
            pl.BlockSpec((FFN_CONV, tn), lambda i, j: (0, j)),
            pl.BlockSpec((1, tn), lambda i, j: (0, j)),
        ],
        out_specs=out_specs,
        out_shape=out_shape,
        scratch_shapes=scratch,
        compiler_params=_params("arbitrary", "arbitrary"),
        name="ffn_up_act",
    )(x, g.reshape(1, d).astype(F32), w_u.array, w_g.array, buf0, cw.astype(F32),
      cb.reshape(1, D_FF).astype(F32))
    nb = outs[1].transpose(0, 2, 1, 3).reshape(b, FFN_CONV - 1, D_FF)
    return (outs[0], nb) + tuple(outs[2:])


def _suffix_matrix():
    j = lax.broadcasted_iota(jnp.int32, (2 * HEAD, 2 * HEAD), 0) % HEAD
    s = lax.broadcasted_iota(jnp.int32, (2 * HEAD, 2 * HEAD), 1)
    return jnp.where((s >= HEAD) | (j >= s), 1.0, 0.0).astype(BF16)


def _sb_weights(s, neg_bias, carry, suffix, mask):
    nz, lk = _sb_log_keep(s, neg_bias, mask)
    return _sb_finish(nz, _sb_suffix_sums(lk, suffix), carry, mask)


def _sb_log_keep(s, neg_bias, mask):
    nz = neg_bias - s
    neg_abs = pltpu.bitcast(pltpu.bitcast(nz, jnp.uint32) | jnp.uint32(0x80000000), F32)
    lk = jnp.minimum(nz, 0.0) - jnp.log2(1.0 + jnp.exp2(neg_abs))
    if mask is not None:
        lk = jnp.where(mask, lk, 0.0)
    return nz, lk


def _sb_suffix_sums(lk, suffix):
    hi = lk.astype(BF16)
    lo = (lk - hi.astype(F32)).astype(BF16)
    return jnp.dot(jnp.concatenate([hi, lo], axis=1), suffix, preferred_element_type=F32)


def _sb_finish(nz, c2, carry, mask):
    w = jnp.exp2((carry + c2[:, :HEAD]) - nz)
    if mask is not None:
        w = jnp.where(mask, w, 0.0)
    return w, carry + c2[:, HEAD:]


def _nt_dot(a, b):
    return lax.dot_general(a, b, (((1,), (1,)), ((), ())), preferred_element_type=F32)


def _sb_prompt_kernel(nbias_ref, q_ref, k_ref, v_ref, o_ref, carry_ref, acc_ref, *, tq, hp):
    hg = pl.program_id(1)
    qi = pl.program_id(2)
    suffix = _suffix_matrix()
    n_sub = tq // HEAD
    carry_ref[...] = jnp.zeros(carry_ref.shape, F32)
    acc_ref[...] = jnp.zeros(acc_ref.shape, F32)

    def block(start, masked):
        for e in range(hp):
            neg_bias = nbias_ref[hg * hp + e]
            s = _nt_dot(q_ref[0, :, e * HEAD:(e + 1) * HEAD], k_ref[0, e, pl.ds(start, tq), :])
            carry = carry_ref[e]
            ws = [None] * n_sub
            for j in range(n_sub - 1, -1, -1):
                if masked:
                    r0 = j * HEAD
                    mask = (lax.broadcasted_iota(jnp.int32, (tq - r0, HEAD), 1)
                            < lax.broadcasted_iota(jnp.int32, (tq - r0, HEAD), 0))
                    w, c = _sb_weights(s[r0:, j * HEAD:(j + 1) * HEAD], neg_bias, carry[r0:], suffix, mask)
                    if r0:
                        w = jnp.concatenate([jnp.zeros((r0, HEAD), F32), w], axis=0)
                        c = jnp.concatenate([carry[:r0], c], axis=0)
                    carry = c
                else:
                    w, carry = _sb_weights(s[:, j * HEAD:(j + 1) * HEAD], neg_bias, carry, suffix, None)
                ws[j] = w.astype(BF16)
            carry_ref[e] = carry
            acc_ref[e] += jnp.dot(jnp.concatenate(ws, axis=1), v_ref[0, e, pl.ds(start, tq), :],
                                  preferred_element_type=F32)

    block(pl.multiple_of(qi * tq, tq), True)

    @pl.loop(0, qi)
    def _(i):
        block(pl.multiple_of((qi - 1 - i) * tq, tq), False)

    for e in range(hp):
        o_ref[0, :, e * HEAD:(e + 1) * HEAD] = acc_ref[e].astype(o_ref.dtype)


def sb_prompt(q, k, v, bias):
    b, _, t, _ = k.shape
    tq = _tile(t, 512, HEAD)
    hp = SB_HEADS_PER_STEP
    return pl.pallas_call(
        functools.partial(_sb_prompt_kernel, tq=tq, hp=hp),
        grid_spec=pltpu.PrefetchScalarGridSpec(
            num_scalar_prefetch=1,
            grid=(b, SB_HEADS // hp, t // tq),
            in_specs=[
                pl.BlockSpec((1, tq, hp * HEAD), lambda i, h, j, nb: (i, j, h)),
                pl.BlockSpec((1, hp, t, HEAD), lambda i, h, j, nb: (i, h, 0, 0)),
                pl.BlockSpec((1, hp, t, HEAD), lambda i, h, j, nb: (i, h, 0, 0)),
            ],
            out_specs=pl.BlockSpec((1, tq, hp * HEAD), lambda i, h, j, nb: (i, j, h)),
            scratch_shapes=[pltpu.VMEM((hp, tq, HEAD), F32), pltpu.VMEM((hp, tq, HEAD), F32)],
        ),
        out_shape=jax.ShapeDtypeStruct((b, t, D_RNN), BF16),
        compiler_params=_params("parallel", "parallel", "arbitrary"),
        name="sb_prompt",
    )(-LOG2E * bias.astype(F32), q, k, v)


def _sb_sample_kernel(pt_ref, q_ref, kn_ref, vn_ref, bias_ref, *rest, pages_per_step, t_valid):
    kp_refs = rest[:pages_per_step]
    vp_refs = rest[pages_per_step:2 * pages_per_step]
    o_ref, carry_ref, acc_ref = rest[2 * pages_per_step:]
    g = pl.program_id(1)
    rows = SB_HEADS * Q_ROWS
    suffix = _suffix_matrix()
    neg_bias = bias_ref[...]
    hcols = [slice(h * HEAD, (h + 1) * HEAD) for h in range(SB_HEADS)]
    q_heads = [q_ref[0, :, hcols[h]] for h in range(SB_HEADS)]

    def update(blocks, mask):
        n = len(blocks)
        s_heads = [_nt_dot(q_heads[h], jnp.concatenate([blocks[r][0][h] for r in range(n)], axis=0))[0:Q_ROWS]
                   for h in range(SB_HEADS)]
        nzs, lks = [], []
        for r in range(n):
            s = jnp.concatenate([sh[:, r * HEAD:(r + 1) * HEAD] for sh in s_heads], axis=0)
            nz, lk = _sb_log_keep(s, neg_bias, mask)
            nzs.append(nz)
            lks.append(lk)
        c2 = _sb_suffix_sums(jnp.concatenate(lks, axis=0), suffix)
        carry = carry_ref[...]
        ws = []
        for r in range(n):
            w, carry = _sb_finish(nzs[r], c2[r * rows:(r + 1) * rows], carry, mask)
            ws.append(w)
        carry_ref[...] = carry
        zpad = jnp.zeros((SAMPLE_T_PAD - Q_ROWS, n * HEAD), F32)
        for h in range(SB_HEADS):
            rs = slice(h * Q_ROWS, (h + 1) * Q_ROWS)
            wh = jnp.concatenate([jnp.concatenate([ws[r][rs] for r in range(n)], axis=1), zpad], axis=0)
            vh = jnp.concatenate([blocks[r][1][h] for r in range(n)], axis=0)
            acc_ref[rs, :] += jnp.dot(wh.astype(BF16), vh, preferred_element_type=F32)[0:Q_ROWS]

    @pl.when(g == 0)
    def _():
        carry_ref[...] = jnp.zeros((rows, HEAD), F32)
        acc_ref[...] = jnp.zeros((rows, HEAD), F32)
        kz = jnp.zeros((HEAD - SAMPLE_T_PAD, HEAD), BF16)
        k_heads = [jnp.concatenate([kn_ref[0, :, hcols[h]], kz], axis=0) for h in range(SB_HEADS)]
        v_heads = [jnp.concatenate([vn_ref[0, :, hcols[h]], kz], axis=0) for h in range(SB_HEADS)]
        i_q = lax.broadcasted_iota(jnp.int32, (rows, HEAD), 0) % Q_ROWS
        j_k = lax.broadcasted_iota(jnp.int32, (rows, HEAD), 1)
        update([(k_heads, v_heads)], (j_k < i_q) & (j_k < t_valid))

    update([([kp_refs[r][0, h].astype(BF16) for h in range(SB_HEADS)],
             [vp_refs[r][0, h].astype(BF16) for h in range(SB_HEADS)]) for r in range(pages_per_step)], None)

    @pl.when(g == pl.num_programs(1) - 1)
    def _():
        z8 = jnp.zeros((SAMPLE_T_PAD - Q_ROWS, HEAD), F32)
        for h in range(SB_HEADS):
            blk = jnp.concatenate([acc_ref[h * Q_ROWS:(h + 1) * Q_ROWS, :], z8], axis=0)
            o_ref[0, :, h * HEAD:(h + 1) * HEAD] = blk.astype(o_ref.dtype)


def sb_sample(q, k_new, v_new, cache_k, cache_v, page_table, bias, *, t_valid, pages_per_step=8):
    b, n_pages = page_table.shape
    page = cache_k.shape[2]
    assert page == HEAD and n_pages % pages_per_step == 0 and t_valid <= Q_ROWS
    assert cache_k.shape[1:] == (SB_HEADS, page, HEAD)
    rows = SB_HEADS * Q_ROWS
    bias_rows = jnp.broadcast_to(jnp.repeat(-LOG2E * bias.astype(F32), Q_ROWS)[:, None], (rows, HEAD))

    def page_spec(r):
        return pl.BlockSpec(
            (1, SB_HEADS, page, HEAD),
            lambda i, g, pt: (pt[i, n_pages - 1 - (g * pages_per_step + r)], 0, 0, 0))

    return pl.pallas_call(
        functools.partial(_sb_sample_kernel, pages_per_step=pages_per_step, t_valid=t_valid),
        grid_spec=pltpu.PrefetchScalarGridSpec(
            num_scalar_prefetch=1,
            grid=(b, n_pages // pages_per_step),
            in_specs=[
                pl.BlockSpec((1, SAMPLE_T_PAD, D_RNN), lambda i, g, pt: (i, 0, 0)),
                pl.BlockSpec((1, SAMPLE_T_PAD, D_RNN), lambda i, g, pt: (i, 0, 0)),
                pl.BlockSpec((1, SAMPLE_T_PAD, D_RNN), lambda i, g, pt: (i, 0, 0)),
                pl.BlockSpec((rows, HEAD), lambda i, g, pt: (0, 0)),
            ] + [page_spec(r) for r in range(pages_per_step)] + [page_spec(r) for r in range(pages_per_step)],
            out_specs=pl.BlockSpec((1, SAMPLE_T_PAD, D_RNN), lambda i, g, pt: (i, 0, 0)),
            scratch_shapes=[pltpu.VMEM((rows, HEAD), F32), pltpu.VMEM((rows, HEAD), F32)],
        ),
        out_shape=jax.ShapeDtypeStruct((b, SAMPLE_T_PAD, D_RNN), BF16),
        compiler_params=_params("parallel", "arbitrary"),
        name="sb_sample",
    )(page_table, q, k_new, v_new, bias_rows, *([cache_k] * pages_per_step), *([cache_v] * pages_per_step))


def _trunk(x3, t_valid, rg_h0, rg_conv0, ffn_conv0, mem_k, mem_v, sb_attend, head_major_kv, P, W, emit_w):
    b, t, d = x3.shape
    m = b * t
    x = x3.reshape(m, d)
    rg_h_out, rg_conv_out, ffn_conv_out = [], [], []
    k_f32 = v_f32 = k_bf = v_bf = None
    emitted = {}

    def nmm(key, x, g, **kw):
        out = norm_matmul(x, g, W[key], emit_w=emit_w, **kw)
        if not emit_w:
            return out
        *out, emitted[key] = out
        return out[0] if len(out) == 1 else tuple(out)

    def mmr(keys, acts, x, **kw):
        out = matmul_residual([(a, W[k]) for a, k in zip(acts, keys)], x, emit_w=emit_w, **kw)
        if not emit_w:
            return out
        for k, wb in zip(keys, out[1:]):
            emitted[k] = wb
        return out[0]

    for l in range(DEPTH):
        mem_gain = jnp.tile(P['mem_q_norm'][l] * ATTN_SCALE, MEM_HEADS)
        if l < N_A_LAYERS:
            p_rg = nmm(('in_a_rg', l), x, P['g_mix'][l])
            qm = nmm(('in_a_qm', l), x, P['g_mix'][l], head_gain=mem_gain)
            y_tok, h_last, cbuf = rglru(
                p_rg.reshape(b, t, 2 * D_RNN), rg_h0[l], rg_conv0[l], P['rg_conv_w'][l], P['rg_conv_b'][l],
                P['rg_gate_x_w'][l], P['rg_gate_x_b'][l], P['rg_gate_a_w'][l], P['rg_gate_a_b'][l],
                P['rg_lambda'][l], t_valid=t_valid)
            rg_h_out.append(h_last)
            rg_conv_out.append(cbuf)
            q3, q_col = qm.reshape(b, t, D_MEM), 0
        else:
            j = l - N_A_LAYERS
            gain = jnp.concatenate([jnp.tile(P['sb_q_norm'][j] * (ATTN_SCALE * LOG2E), SB_HEADS), mem_gain])
            pq = nmm(('in_b', l), x, P['g_mix'][l], head_gain=gain)
            q3, q_col = pq.reshape(b, t, D_RNN + D_MEM), D_RNN // D_MEM
            y_tok = sb_attend(q3, k_bf, v_bf, P['sb_beta_bias'][j])
        y_mem = mem_attn(q3, q_col, mem_k[l], mem_v[l])
        x = mmr([('out_tok', l), ('out_mem', l)], [y_tok.reshape(m, D_RNN), y_mem.reshape(m, D_MEM)], x)
        act, fbuf, *wbs = ffn_up_act(x, P['g_ffn'][l], W[('ffn_u', l)], W[('ffn_g', l)], ffn_conv0[l],
                                     P['ffn_conv_w'][l], P['ffn_conv_b'][l], seq=t, t_valid=t_valid,
                                     emit_w=emit_w)
        if emit_w:
            emitted[('ffn_u', l)], emitted[('ffn_g', l)] = wbs
        x = mmr([('ffn_down', l)], [act], x, tn_pref=512)
        ffn_conv_out.append(fbuf)
        if l == N_A_LAYERS - 1:
            k_gain = jnp.tile(P['kv_k_norm'], SB_HEADS)
            hm = t if head_major_kv else None
            k_f32, k_bf = nmm(('k', l), x, P['kv_norm'], head_gain=k_gain, out_dtypes=(F32, BF16),
                              head_major_seq=hm)
            v_f32, v_bf = nmm(('v', l), x, P['kv_norm'], out_dtypes=(F32, BF16), head_major_seq=hm)
            if not head_major_kv:
                k_bf = k_bf.reshape(b, t, D_RNN)
                v_bf = v_bf.reshape(b, t, D_RNN)
    if head_major_kv:
        k_out, v_out = k_f32.transpose(0, 2, 1, 3), v_f32.transpose(0, 2, 1, 3)
    else:
        k_out, v_out = k_f32.reshape(b, t, SB_HEADS, HEAD), v_f32.reshape(b, t, SB_HEADS, HEAD)
    return (x.reshape(b, t, d), jnp.stack(rg_h_out), jnp.stack(rg_conv_out), jnp.stack(ffn_conv_out),
            k_out, v_out, emitted)


def kernel(x_prompt, x_sample, mem_prompt, state_rglru_h, state_rglru_conv, state_ffn_conv, cache_mem_k, cache_mem_v, cache_sb_k, cache_sb_v, page_table, g_mix, g_ffn, w_in_a, rg_conv_w, rg_conv_b, rg_gate_x_w, rg_gate_x_b, rg_gate_a_w, rg_gate_a_b, rg_lambda, w_in_b, sb_q_norm, sb_beta_bias, kv_norm, w_kv, kv_k_norm, mem_norm, w_mem_kv, mem_q_norm, mem_k_norm, w_out, w_ffn_up, ffn_conv_w, ffn_conv_b, w_ffn_down):
    P = {'g_mix': g_mix, 'g_ffn': g_ffn, 'rg_conv_w': rg_conv_w, 'rg_conv_b': rg_conv_b,
         'rg_gate_x_b': rg_gate_x_b, 'rg_gate_a_b': rg_gate_a_b, 'rg_lambda': rg_lambda,
         'sb_q_norm': sb_q_norm, 'sb_beta_bias': sb_beta_bias, 'kv_norm': kv_norm,
         'kv_k_norm': kv_k_norm, 'mem_q_norm': mem_q_norm,
         'ffn_conv_w': ffn_conv_w, 'ffn_conv_b': ffn_conv_b,
         'rg_gate_x_w': rg_gate_x_w, 'rg_gate_a_w': rg_gate_a_w}
    dm = D_MODEL
    W = {}
    for l in range(DEPTH):
        if l < N_A_LAYERS:
            W[('in_a_rg', l)] = WeightView(w_in_a, l, 0, dm, 0, 2 * D_RNN)
            W[('in_a_qm', l)] = WeightView(w_in_a, l, 0, dm, 2 * D_RNN, D_MEM)
        else:
            W[('in_b', l)] = WeightView(w_in_b, l - N_A_LAYERS, 0, dm, 0, D_RNN + D_MEM)
        W[('out_tok', l)] = WeightView(w_out, l, 0, D_RNN, 0, dm)
        W[('out_mem', l)] = WeightView(w_out, l, D_RNN, D_MEM, 0, dm)
        W[('ffn_u', l)] = WeightView(w_ffn_up, l, 0, dm, 0, D_FF)
        W[('ffn_g', l)] = WeightView(w_ffn_up, l, 0, dm, D_FF, D_FF)
        W[('ffn_down', l)] = WeightView(w_ffn_down, l, 0, D_FF, 0, dm)
    W[('k', N_A_LAYERS - 1)] = WeightView(w_kv, None, 0, dm, 0, D_RNN)
    W[('v', N_A_LAYERS - 1)] = WeightView(w_kv, None, 0, dm, D_RNN, D_RNN)

    db, dec_seq, _ = x_sample.shape
    xs = jnp.pad(x_sample, ((0, 0), (0, SAMPLE_T_PAD - dec_seq), (0, 0)))

    cache_k_hm = cache_sb_k.transpose(0, 2, 1, 3)
    cache_v_hm = cache_sb_v.transpose(0, 2, 1, 3)

    def sb_paged(q3, k_bf, v_bf, bias):
        return sb_sample(q3, k_bf, v_bf, cache_k_hm, cache_v_hm, page_table, bias, t_valid=dec_seq)

    (y_s, sample_rglru_h, sample_rglru_conv, sample_ffn_conv, s_k, s_v, w_bf16) = _trunk(
        xs, dec_seq, state_rglru_h, state_rglru_conv, state_ffn_conv,
        cache_mem_k.reshape(DEPTH, db, MEM_TOKENS, D_MEM), cache_mem_v.reshape(DEPTH, db, MEM_TOKENS, D_MEM),
        sb_paged, False, P, W, True)
    W_prompt = {key: _bf16_view(arr) for key, arr in w_bf16.items()}

    bp, seq, d = x_prompt.shape
    mem2 = mem_prompt.reshape(bp * MEM_TOKENS, d)
    mk_list, mv_list = [], []
    for l in range(DEPTH):
        mk_list.append(norm_matmul(mem2, mem_norm[l], WeightView(w_mem_kv, l, 0, dm, 0, D_MEM),
                                   head_gain=jnp.tile(mem_k_norm[l], MEM_HEADS), out_dtypes=(F32,)))
        mv_list.append(norm_matmul(mem2, mem_norm[l], WeightView(w_mem_kv, l, 0, dm, D_MEM, D_MEM),
                                   out_dtypes=(F32,)))
    prompt_mem_k = jnp.stack(mk_list).reshape(DEPTH, bp, MEM_TOKENS, D_MEM)
    prompt_mem_v = jnp.stack(mv_list).reshape(DEPTH, bp, MEM_TOKENS, D_MEM)
    zeros_h = jnp.zeros((N_A_LAYERS, bp, D_RNN), F32)
    zeros_rc = jnp.zeros((N_A_LAYERS, bp, RG_CONV - 1, D_RNN), F32)
    zeros_fc = jnp.zeros((DEPTH, bp, FFN_CONV - 1, D_FF), F32)
    (y_prompt, prompt_rglru_h, prompt_rglru_conv, prompt_ffn_conv, prompt_sb_k, prompt_sb_v, _) = _trunk(
        x_prompt, seq, zeros_h, zeros_rc, zeros_fc, prompt_mem_k, prompt_mem_v, sb_prompt, True, P, W_prompt,
        False)

    mem_shape = (DEPTH, bp, MEM_TOKENS, MEM_HEADS, HEAD)
    return (y_prompt, y_s[:, :dec_seq],
            prompt_rglru_h, prompt_rglru_conv, prompt_ffn_conv,
            prompt_sb_k, prompt_sb_v, prompt_mem_k.reshape(mem_shape), prompt_mem_v.reshape(mem_shape),
            sample_rglru_h, sample_rglru_conv, sample_ffn_conv,
            s_k[:, :dec_seq], s_v[:, :dec_seq])
```

```python
import functools
import math
from typing import NamedTuple

import jax
import jax.numpy as jnp
from jax import lax
from jax.experimental import pallas as pl
from jax.experimental.pallas import tpu as pltpu

F32 = jnp.float32
BF16 = jnp.bfloat16

LANE = 128
SUBLANE = 8
VMEM_LIMIT_BYTES = 56 * 1024 * 1024

D_MODEL = 2048
DEPTH = 4
N_A_LAYERS = 2
D_RNN = 1536
HEAD = 128
RG_HEADS = D_RNN // HEAD
RG_CONV = 4
RG_C = 8.0
SB_HEADS = D_RNN // HEAD
MEM_TOKENS = 256
MEM_HEADS = 4
D_MEM = MEM_HEADS * HEAD
D_FF = 3 * D_MODEL
FFN_CONV = 3
EPS = 1e-6
ATTN_SCALE = HEAD ** -0.5
LOG2E = math.log2(math.e)
SAMPLE_T_PAD = 16
Q_ROWS = 8
FFN_ROW_CHUNK = 256
NORM_ROW_CHUNK = 256
SB_HEADS_PER_STEP = 3


def _params(*sem):
    return pltpu.CompilerParams(dimension_semantics=sem, vmem_limit_bytes=VMEM_LIMIT_BYTES)


def _tile(n, pref, mult):
    if n <= pref:
        return n
    t = (pref // mult) * mult
    while t > mult and n % t:
        t -= mult
    assert n % t == 0, (n, pref, mult)
    return t


class WeightView(NamedTuple):
    array: jax.Array
    layer: int | None
    row0: int
    rows: int
    col0: int
    cols: int


def _weight_spec(wv, tk, tn, row_block, col_block):
    assert wv.row0 % tk == 0 and wv.col0 % tn == 0 and wv.rows % tk == 0 and wv.cols % tn == 0
    r0, c0 = wv.row0 // tk, wv.col0 // tn
    if wv.layer is None:
        return pl.BlockSpec((tk, tn), lambda *g: (r0 + row_block(*g), c0 + col_block(*g)))
    layer = wv.layer
    return pl.BlockSpec((None, tk, tn), lambda *g: (layer, r0 + row_block(*g), c0 + col_block(*g)))


def _softplus(z):
    return jnp.maximum(z, 0.0) + jnp.log1p(jnp.exp(-jnp.abs(z)))


def _head_rmsnorm(blk, gain):
    ms = jnp.mean(blk * blk, axis=-1, keepdims=True)
    return blk * lax.rsqrt(ms + EPS) * gain


def _norm_matmul_kernel(*refs, head_norm, head_major, n_out, emit_w):
    x_ref, g_ref, w_ref = refs[:3]
    pos = 3
    hg_ref = None
    if head_norm:
        hg_ref = refs[pos]
        pos += 1
    o_refs = refs[pos:pos + n_out]
    pos += n_out
    wb_ref = None
    if emit_w:
        wb_ref = refs[pos]
        pos += 1
    xn_ref = refs[pos]

    @pl.when(pl.program_id(1) == 0)
    def _():
        x = x_ref[...]
        ms = jnp.mean(x * x, axis=-1, keepdims=True)
        xn_ref[...] = (x * lax.rsqrt(ms + EPS) * g_ref[...]).astype(BF16)

    wb = w_ref[...].astype(BF16)
    if emit_w:
        wb_ref[...] = wb
    tm = xn_ref.shape[0]
    tn = wb.shape[1]
    if head_norm or head_major:
        rc = min(tm, NORM_ROW_CHUNK)
        for c in range(tm // rc):
            rows = slice(c * rc, (c + 1) * rc)
            acc = jnp.dot(xn_ref[rows, :], wb, preferred_element_type=F32)
            for h in range(tn // HEAD):
                cs = slice(h * HEAD, (h + 1) * HEAD)
                y = _head_rmsnorm(acc[:, cs], hg_ref[:, cs]) if head_norm else acc[:, cs]
                for o_ref in o_refs:
                    if head_major:
                        o_ref[0, h, rows, :] = y.astype(o_ref.dtype)
                    else:
                        o_ref[rows, cs] = y.astype(o_ref.dtype)
    else:
        acc = jnp.dot(xn_ref[...], wb, preferred_element_type=F32)
        for o_ref in o_refs:
            o_ref[...] = acc.astype(o_ref.dtype)


def _bf16_view(arr):
    return WeightView(arr, None, 0, arr.shape[0], 0, arr.shape[1])


def norm_matmul(x, g, w, *, head_gain=None, out_dtypes=(BF16,), head_major_seq=None, emit_w=False,
                tm_pref=1024, tn_pref=1024):
    m, d = x.shape
    n = w.cols
    assert w.rows == d
    tm = _tile(m, tm_pref, SUBLANE)
    tn = _tile(n, tn_pref, 2 * LANE)
    assert not emit_w or (m == tm and head_major_seq is None)
    if head_major_seq is None:
        out_specs = [pl.BlockSpec((tm, tn), lambda i, j: (i, j)) for _ in out_dtypes]
        out_shape = [jax.ShapeDtypeStruct((m, n), dt) for dt in out_dtypes]
    else:
        seq = head_major_seq
        assert seq % tm == 0 and m % seq == 0
        per_seq = seq // tm
        out_specs = [pl.BlockSpec((1, tn // HEAD, tm, HEAD), lambda i, j: (i // per_seq, j, i % per_seq, 0))
                     for _ in out_dtypes]
        out_shape = [jax.ShapeDtypeStruct((m // seq, n // HEAD, seq, HEAD), dt) for dt in out_dtypes]
    in_specs = [
        pl.BlockSpec((tm, d), lambda i, j: (i, 0)),
        pl.BlockSpec((1, d), lambda i, j: (0, 0)),
        _weight_spec(w, d, tn, lambda i, j: 0, lambda i, j: j),
    ]
    args = [x, g.reshape(1, d).astype(F32), w.array]
    if head_gain is not None:
        in_specs.append(pl.BlockSpec((1, tn), lambda i, j: (0, j)))
        args.append(head_gain.reshape(1, n).astype(F32))
    if emit_w:
        out_specs.append(pl.BlockSpec((d, tn), lambda i, j: (0, j)))
        out_shape.append(jax.ShapeDtypeStruct((d, n), BF16))
    outs = pl.pallas_call(
        functools.partial(_norm_matmul_kernel, head_norm=head_gain is not None,
                          head_major=head_major_seq is not None, n_out=len(out_dtypes), emit_w=emit_w),
        grid=(m // tm, n // tn),
        in_specs=in_specs,
        out_specs=out_specs,
        out_shape=out_shape,
        scratch_shapes=[pltpu.VMEM((tm, d), BF16)],
        compiler_params=_params("parallel", "arbitrary"),
        name="norm_matmul",
    )(*args)
    return outs[0] if len(outs) == 1 else tuple(outs)


def _matmul_residual_kernel(*refs, n_pairs, emit_w):
    x_ref = refs[2 * n_pairs]
    o_ref = refs[2 * n_pairs + 1]
    wb_refs = refs[2 * n_pairs + 2:]

    @pl.when(pl.program_id(2) == 0)
    def _():
        o_ref[...] = x_ref[...]

    acc = None
    for p in range(n_pairs):
        wb = refs[2 * p + 1][...].astype(BF16)
        if emit_w:
            wb_refs[p][...] = wb
        d = jnp.dot(refs[2 * p][...], wb, preferred_element_type=F32)
        acc = d if acc is None else acc + d
    o_ref[...] += acc


def matmul_residual(pairs, x, *, n_k=1, emit_w=False, tm_pref=1024, tn_pref=1024):
    m, n = x.shape
    tm = _tile(m, tm_pref, SUBLANE)
    tn = _tile(n, tn_pref, 2 * LANE)
    assert not emit_w or m == tm
    out_specs = [pl.BlockSpec((tm, tn), lambda i, j, k: (i, j))]
    out_shape = [jax.ShapeDtypeStruct((m, n), F32)]
    in_specs, args = [], []
    for a, w in pairs:
        kp = a.shape[1]
        assert kp % n_k == 0 and (w.rows, w.cols) == (kp, n)
        tk = kp // n_k
        in_specs.append(pl.BlockSpec((tm, tk), lambda i, j, k: (i, k)))
        in_specs.append(_weight_spec(w, tk, tn, lambda i, j, k: k, lambda i, j, k: j))
        args += [a, w.array]
        if emit_w:
            out_specs.append(pl.BlockSpec((tk, tn), lambda i, j, k: (k, j)))
            out_shape.append(jax.ShapeDtypeStruct((kp, n), BF16))
    in_specs.append(pl.BlockSpec((tm, tn), lambda i, j, k: (i, j)))
    args.append(x)
    outs = pl.pallas_call(
        functools.partial(_matmul_residual_kernel, n_pairs=len(pairs), emit_w=emit_w),
        grid=(m // tm, n // tn, n_k),
        in_specs=in_specs,
        out_specs=out_specs,
        out_shape=out_shape,
        compiler_params=_params("parallel", "parallel", "arbitrary"),
        name="matmul_residual",
    )(*args)
    return tuple(outs) if emit_w else outs[0]


def _rglru_kernel(p_ref, h0_ref, cb0_ref, cw_ref, cbias_ref, wgx_ref, bgx_ref, wga_ref, bga_ref, lam_ref,
                  y_ref, hl_ref, cbo_ref, ext_ref, hc_ref, *, tr, tile_last, r_last):
    t = pl.program_id(1)
    halo = SUBLANE

    @pl.when(t == 0)
    def _():
        ext_ref[0:halo, :] = jnp.zeros((halo, D_RNN), F32)
        ext_ref[halo - (RG_CONV - 1):halo, :] = cb0_ref[0]
        hc_ref[...] = h0_ref[0]

    ext_ref[halo:halo + tr, :] = p_ref[0, :, 0:D_RNN].astype(F32)
    sub_row = lax.broadcasted_iota(jnp.int32, (tr // SUBLANE, SUBLANE, HEAD), 1)
    for h in range(RG_HEADS):
        cs = slice(h * HEAD, (h + 1) * HEAD)
        xc = cbias_ref[:, cs] + ext_ref[halo - 3:halo - 3 + tr, cs] * cw_ref[0:1, cs]
        for j in range(1, RG_CONV):
            xc = xc + ext_ref[halo - 3 + j:halo - 3 + j + tr, cs] * cw_ref[j:j + 1, cs]
        xcb = xc.astype(BF16)
        gx = jax.nn.sigmoid(jnp.dot(xcb, wgx_ref[h].astype(BF16), preferred_element_type=F32) + bgx_ref[:, cs])
        ga = jax.nn.sigmoid(jnp.dot(xcb, wga_ref[h].astype(BF16), preferred_element_type=F32) + bga_ref[:, cs])
        log_a = (-RG_C) * ga * _softplus(-lam_ref[:, cs])
        a = jnp.exp(log_a)
        u = jnp.sqrt(-jnp.tanh(log_a) * (1.0 + a * a)) * (gx * xc)
        groups = tr // SUBLANE
        a3 = a.reshape(groups, SUBLANE, HEAD)
        u3 = u.reshape(groups, SUBLANE, HEAD)
        d = 1
        while d < SUBLANE:
            live = sub_row >= d
            u3 = jnp.where(live, a3 * pltpu.roll(u3, d, 1) + u3, u3)
            a3 = jnp.where(live, a3 * pltpu.roll(a3, d, 1), a3)
            d *= 2
        h_prev = jnp.broadcast_to(hc_ref[:, cs], (SUBLANE, HEAD))
        h_groups = []
        for k in range(groups):
            h_k = a3[k] * h_prev + u3[k]
            h_groups.append(h_k)
            h_prev = jnp.broadcast_to(h_k[SUBLANE - 1:SUBLANE, :], (SUBLANE, HEAD))
        u = jnp.concatenate(h_groups, axis=0)
        gate = jax.nn.gelu(p_ref[0, :, D_RNN + h * HEAD:D_RNN + (h + 1) * HEAD].astype(F32))
        y_ref[0, :, cs] = (u * gate).astype(y_ref.dtype)
        hc_ref[:, cs] = u[tr - 1:tr, :]

        @pl.when(t == tile_last)
        def _():
            hl_ref[0, :, cs] = u[r_last:r_last + 1, :]

    @pl.when(t == tile_last)
    def _():
        cbo_ref[0] = ext_ref[halo + r_last - 2:halo + r_last + 1, :]

    if tr >= halo:
        ext_ref[0:halo, :] = ext_ref[tr:tr + halo, :]


def rglru(p, h0, cb0, cw, cbias, wgx, bgx, wga, bga, lam, *, t_valid):
    b, t, _ = p.shape
    tr = _tile(t, 256, 16)
    row = lambda v: v.reshape(1, D_RNN).astype(F32)
    full = lambda shape: pl.BlockSpec(shape, lambda i, j: (0,) * len(shape))
    y, hl, cbo = pl.pallas_call(
        functools.partial(_rglru_kernel, tr=tr, tile_last=(t_valid - 1) // tr, r_last=(t_valid - 1) % tr),
        grid=(b, t // tr),
        in_specs=[
            pl.BlockSpec((1, tr, 2 * D_RNN), lambda i, j: (i, j, 0)),
            pl.BlockSpec((1, 1, D_RNN), lambda i, j: (i, 0, 0)),
            pl.BlockSpec((1, RG_CONV - 1, D_RNN), lambda i, j: (i, 0, 0)),
            full((RG_CONV, D_RNN)), full((1, D_RNN)),
            full((RG_HEADS, HEAD, HEAD)), full((1, D_RNN)),
            full((RG_HEADS, HEAD, HEAD)), full((1, D_RNN)),
            full((1, D_RNN)),
        ],
        out_specs=[
            pl.BlockSpec((1, tr, D_RNN), lambda i, j: (i, j, 0)),
            pl.BlockSpec((1, 1, D_RNN), lambda i, j: (i, 0, 0)),
            pl.BlockSpec((1, RG_CONV - 1, D_RNN), lambda i, j: (i, 0, 0)),
        ],
        out_shape=[
            jax.ShapeDtypeStruct((b, t, D_RNN), BF16),
            jax.ShapeDtypeStruct((b, 1, D_RNN), F32),
            jax.ShapeDtypeStruct((b, RG_CONV - 1, D_RNN), F32),
        ],
        scratch_shapes=[pltpu.VMEM((tr + SUBLANE, D_RNN), F32), pltpu.VMEM((1, D_RNN), F32)],
        compiler_params=_params("parallel", "arbitrary"),
        name="rglru",
    )(p, h0.reshape(b, 1, D_RNN), cb0, cw.astype(F32), row(cbias), wgx, row(bgx), wga, row(bga), row(lam))
    return y, hl.reshape(b, D_RNN), cbo


def _mem_attn_kernel(q_ref, k_ref, v_ref, o_ref):
    for h in range(MEM_HEADS):
        cs = slice(h * HEAD, (h + 1) * HEAD)
        q = q_ref[0, :, cs]
        k = k_ref[0, :, cs].astype(BF16)
        v = v_ref[0, :, cs].astype(BF16)
        s = lax.dot_general(q, k, (((1,), (1,)), ((), ())), preferred_element_type=F32)
        e = jnp.exp(s - jnp.max(s, axis=-1, keepdims=True))
        p = e / jnp.sum(e, axis=-1, keepdims=True)
        o_ref[0, :, cs] = jnp.dot(p.astype(BF16), v, preferred_element_type=F32).astype(o_ref.dtype)


def mem_attn(q, q_col_block, mk, mv):
    b, t, _ = q.shape
    tq = _tile(t, 512, 16)
    return pl.pallas_call(
        _mem_attn_kernel,
        grid=(b, t // tq),
        in_specs=[
            pl.BlockSpec((1, tq, D_MEM), lambda i, j: (i, j, q_col_block)),
            pl.BlockSpec((1, MEM_TOKENS, D_MEM), lambda i, j: (i, 0, 0)),
            pl.BlockSpec((1, MEM_TOKENS, D_MEM), lambda i, j: (i, 0, 0)),
        ],
        out_specs=pl.BlockSpec((1, tq, D_MEM), lambda i, j: (i, j, 0)),
        out_shape=jax.ShapeDtypeStruct((b, t, D_MEM), BF16),
        compiler_params=_params("parallel", "parallel"),
        name="mem_attn",
    )(q, mk, mv)


def _shift_rows(u, shift, head_rows):
    s = pltpu.roll(u, shift, 0)
    row = lax.broadcasted_iota(jnp.int32, (SUBLANE, u.shape[1]), 0)
    top = s[0:SUBLANE]
    for r, v in enumerate(head_rows):
        top = jnp.where(row == r, v, top)
    return jnp.concatenate([top, s[SUBLANE:]], axis=0)


def _conv_gelu_gate(u, gate, prev2, prev1, cw_ref, cb_ref):
    uc = (cb_ref[...] + _shift_rows(u, 2, [prev2, prev1]) * cw_ref[0:1, :]
          + _shift_rows(u, 1, [prev1]) * cw_ref[1:2, :] + u * cw_ref[2:3, :])
    return jax.nn.gelu(uc) * gate


def _ffn_up_act_kernel(x_ref, g_ref, wu_in_ref, wg_in_ref, b0_ref, cw_ref, cb_ref, act_ref, nb_ref, *rest,
                       seq, tiles_per_seq, t_valid, cast_w):
    i = pl.program_id(0)
    j = pl.program_id(1)
    tm = x_ref.shape[0]
    if cast_w:
        wu_ref, wg_ref, xn_ref, tail_ref = rest
        wu_ref[...] = wu_in_ref[...].astype(BF16)
        wg_ref[...] = wg_in_ref[...].astype(BF16)
    else:
        xn_ref, tail_ref = rest
        wu_ref, wg_ref = wu_in_ref, wg_in_ref

    @pl.when(j == 0)
    def _():
        x = x_ref[...]
        ms = jnp.mean(x * x, axis=-1, keepdims=True)
        xn_ref[...] = (x * lax.rsqrt(ms + EPS) * g_ref[...]).astype(BF16)

    if tiles_per_seq >= 1:
        @pl.when(i == 0)
        def _():
            tail_ref[j] = jnp.zeros(tail_ref.shape[1:], F32)

        first = (i % tiles_per_seq) == 0
        prev2 = jnp.where(first, b0_ref[0, 0:1, :], tail_ref[j, SUBLANE - 2:SUBLANE - 1, :])
        prev1 = jnp.where(first, b0_ref[0, 1:2, :], tail_ref[j, SUBLANE - 1:SUBLANE, :])
        rc = min(tm, FFN_ROW_CHUNK)
        for c in range(tm // rc):
            xs = xn_ref[c * rc:(c + 1) * rc, :]
            u = jnp.dot(xs, wu_ref[...], preferred_element_type=F32)
            gate = jnp.dot(xs, wg_ref[...], preferred_element_type=F32)
            act_ref[c * rc:(c + 1) * rc, :] = _conv_gelu_gate(u, gate, prev2, prev1, cw_ref, cb_ref
                                                              ).astype(act_ref.dtype)
            prev2, prev1 = u[rc - 2:rc - 1], u[rc - 1:rc]
        tail_ref[j] = u[rc - SUBLANE:rc]

        @pl.when((i % tiles_per_seq) == tiles_per_seq - 1)
        def _():
            nb_ref[i // tiles_per_seq, j] = u[rc - 2:rc]
    else:
        xn = xn_ref[...]
        u = jnp.dot(xn, wu_ref[...], preferred_element_type=F32)
        gate = jnp.dot(xn, wg_ref[...], preferred_element_type=F32)
        for b in range(tm // seq):
            rs = slice(b * seq, (b + 1) * seq)
            a = _conv_gelu_gate(u[rs], gate[rs], b0_ref[b, 0:1, :], b0_ref[b, 1:2, :], cw_ref, cb_ref)
            act_ref[rs, :] = a.astype(act_ref.dtype)
            nb_ref[(tm // seq) * i + b, j] = u[b * seq + t_valid - 2:b * seq + t_valid]


def ffn_up_act(x, g, w_u, w_g, buf0, cw, cb, *, seq, t_valid, emit_w=False, tm_pref=1024):
    m, d = x.shape
    assert (w_u.rows, w_u.cols) == (d, D_FF) and (w_g.rows, w_g.cols) == (d, D_FF)
    b = m // seq
    tm = _tile(m, tm_pref, SUBLANE)
    assert t_valid >= FFN_CONV - 1 and (seq % tm == 0 or tm % seq == 0)
    tiles_per_seq = seq // tm if tm <= seq else 0
    assert tiles_per_seq == 0 or t_valid == seq
    seqs_per_tile = max(tm // seq, 1)
    cast_w = w_u.array.dtype != BF16
    tn = 512 if cast_w else 1024
    n_j = D_FF // tn
    assert not emit_w or (cast_w and m == tm)
    out_specs = [
        pl.BlockSpec((tm, tn), lambda i, j: (i, j)),
        pl.BlockSpec((b, n_j, FFN_CONV - 1, tn), lambda i, j: (0, 0, 0, 0)),
    ]
    out_shape = [
        jax.ShapeDtypeStruct((m, D_FF), BF16),
        jax.ShapeDtypeStruct((b, n_j, FFN_CONV - 1, tn), F32),
    ]
    scratch = [pltpu.VMEM((tm, d), BF16), pltpu.VMEM((n_j, SUBLANE, tn), F32)]
    if emit_w:
        out_specs += [pl.BlockSpec((d, tn), lambda i, j: (0, j)), pl.BlockSpec((d, tn), lambda i, j: (0, j))]
        out_shape += [jax.ShapeDtypeStruct((d, D_FF), BF16), jax.ShapeDtypeStruct((d, D_FF), BF16)]
    elif cast_w:
        scratch = [pltpu.VMEM((d, tn), BF16), pltpu.VMEM((d, tn), BF16)] + scratch
    outs = pl.pallas_call(
        functools.partial(_ffn_up_act_kernel, seq=seq, tiles_per_seq=tiles_per_seq, t_valid=t_valid,
                          cast_w=cast_w),
        grid=(m // tm, n_j),
        in_specs=[
            pl.BlockSpec((tm, d), lambda i, j: (i, 0)),
            pl.BlockSpec((1, d), lambda i, j: (0, 0)),
            _weight_spec(w_u, d, tn, lambda i, j: 0, lambda i, j: j),
            _weight_spec(w_g, d, tn, lambda i, j: 0, lambda i, j: j),
            pl.BlockSpec((seqs_per_tile, FFN_CONV - 1, tn),
                         lambda i, j: ((i // tiles_per_seq) if tiles_per_seq else i, 0, j)),
            pl.BlockSpec((FFN_CONV, tn), lambda i, j: (0, j)),
            pl.BlockSpec((1, tn), lambda i, j: (0, j)),
        ],
        out_specs=out_specs,
        out_shape=out_shape,
        scratch_shapes=scratch,
        compiler_params=_params("arbitrary", "arbitrary"),
        name="ffn_up_act",
    )(x, g.reshape(1, d).astype(F32), w_u.array, w_g.array, buf0, cw.astype(F32),
      cb.reshape(1, D_FF).astype(F32))
    nb = outs[1].transpose(0, 2, 1, 3).reshape(b, FFN_CONV - 1, D_FF)
    return (outs[0], nb) + tuple(outs[2:])


def _suffix_matrix():
    j = lax.broadcasted_iota(jnp.int32, (2 * HEAD, 2 * HEAD), 0) % HEAD
    s = lax.broadcasted_iota(jnp.int32, (2 * HEAD, 2 * HEAD), 1)
    return jnp.where((s >= HEAD) | (j >= s), 1.0, 0.0).astype(BF16)


def _sb_weights(s, neg_bias, carry, suffix, mask):
    nz, lk = _sb_log_keep(s, neg_bias, mask)
    return _sb_finish(nz, _sb_suffix_sums(lk, suffix), carry, mask)


def _sb_log_keep(s, neg_bias, mask):
    nz = neg_bias - s
    neg_abs = pltpu.bitcast(pltpu.bitcast(nz, jnp.uint32) | jnp.uint32(0x80000000), F32)
    lk = jnp.minimum(nz, 0.0) - jnp.log2(1.0 + jnp.exp2(neg_abs))
    if mask is not None:
        lk = jnp.where(mask, lk, 0.0)
    return nz, lk


def _sb_suffix_sums(lk, suffix):
    hi = lk.astype(BF16)
    lo = (lk - hi.astype(F32)).astype(BF16)
    return jnp.dot(jnp.concatenate([hi, lo], axis=1), suffix, preferred_element_type=F32)


def _sb_finish(nz, c2, carry, mask):
    w = jnp.exp2((carry + c2[:, :HEAD]) - nz)
    if mask is not None:
        w = jnp.where(mask, w, 0.0)
    return w, carry + c2[:, HEAD:]


def _nt_dot(a, b):
    return lax.dot_general(a, b, (((1,), (1,)), ((), ())), preferred_element_type=F32)


def _sb_prompt_kernel(nbias_ref, q_ref, k_ref, v_ref, o_ref, carry_ref, acc_ref, *, tq, hp):
    hg = pl.program_id(1)
    qi = pl.program_id(2)
    suffix = _suffix_matrix()
    n_sub = tq // HEAD
    carry_ref[...] = jnp.zeros(carry_ref.shape, F32)
    acc_ref[...] = jnp.zeros(acc_ref.shape, F32)

    def block(start, masked):
        for e in range(hp):
            neg_bias = nbias_ref[hg * hp + e]
            s = _nt_dot(q_ref[0, :, e * HEAD:(e + 1) * HEAD], k_ref[0, e, pl.ds(start, tq), :])
            carry = carry_ref[e]
            ws = [None] * n_sub
            for j in range(n_sub - 1, -1, -1):
                if masked:
                    r0 = j * HEAD
                    mask = (lax.broadcasted_iota(jnp.int32, (tq - r0, HEAD), 1)
                            < lax.broadcasted_iota(jnp.int32, (tq - r0, HEAD), 0))
                    w, c = _sb_weights(s[r0:, j * HEAD:(j + 1) * HEAD], neg_bias, carry[r0:], suffix, mask)
                    if r0:
                        w = jnp.concatenate([jnp.zeros((r0, HEAD), F32), w], axis=0)
                        c = jnp.concatenate([carry[:r0], c], axis=0)
                    carry = c
                else:
                    w, carry = _sb_weights(s[:, j * HEAD:(j + 1) * HEAD], neg_bias, carry, suffix, None)
                ws[j] = w.astype(BF16)
            carry_ref[e] = carry
            acc_ref[e] += jnp.dot(jnp.concatenate(ws, axis=1), v_ref[0, e, pl.ds(start, tq), :],
                                  preferred_element_type=F32)

    block(pl.multiple_of(qi * tq, tq), True)

    odd = qi % 2

    @pl.when(odd == 1)
    def _():
        block(pl.multiple_of((qi - 1) * tq, tq), False)

    @pl.loop(0, qi // 2)
    def _(i):
        later = qi - odd - 1 - 2 * i
        block(pl.multiple_of(later * tq, tq), False)
        block(pl.multiple_of((later - 1) * tq, tq), False)

    for e in range(hp):
        o_ref[0, :, e * HEAD:(e + 1) * HEAD] = acc_ref[e].astype(o_ref.dtype)


def sb_prompt(q, k, v, bias):
    b, _, t, _ = k.shape
    tq = _tile(t, 512, HEAD)
    hp = SB_HEADS_PER_STEP
    return pl.pallas_call(
        functools.partial(_sb_prompt_kernel, tq=tq, hp=hp),
        grid_spec=pltpu.PrefetchScalarGridSpec(
            num_scalar_prefetch=1,
            grid=(b, SB_HEADS // hp, t // tq),
            in_specs=[
                pl.BlockSpec((1, tq, hp * HEAD), lambda i, h, j, nb: (i, j, h)),
                pl.BlockSpec((1, hp, t, HEAD), lambda i, h, j, nb: (i, h, 0, 0)),
                pl.BlockSpec((1, hp, t, HEAD), lambda i, h, j, nb: (i, h, 0, 0)),
            ],
            out_specs=pl.BlockSpec((1, tq, hp * HEAD), lambda i, h, j, nb: (i, j, h)),
            scratch_shapes=[pltpu.VMEM((hp, tq, HEAD), F32), pltpu.VMEM((hp, tq, HEAD), F32)],
        ),
        out_shape=jax.ShapeDtypeStruct((b, t, D_RNN), BF16),
        compiler_params=_params("parallel", "parallel", "arbitrary"),
        name="sb_prompt",
    )(-LOG2E * bias.astype(F32), q, k, v)


def _sb_sample_kernel(pt_ref, q_ref, kn_ref, vn_ref, bias_ref, *rest, pages_per_step, t_valid):
    kp_refs = rest[:pages_per_step]
    vp_refs = rest[pages_per_step:2 * pages_per_step]
    o_ref, carry_ref, acc_ref = rest[2 * pages_per_step:]
    g = pl.program_id(1)
    rows = SB_HEADS * Q_ROWS
    suffix = _suffix_matrix()
    neg_bias = bias_ref[...]
    hcols = [slice(h * HEAD, (h + 1) * HEAD) for h in range(SB_HEADS)]
    q_heads = [q_ref[0, :, hcols[h]] for h in range(SB_HEADS)]

    def update(blocks, mask):
        n = len(blocks)
        s_heads = [_nt_dot(q_heads[h], jnp.concatenate([blocks[r][0][h] for r in range(n)], axis=0))[0:Q_ROWS]
                   for h in range(SB_HEADS)]
        nzs, lks = [], []
        for r in range(n):
            s = jnp.concatenate([sh[:, r * HEAD:(r + 1) * HEAD] for sh in s_heads], axis=0)
            nz, lk = _sb_log_keep(s, neg_bias, mask)
            nzs.append(nz)
            lks.append(lk)
        c2 = _sb_suffix_sums(jnp.concatenate(lks, axis=0), suffix)
        carry = carry_ref[...]
        ws = []
        for r in range(n):
            w, carry = _sb_finish(nzs[r], c2[r * rows:(r + 1) * rows], carry, mask)
            ws.append(w)
        carry_ref[...] = carry
        zpad = jnp.zeros((SAMPLE_T_PAD - Q_ROWS, n * HEAD), F32)
        for h in range(SB_HEADS):
            rs = slice(h * Q_ROWS, (h + 1) * Q_ROWS)
            wh = jnp.concatenate([jnp.concatenate([ws[r][rs] for r in range(n)], axis=1), zpad], axis=0)
            vh = jnp.concatenate([blocks[r][1][h] for r in range(n)], axis=0)
            acc_ref[rs, :] += jnp.dot(wh.astype(BF16), vh, preferred_element_type=F32)[0:Q_ROWS]

    @pl.when(g == 0)
    def _():
        carry_ref[...] = jnp.zeros((rows, HEAD), F32)
        acc_ref[...] = jnp.zeros((rows, HEAD), F32)
        kz = jnp.zeros((HEAD - SAMPLE_T_PAD, HEAD), BF16)
        k_heads = [jnp.concatenate([kn_ref[0, :, hcols[h]], kz], axis=0) for h in range(SB_HEADS)]
        v_heads = [jnp.concatenate([vn_ref[0, :, hcols[h]], kz], axis=0) for h in range(SB_HEADS)]
        i_q = lax.broadcasted_iota(jnp.int32, (rows, HEAD), 0) % Q_ROWS
        j_k = lax.broadcasted_iota(jnp.int32, (rows, HEAD), 1)
        update([(k_heads, v_heads)], (j_k < i_q) & (j_k < t_valid))

    update([([kp_refs[r][0, h].astype(BF16) for h in range(SB_HEADS)],
             [vp_refs[r][0, h].astype(BF16) for h in range(SB_HEADS)]) for r in range(pages_per_step)], None)

    @pl.when(g == pl.num_programs(1) - 1)
    def _():
        z8 = jnp.zeros((SAMPLE_T_PAD - Q_ROWS, HEAD), F32)
        for h in range(SB_HEADS):
            blk = jnp.concatenate([acc_ref[h * Q_ROWS:(h + 1) * Q_ROWS, :], z8], axis=0)
            o_ref[0, :, h * HEAD:(h + 1) * HEAD] = blk.astype(o_ref.dtype)


def sb_sample(q, k_new, v_new, cache_k, cache_v, page_table, bias, *, t_valid, pages_per_step=8):
    b, n_pages = page_table.shape
    page = cache_k.shape[2]
    assert page == HEAD and n_pages % pages_per_step == 0 and t_valid <= Q_ROWS
    assert cache_k.shape[1:] == (SB_HEADS, page, HEAD)
    rows = SB_HEADS * Q_ROWS
    bias_rows = jnp.broadcast_to(jnp.repeat(-LOG2E * bias.astype(F32), Q_ROWS)[:, None], (rows, HEAD))

    def page_spec(r):
        return pl.BlockSpec(
            (1, SB_HEADS, page, HEAD),
            lambda i, g, pt: (pt[i, n_pages - 1 - (g * pages_per_step + r)], 0, 0, 0))

    return pl.pallas_call(
        functools.partial(_sb_sample_kernel, pages_per_step=pages_per_step, t_valid=t_valid),
        grid_spec=pltpu.PrefetchScalarGridSpec(
            num_scalar_prefetch=1,
            grid=(b, n_pages // pages_per_step),
            in_specs=[
                pl.BlockSpec((1, SAMPLE_T_PAD, D_RNN), lambda i, g, pt: (i, 0, 0)),
                pl.BlockSpec((1, SAMPLE_T_PAD, D_RNN), lambda i, g, pt: (i, 0, 0)),
                pl.BlockSpec((1, SAMPLE_T_PAD, D_RNN), lambda i, g, pt: (i, 0, 0)),
                pl.BlockSpec((rows, HEAD), lambda i, g, pt: (0, 0)),
            ] + [page_spec(r) for r in range(pages_per_step)] + [page_spec(r) for r in range(pages_per_step)],
            out_specs=pl.BlockSpec((1, SAMPLE_T_PAD, D_RNN), lambda i, g, pt: (i, 0, 0)),
            scratch_shapes=[pltpu.VMEM((rows, HEAD), F32), pltpu.VMEM((rows, HEAD), F32)],
        ),
        out_shape=jax.ShapeDtypeStruct((b, SAMPLE_T_PAD, D_RNN), BF16),
        compiler_params=_params("parallel", "arbitrary"),
        name="sb_sample",
    )(page_table, q, k_new, v_new, bias_rows, *([cache_k] * pages_per_step), *([cache_v] * pages_per_step))


def _trunk(x3, t_valid, rg_h0, rg_conv0, ffn_conv0, mem_k, mem_v, sb_attend, head_major_kv, P, W, emit_w):
    b, t, d = x3.shape
    m = b * t
    x = x3.reshape(m, d)
    rg_h_out, rg_conv_out, ffn_conv_out = [], [], []
    k_f32 = v_f32 = k_bf = v_bf = None
    emitted = {}

    def nmm(key, x, g, **kw):
        out = norm_matmul(x, g, W[key], emit_w=emit_w, **kw)
        if not emit_w:
            return out
        *out, emitted[key] = out
        return out[0] if len(out) == 1 else tuple(out)

    def mmr(keys, acts, x, **kw):
        out = matmul_residual([(a, W[k]) for a, k in zip(acts, keys)], x, emit_w=emit_w, **kw)
        if not emit_w:
            return out
        for k, wb in zip(keys, out[1:]):
            emitted[k] = wb
        return out[0]

    for l in range(DEPTH):
        mem_gain = jnp.tile(P['mem_q_norm'][l] * ATTN_SCALE, MEM_HEADS)
        if l < N_A_LAYERS:
            p_rg = nmm(('in_a_rg', l), x, P['g_mix'][l])
            qm = nmm(('in_a_qm', l), x, P['g_mix'][l], head_gain=mem_gain)
            y_tok, h_last, cbuf = rglru(
                p_rg.reshape(b, t, 2 * D_RNN), rg_h0[l], rg_conv0[l], P['rg_conv_w'][l], P['rg_conv_b'][l],
                P['rg_gate_x_w'][l], P['rg_gate_x_b'][l], P['rg_gate_a_w'][l], P['rg_gate_a_b'][l],
                P['rg_lambda'][l], t_valid=t_valid)
            rg_h_out.append(h_last)
            rg_conv_out.append(cbuf)
            q3, q_col = qm.reshape(b, t, D_MEM), 0
        else:
            j = l - N_A_LAYERS
            gain = jnp.concatenate([jnp.tile(P['sb_q_norm'][j] * (ATTN_SCALE * LOG2E), SB_HEADS), mem_gain])
            pq = nmm(('in_b', l), x, P['g_mix'][l], head_gain=gain)
            q3, q_col = pq.reshape(b, t, D_RNN + D_MEM), D_RNN // D_MEM
            y_tok = sb_attend(q3, k_bf, v_bf, P['sb_beta_bias'][j])
        y_mem = mem_attn(q3, q_col, mem_k[l], mem_v[l])
        x = mmr([('out_tok', l), ('out_mem', l)], [y_tok.reshape(m, D_RNN), y_mem.reshape(m, D_MEM)], x)
        act, fbuf, *wbs = ffn_up_act(x, P['g_ffn'][l], W[('ffn_u', l)], W[('ffn_g', l)], ffn_conv0[l],
                                     P['ffn_conv_w'][l], P['ffn_conv_b'][l], seq=t, t_valid=t_valid,
                                     emit_w=emit_w)
        if emit_w:
            emitted[('ffn_u', l)], emitted[('ffn_g', l)] = wbs
        x = mmr([('ffn_down', l)], [act], x, tn_pref=512)
        ffn_conv_out.append(fbuf)
        if l == N_A_LAYERS - 1:
            k_gain = jnp.tile(P['kv_k_norm'], SB_HEADS)
            hm = t if head_major_kv else None
            k_f32, k_bf = nmm(('k', l), x, P['kv_norm'], head_gain=k_gain, out_dtypes=(F32, BF16),
                              head_major_seq=hm)
            v_f32, v_bf = nmm(('v', l), x, P['kv_norm'], out_dtypes=(F32, BF16), head_major_seq=hm)
            if not head_major_kv:
                k_bf = k_bf.reshape(b, t, D_RNN)
                v_bf = v_bf.reshape(b, t, D_RNN)
    if head_major_kv:
        k_out, v_out = k_f32.transpose(0, 2, 1, 3), v_f32.transpose(0, 2, 1, 3)
    else:
        k_out, v_out = k_f32.reshape(b, t, SB_HEADS, HEAD), v_f32.reshape(b, t, SB_HEADS, HEAD)
    return (x.reshape(b, t, d), jnp.stack(rg_h_out), jnp.stack(rg_conv_out), jnp.stack(ffn_conv_out),
            k_out, v_out, emitted)


def kernel(x_prompt, x_sample, mem_prompt, state_rglru_h, state_rglru_conv, state_ffn_conv, cache_mem_k, cache_mem_v, cache_sb_k, cache_sb_v, page_table, g_mix, g_ffn, w_in_a, rg_conv_w, rg_conv_b, rg_gate_x_w, rg_gate_x_b, rg_gate_a_w, rg_gate_a_b, rg_lambda, w_in_b, sb_q_norm, sb_beta_bias, kv_norm, w_kv, kv_k_norm, mem_norm, w_mem_kv, mem_q_norm, mem_k_norm, w_out, w_ffn_up, ffn_conv_w, ffn_conv_b, w_ffn_down):
    P = {'g_mix': g_mix, 'g_ffn': g_ffn, 'rg_conv_w': rg_conv_w, 'rg_conv_b': rg_conv_b,
         'rg_gate_x_b': rg_gate_x_b, 'rg_gate_a_b': rg_gate_a_b, 'rg_lambda': rg_lambda,
         'sb_q_norm': sb_q_norm, 'sb_beta_bias': sb_beta_bias, 'kv_norm': kv_norm,
         'kv_k_norm': kv_k_norm, 'mem_q_norm': mem_q_norm,
         'ffn_conv_w': ffn_conv_w, 'ffn_conv_b': ffn_conv_b,
         'rg_gate_x_w': rg_gate_x_w, 'rg_gate_a_w': rg_gate_a_w}
    dm = D_MODEL
    W = {}
    for l in range(DEPTH):
        if l < N_A_LAYERS:
            W[('in_a_rg', l)] = WeightView(w_in_a, l, 0, dm, 0, 2 * D_RNN)
            W[('in_a_qm', l)] = WeightView(w_in_a, l, 0, dm, 2 * D_RNN, D_MEM)
        else:
            W[('in_b', l)] = WeightView(w_in_b, l - N_A_LAYERS, 0, dm, 0, D_RNN + D_MEM)
        W[('out_tok', l)] = WeightView(w_out, l, 0, D_RNN, 0, dm)
        W[('out_mem', l)] = WeightView(w_out, l, D_RNN, D_MEM, 0, dm)
        W[('ffn_u', l)] = WeightView(w_ffn_up, l, 0, dm, 0, D_FF)
        W[('ffn_g', l)] = WeightView(w_ffn_up, l, 0, dm, D_FF, D_FF)
        W[('ffn_down', l)] = WeightView(w_ffn_down, l, 0, D_FF, 0, dm)
    W[('k', N_A_LAYERS - 1)] = WeightView(w_kv, None, 0, dm, 0, D_RNN)
    W[('v', N_A_LAYERS - 1)] = WeightView(w_kv, None, 0, dm, D_RNN, D_RNN)

    db, dec_seq, _ = x_sample.shape
    xs = jnp.pad(x_sample, ((0, 0), (0, SAMPLE_T_PAD - dec_seq), (0, 0)))

    cache_k_hm = cache_sb_k.transpose(0, 2, 1, 3)
    cache_v_hm = cache_sb_v.transpose(0, 2, 1, 3)

    def sb_paged(q3, k_bf, v_bf, bias):
        return sb_sample(q3, k_bf, v_bf, cache_k_hm, cache_v_hm, page_table, bias, t_valid=dec_seq)

    (y_s, sample_rglru_h, sample_rglru_conv, sample_ffn_conv, s_k, s_v, w_bf16) = _trunk(
        xs, dec_seq, state_rglru_h, state_rglru_conv, state_ffn_conv,
        cache_mem_k.reshape(DEPTH, db, MEM_TOKENS, D_MEM), cache_mem_v.reshape(DEPTH, db, MEM_TOKENS, D_MEM),
        sb_paged, False, P, W, True)
    W_prompt = {key: _bf16_view(arr) for key, arr in w_bf16.items()}

    bp, seq, d = x_prompt.shape
    mem2 = mem_prompt.reshape(bp * MEM_TOKENS, d)
    mk_list, mv_list = [], []
    for l in range(DEPTH):
        mk_list.append(norm_matmul(mem2, mem_norm[l], WeightView(w_mem_kv, l, 0, dm, 0, D_MEM),
                                   head_gain=jnp.tile(mem_k_norm[l], MEM_HEADS), out_dtypes=(F32,)))
        mv_list.append(norm_matmul(mem2, mem_norm[l], WeightView(w_mem_kv, l, 0, dm, D_MEM, D_MEM),
                                   out_dtypes=(F32,)))
    prompt_mem_k = jnp.stack(mk_list).reshape(DEPTH, bp, MEM_TOKENS, D_MEM)
    prompt_mem_v = jnp.stack(mv_list).reshape(DEPTH, bp, MEM_TOKENS, D_MEM)
    zeros_h = jnp.zeros((N_A_LAYERS, bp, D_RNN), F32)
    zeros_rc = jnp.zeros((N_A_LAYERS, bp, RG_CONV - 1, D_RNN), F32)
    zeros_fc = jnp.zeros((DEPTH, bp, FFN_CONV - 1, D_FF), F32)
    (y_prompt, prompt_rglru_h, prompt_rglru_conv, prompt_ffn_conv, prompt_sb_k, prompt_sb_v, _) = _trunk(
        x_prompt, seq, zeros_h, zeros_rc, zeros_fc, prompt_mem_k, prompt_mem_v, sb_prompt, True, P, W_prompt,
        False)

    mem_shape = (DEPTH, bp, MEM_TOKENS, MEM_HEADS, HEAD)
    return (y_prompt, y_s[:, :dec_seq],
            prompt_rglru_h, prompt_rglru_conv, prompt_ffn_conv,
            prompt_sb_k, prompt_sb_v, prompt_mem_k.reshape(mem_shape), prompt_mem_v.reshape(mem_shape),
            sample_rglru_h, sample_rglru_conv, sample_ffn_conv,
            s_k[:, :dec_seq], s_v[:, :dec_seq])
```

```python
import functools
import math
from typing import NamedTuple

import jax
import jax.numpy as jnp
from jax import lax
from jax.experimental import pallas as pl
from jax.experimental.pallas import tpu as pltpu

F32 = jnp.float32
BF16 = jnp.bfloat16

LANE = 128
SUBLANE = 8
VMEM_LIMIT_BYTES = 56 * 1024 * 1024

D_MODEL = 2048
DEPTH = 4
N_A_LAYERS = 2
D_RNN = 1536
HEAD = 128
RG_HEADS = D_RNN // HEAD
RG_CONV = 4
RG_C = 8.0
SB_HEADS = D_RNN // HEAD
MEM_TOKENS = 256
MEM_HEADS = 4
D_MEM = MEM_HEADS * HEAD
D_FF = 3 * D_MODEL
FFN_CONV = 3
EPS = 1e-6
ATTN_SCALE = HEAD ** -0.5
LOG2E = math.log2(math.e)
SAMPLE_T_PAD = 16
Q_ROWS = 8
FFN_ROW_CHUNK = 256
NORM_ROW_CHUNK = 256
SB_HEADS_PER_STEP = 6


def _params(*sem):
    return pltpu.CompilerParams(dimension_semantics=sem, vmem_limit_bytes=VMEM_LIMIT_BYTES)


def _tile(n, pref, mult):
    if n <= pref:
        return n
    t = (pref // mult) * mult
    while t > mult and n % t:
        t -= mult
    assert n % t == 0, (n, pref, mult)
    return t


class WeightView(NamedTuple):
    array: jax.Array
    layer: int | None
    row0: int
    rows: int
    col0: int
    cols: int


def _weight_spec(wv, tk, tn, row_block, col_block):
    assert wv.row0 % tk == 0 and wv.col0 % tn == 0 and wv.rows % tk == 0 and wv.cols % tn == 0
    r0, c0 = wv.row0 // tk, wv.col0 // tn
    if wv.layer is None:
        return pl.BlockSpec((tk, tn), lambda *g: (r0 + row_block(*g), c0 + col_block(*g)))
    layer = wv.layer
    return pl.BlockSpec((None, tk, tn), lambda *g: (layer, r0 + row_block(*g), c0 + col_block(*g)))


def _softplus(z):
    return jnp.maximum(z, 0.0) + jnp.log1p(jnp.exp(-jnp.abs(z)))


def _head_rmsnorm(blk, gain):
    ms = jnp.mean(blk * blk, axis=-1, keepdims=True)
    return blk * lax.rsqrt(ms + EPS) * gain


def _norm_matmul_kernel(*refs, head_norm, head_major, n_out, emit_w):
    x_ref, g_ref, w_ref = refs[:3]
    pos = 3
    hg_ref = None
    if head_norm:
        hg_ref = refs[pos]
        pos += 1
    o_refs = refs[pos:pos + n_out]
    pos += n_out
    wb_ref = None
    if emit_w:
        wb_ref = refs[pos]
        pos += 1
    xn_ref = refs[pos]

    @pl.when(pl.program_id(1) == 0)
    def _():
        x = x_ref[...]
        ms = jnp.mean(x * x, axis=-1, keepdims=True)
        xn_ref[...] = (x * lax.rsqrt(ms + EPS) * g_ref[...]).astype(BF16)

    wb = w_ref[...].astype(BF16)
    if emit_w:
        wb_ref[...] = wb
    tm = xn_ref.shape[0]
    tn = wb.shape[1]
    if head_norm or head_major:
        rc = min(tm, NORM_ROW_CHUNK)
        for c in range(tm // rc):
            rows = slice(c * rc, (c + 1) * rc)
            acc = jnp.dot(xn_ref[rows, :], wb, preferred_element_type=F32)
            for h in range(tn // HEAD):
                cs = slice(h * HEAD, (h + 1) * HEAD)
                y = _head_rmsnorm(acc[:, cs], hg_ref[:, cs]) if head_norm else acc[:, cs]
                for o_ref in o_refs:
                    if head_major:
                        o_ref[0, h, rows, :] = y.astype(o_ref.dtype)
                    else:
                        o_ref[rows, cs] = y.astype(o_ref.dtype)
    else:
        acc = jnp.dot(xn_ref[...], wb, preferred_element_type=F32)
        for o_ref in o_refs:
            o_ref[...] = acc.astype(o_ref.dtype)


def _bf16_view(arr):
    return WeightView(arr, None, 0, arr.shape[0], 0, arr.shape[1])


def norm_matmul(x, g, w, *, head_gain=None, out_dtypes=(BF16,), head_major_seq=None, emit_w=False,
                tm_pref=1024, tn_pref=1024):
    m, d = x.shape
    n = w.cols
    assert w.rows == d
    tm = _tile(m, tm_pref, SUBLANE)
    tn = _tile(n, tn_pref, 2 * LANE)
    assert not emit_w or (m == tm and head_major_seq is None)
    if head_major_seq is None:
        out_specs = [pl.BlockSpec((tm, tn), lambda i, j: (i, j)) for _ in out_dtypes]
        out_shape = [jax.ShapeDtypeStruct((m, n), dt) for dt in out_dtypes]
    else:
        seq = head_major_seq
        assert seq % tm == 0 and m % seq == 0
        per_seq = seq // tm
        out_specs = [pl.BlockSpec((1, tn // HEAD, tm, HEAD), lambda i, j: (i // per_seq, j, i % per_seq, 0))
                     for _ in out_dtypes]
        out_shape = [jax.ShapeDtypeStruct((m // seq, n // HEAD, seq, HEAD), dt) for dt in out_dtypes]
    in_specs = [
        pl.BlockSpec((tm, d), lambda i, j: (i, 0)),
        pl.BlockSpec((1, d), lambda i, j: (0, 0)),
        _weight_spec(w, d, tn, lambda i, j: 0, lambda i, j: j),
    ]
    args = [x, g.reshape(1, d).astype(F32), w.array]
    if head_gain is not None:
        in_specs.append(pl.BlockSpec((1, tn), lambda i, j: (0, j)))
        args.append(head_gain.reshape(1, n).astype(F32))
    if emit_w:
        out_specs.append(pl.BlockSpec((d, tn), lambda i, j: (0, j)))
        out_shape.append(jax.ShapeDtypeStruct((d, n), BF16))
    outs = pl.pallas_call(
        functools.partial(_norm_matmul_kernel, head_norm=head_gain is not None,
                          head_major=head_major_seq is not None, n_out=len(out_dtypes), emit_w=emit_w),
        grid=(m // tm, n // tn),
        in_specs=in_specs,
        out_specs=out_specs,
        out_shape=out_shape,
        scratch_shapes=[pltpu.VMEM((tm, d), BF16)],
        compiler_params=_params("parallel", "arbitrary"),
        name="norm_matmul",
    )(*args)
    return outs[0] if len(outs) == 1 else tuple(outs)


def _matmul_residual_kernel(*refs, n_pairs, emit_w):
    x_ref = refs[2 * n_pairs]
    o_ref = refs[2 * n_pairs + 1]
    wb_refs = refs[2 * n_pairs + 2:]

    @pl.when(pl.program_id(2) == 0)
    def _():
        o_ref[...] = x_ref[...]

    acc = None
    for p in range(n_pairs):
        wb = refs[2 * p + 1][...].astype(BF16)
        if emit_w:
            wb_refs[p][...] = wb
        d = jnp.dot(refs[2 * p][...], wb, preferred_element_type=F32)
        acc = d if acc is None else acc + d
    o_ref[...] += acc


def matmul_residual(pairs, x, *, n_k=1, emit_w=False, tm_pref=1024, tn_pref=1024):
    m, n = x.shape
    tm = _tile(m, tm_pref, SUBLANE)
    tn = _tile(n, tn_pref, 2 * LANE)
    assert not emit_w or m == tm
    out_specs = [pl.BlockSpec((tm, tn), lambda i, j, k: (i, j))]
    out_shape = [jax.ShapeDtypeStruct((m, n), F32)]
    in_specs, args = [], []
    for a, w in pairs:
        kp = a.shape[1]
        assert kp % n_k == 0 and (w.rows, w.cols) == (kp, n)
        tk = kp // n_k
        in_specs.append(pl.BlockSpec((tm, tk), lambda i, j, k: (i, k)))
        in_specs.append(_weight_spec(w, tk, tn, lambda i, j, k: k, lambda i, j, k: j))
        args += [a, w.array]
        if emit_w:
            out_specs.append(pl.BlockSpec((tk, tn), lambda i, j, k: (k, j)))
            out_shape.append(jax.ShapeDtypeStruct((kp, n), BF16))
    in_specs.append(pl.BlockSpec((tm, tn), lambda i, j, k: (i, j)))
    args.append(x)
    outs = pl.pallas_call(
        functools.partial(_matmul_residual_kernel, n_pairs=len(pairs), emit_w=emit_w),
        grid=(m // tm, n // tn, n_k),
        in_specs=in_specs,
        out_specs=out_specs,
        out_shape=out_shape,
        compiler_params=_params("parallel", "parallel", "arbitrary"),
        name="matmul_residual",
    )(*args)
    return tuple(outs) if emit_w else outs[0]


def _rglru_kernel(p_ref, h0_ref, cb0_ref, cw_ref, cbias_ref, wgx_ref, bgx_ref, wga_ref, bga_ref, lam_ref,
                  y_ref, hl_ref, cbo_ref, ext_ref, hc_ref, *, tr, tile_last, r_last):
    t = pl.program_id(1)
    halo = SUBLANE

    @pl.when(t == 0)
    def _():
        ext_ref[0:halo, :] = jnp.zeros((halo, D_RNN), F32)
        ext_ref[halo - (RG_CONV - 1):halo, :] = cb0_ref[0]
        hc_ref[...] = h0_ref[0]

    ext_ref[halo:halo + tr, :] = p_ref[0, :, 0:D_RNN].astype(F32)
    sub_row = lax.broadcasted_iota(jnp.int32, (tr // SUBLANE, SUBLANE, HEAD), 1)
    for h in range(RG_HEADS):
        cs = slice(h * HEAD, (h + 1) * HEAD)
        xc = cbias_ref[:, cs] + ext_ref[halo - 3:halo - 3 + tr, cs] * cw_ref[0:1, cs]
        for j in range(1, RG_CONV):
            xc = xc + ext_ref[halo - 3 + j:halo - 3 + j + tr, cs] * cw_ref[j:j + 1, cs]
        xcb = xc.astype(BF16)
        gx = jax.nn.sigmoid(jnp.dot(xcb, wgx_ref[h].astype(BF16), preferred_element_type=F32) + bgx_ref[:, cs])
        ga = jax.nn.sigmoid(jnp.dot(xcb, wga_ref[h].astype(BF16), preferred_element_type=F32) + bga_ref[:, cs])
        log_a = (-RG_C) * ga * _softplus(-lam_ref[:, cs])
        a = jnp.exp(log_a)
        u = jnp.sqrt(-jnp.tanh(log_a) * (1.0 + a * a)) * (gx * xc)
        groups = tr // SUBLANE
        a3 = a.reshape(groups, SUBLANE, HEAD)
        u3 = u.reshape(groups, SUBLANE, HEAD)
        d = 1
        while d < SUBLANE:
            live = sub_row >= d
            u3 = jnp.where(live, a3 * pltpu.roll(u3, d, 1) + u3, u3)
            a3 = jnp.where(live, a3 * pltpu.roll(a3, d, 1), a3)
            d *= 2
        h_prev = jnp.broadcast_to(hc_ref[:, cs], (SUBLANE, HEAD))
        h_groups = []
        for k in range(groups):
            h_k = a3[k] * h_prev + u3[k]
            h_groups.append(h_k)
            h_prev = jnp.broadcast_to(h_k[SUBLANE - 1:SUBLANE, :], (SUBLANE, HEAD))
        u = jnp.concatenate(h_groups, axis=0)
        gate = jax.nn.gelu(p_ref[0, :, D_RNN + h * HEAD:D_RNN + (h + 1) * HEAD].astype(F32))
        y_ref[0, :, cs] = (u * gate).astype(y_ref.dtype)
        hc_ref[:, cs] = u[tr - 1:tr, :]

        @pl.when(t == tile_last)
        def _():
            hl_ref[0, :, cs] = u[r_last:r_last + 1, :]

    @pl.when(t == tile_last)
    def _():
        cbo_ref[0] = ext_ref[halo + r_last - 2:halo + r_last + 1, :]

    if tr >= halo:
        ext_ref[0:halo, :] = ext_ref[tr:tr + halo, :]


def rglru(p, h0, cb0, cw, cbias, wgx, bgx, wga, bga, lam, *, t_valid):
    b, t, _ = p.shape
    tr = _tile(t, 256, 16)
    row = lambda v: v.reshape(1, D_RNN).astype(F32)
    full = lambda shape: pl.BlockSpec(shape, lambda i, j: (0,) * len(shape))
    y, hl, cbo = pl.pallas_call(
        functools.partial(_rglru_kernel, tr=tr, tile_last=(t_valid - 1) // tr, r_last=(t_valid - 1) % tr),
        grid=(b, t // tr),
        in_specs=[
            pl.BlockSpec((1, tr, 2 * D_RNN), lambda i, j: (i, j, 0)),
            pl.BlockSpec((1, 1, D_RNN), lambda i, j: (i, 0, 0)),
            pl.BlockSpec((1, RG_CONV - 1, D_RNN), lambda i, j: (i, 0, 0)),
            full((RG_CONV, D_RNN)), full((1, D_RNN)),
            full((RG_HEADS, HEAD, HEAD)), full((1, D_RNN)),
            full((RG_HEADS, HEAD, HEAD)), full((1, D_RNN)),
            full((1, D_RNN)),
        ],
        out_specs=[
            pl.BlockSpec((1, tr, D_RNN), lambda i, j: (i, j, 0)),
            pl.BlockSpec((1, 1, D_RNN), lambda i, j: (i, 0, 0)),
            pl.BlockSpec((1, RG_CONV - 1, D_RNN), lambda i, j: (i, 0, 0)),
        ],
        out_shape=[
            jax.ShapeDtypeStruct((b, t, D_RNN), BF16),
            jax.ShapeDtypeStruct((b, 1, D_RNN), F32),
            jax.ShapeDtypeStruct((b, RG_CONV - 1, D_RNN), F32),
        ],
        scratch_shapes=[pltpu.VMEM((tr + SUBLANE, D_RNN), F32), pltpu.VMEM((1, D_RNN), F32)],
        compiler_params=_params("parallel", "arbitrary"),
        name="rglru",
    )(p, h0.reshape(b, 1, D_RNN), cb0, cw.astype(F32), row(cbias), wgx, row(bgx), wga, row(bga), row(lam))
    return y, hl.reshape(b, D_RNN), cbo


def _mem_attn_kernel(q_ref, k_ref, v_ref, o_ref):
    for h in range(MEM_HEADS):
        cs = slice(h * HEAD, (h + 1) * HEAD)
        q = q_ref[0, :, cs]
        k = k_ref[0, :, cs].astype(BF16)
        v = v_ref[0, :, cs].astype(BF16)
        s = lax.dot_general(q, k, (((1,), (1,)), ((), ())), preferred_element_type=F32)
        e = jnp.exp(s - jnp.max(s, axis=-1, keepdims=True))
        p = e / jnp.sum(e, axis=-1, keepdims=True)
        o_ref[0, :, cs] = jnp.dot(p.astype(BF16), v, preferred_element_type=F32).astype(o_ref.dtype)


def mem_attn(q, q_col_block, mk, mv):
    b, t, _ = q.shape
    tq = _tile(t, 512, 16)
    return pl.pallas_call(
        _mem_attn_kernel,
        grid=(b, t // tq),
        in_specs=[
            pl.BlockSpec((1, tq, D_MEM), lambda i, j: (i, j, q_col_block)),
            pl.BlockSpec((1, MEM_TOKENS, D_MEM), lambda i, j: (i, 0, 0)),
            pl.BlockSpec((1, MEM_TOKENS, D_MEM), lambda i, j: (i, 0, 0)),
        ],
        out_specs=pl.BlockSpec((1, tq, D_MEM), lambda i, j: (i, j, 0)),
        out_shape=jax.ShapeDtypeStruct((b, t, D_MEM), BF16),
        compiler_params=_params("parallel", "parallel"),
        name="mem_attn",
    )(q, mk, mv)


def _shift_rows(u, shift, head_rows):
    s = pltpu.roll(u, shift, 0)
    row = lax.broadcasted_iota(jnp.int32, (SUBLANE, u.shape[1]), 0)
    top = s[0:SUBLANE]
    for r, v in enumerate(head_rows):
        top = jnp.where(row == r, v, top)
    return jnp.concatenate([top, s[SUBLANE:]], axis=0)


def _conv_gelu_gate(u, gate, prev2, prev1, cw_ref, cb_ref):
    uc = (cb_ref[...] + _shift_rows(u, 2, [prev2, prev1]) * cw_ref[0:1, :]
          + _shift_rows(u, 1, [prev1]) * cw_ref[1:2, :] + u * cw_ref[2:3, :])
    return jax.nn.gelu(uc) * gate


def _ffn_up_act_kernel(x_ref, g_ref, wu_in_ref, wg_in_ref, b0_ref, cw_ref, cb_ref, act_ref, nb_ref, *rest,
                       seq, tiles_per_seq, t_valid, cast_w):
    i = pl.program_id(0)
    j = pl.program_id(1)
    tm = x_ref.shape[0]
    if cast_w:
        wu_ref, wg_ref, xn_ref, tail_ref = rest
        wu_ref[...] = wu_in_ref[...].astype(BF16)
        wg_ref[...] = wg_in_ref[...].astype(BF16)
    else:
        xn_ref, tail_ref = rest
        wu_ref, wg_ref = wu_in_ref, wg_in_ref

    @pl.when(j == 0)
    def _():
        x = x_ref[...]
        ms = jnp.mean(x * x, axis=-1, keepdims=True)
        xn_ref[...] = (x * lax.rsqrt(ms + EPS) * g_ref[...]).astype(BF16)

    if tiles_per_seq >= 1:
        @pl.when(i == 0)
        def _():
            tail_ref[j] = jnp.zeros(tail_ref.shape[1:], F32)

        first = (i % tiles_per_seq) == 0
        prev2 = jnp.where(first, b0_ref[0, 0:1, :], tail_ref[j, SUBLANE - 2:SUBLANE - 1, :])
        prev1 = jnp.where(first, b0_ref[0, 1:2, :], tail_ref[j, SUBLANE - 1:SUBLANE, :])
        rc = min(tm, FFN_ROW_CHUNK)
        for c in range(tm // rc):
            xs = xn_ref[c * rc:(c + 1) * rc, :]
            u = jnp.dot(xs, wu_ref[...], preferred_element_type=F32)
            gate = jnp.dot(xs, wg_ref[...], preferred_element_type=F32)
            act_ref[c * rc:(c + 1) * rc, :] = _conv_gelu_gate(u, gate, prev2, prev1, cw_ref, cb_ref
                                                              ).astype(act_ref.dtype)
            prev2, prev1 = u[rc - 2:rc - 1], u[rc - 1:rc]
        tail_ref[j] = u[rc - SUBLANE:rc]

        @pl.when((i % tiles_per_seq) == tiles_per_seq - 1)
        def _():
            nb_ref[i // tiles_per_seq, j] = u[rc - 2:rc]
    else:
        xn = xn_ref[...]
        u = jnp.dot(xn, wu_ref[...], preferred_element_type=F32)
        gate = jnp.dot(xn, wg_ref[...], preferred_element_type=F32)
        for b in range(tm // seq):
            rs = slice(b * seq, (b + 1) * seq)
            a = _conv_gelu_gate(u[rs], gate[rs], b0_ref[b, 0:1, :], b0_ref[b, 1:2, :], cw_ref, cb_ref)
            act_ref[rs, :] = a.astype(act_ref.dtype)
            nb_ref[(tm // seq) * i + b, j] = u[b * seq + t_valid - 2:b * seq + t_valid]


def ffn_up_act(x, g, w_u, w_g, buf0, cw, cb, *, seq, t_valid, emit_w=False, tm_pref=1024):
    m, d = x.shape
    assert (w_u.rows, w_u.cols) == (d, D_FF) and (w_g.rows, w_g.cols) == (d, D_FF)
    b = m // seq
    tm = _tile(m, tm_pref, SUBLANE)
    assert t_valid >= FFN_CONV - 1 and (seq % tm == 0 or tm % seq == 0)
    tiles_per_seq = seq // tm if tm <= seq else 0
    assert tiles_per_seq == 0 or t_valid == seq
    seqs_per_tile = max(tm // seq, 1)
    cast_w = w_u.array.dtype != BF16
    tn = 512 if cast_w else 1024
    n_j = D_FF // tn
    assert not emit_w or (cast_w and m == tm)
    out_specs = [
        pl.BlockSpec((tm, tn), lambda i, j: (i, j)),
        pl.BlockSpec((b, n_j, FFN_CONV - 1, tn), lambda i, j: (0, 0, 0, 0)),
    ]
    out_shape = [
        jax.ShapeDtypeStruct((m, D_FF), BF16),
        jax.ShapeDtypeStruct((b, n_j, FFN_CONV - 1, tn), F32),
    ]
    scratch = [pltpu.VMEM((tm, d), BF16), pltpu.VMEM((n_j, SUBLANE, tn), F32)]
    if emit_w:
        out_specs += [pl.BlockSpec((d, tn), lambda i, j: (0, j)), pl.BlockSpec((d, tn), lambda i, j: (0, j))]
        out_shape += [jax.ShapeDtypeStruct((d, D_FF), BF16), jax.ShapeDtypeStruct((d, D_FF), BF16)]
    elif cast_w:
        scratch = [pltpu.VMEM((d, tn), BF16), pltpu.VMEM((d, tn), BF16)] + scratch
    outs = pl.pallas_call(
        functools.partial(_ffn_up_act_kernel, seq=seq, tiles_per_seq=tiles_per_seq, t_valid=t_valid,
                          cast_w=cast_w),
        grid=(m // tm, n_j),
        in_specs=[
            pl.BlockSpec((tm, d), lambda i, j: (i, 0)),
            pl.BlockSpec((1, d), lambda i, j: (0, 0)),
            _weight_spec(w_u, d, tn, lambda i, j: 0, lambda i, j: j),
            _weight_spec(w_g, d, tn, lambda i, j: 0, lambda i, j: j),
            pl.BlockSpec((seqs_per_tile, FFN_CONV - 1, tn),
                         lambda i, j: ((i // tiles_per_seq) if tiles_per_seq else i, 0, j)),
            pl.BlockSpec((FFN_CONV, tn), lambda i, j: (0, j)),
            pl.BlockSpec((1, tn), lambda i, j: (0, j)),
        ],
        out_specs=out_specs,
        out_shape=out_shape,
        scratch_shapes=scratch,
        compiler_params=_params("arbitrary", "arbitrary"),
        name="ffn_up_act",
    )(x, g.reshape(1, d).astype(F32), w_u.array, w_g.array, buf0, cw.astype(F32),
      cb.reshape(1, D_FF).astype(F32))
    nb = outs[1].transpose(0, 2, 1, 3).reshape(b, FFN_CONV - 1, D_FF)
    return (outs[0], nb) + tuple(outs[2:])


def _suffix_matrix():
    j = lax.broadcasted_iota(jnp.int32, (2 * HEAD, 2 * HEAD), 0) % HEAD
    s = lax.broadcasted_iota(jnp.int32, (2 * HEAD, 2 * HEAD), 1)
    return jnp.where((s >= HEAD) | (j >= s), 1.0, 0.0).astype(BF16)


def _sb_weights(s, neg_bias, carry, suffix, mask):
    nz, lk = _sb_log_keep(s, neg_bias, mask)
    return _sb_finish(nz, _sb_suffix_sums(lk, suffix), carry, mask)


def _sb_log_keep(s, neg_bias, mask):
    nz = neg_bias - s
    neg_abs = pltpu.bitcast(pltpu.bitcast(nz, jnp.uint32) | jnp.uint32(0x80000000), F32)
    lk = jnp.minimum(nz, 0.0) - jnp.log2(1.0 + jnp.exp2(neg_abs))
    if mask is not None:
        lk = jnp.where(mask, lk, 0.0)
    return nz, lk


def _sb_suffix_sums(lk, suffix):
    hi = lk.astype(BF16)
    lo = (lk - hi.astype(F32)).astype(BF16)
    return jnp.dot(jnp.concatenate([hi, lo], axis=1), suffix, preferred_element_type=F32)


def _sb_finish(nz, c2, carry, mask):
    w = jnp.exp2((carry + c2[:, :HEAD]) - nz)
    if mask is not None:
        w = jnp.where(mask, w, 0.0)
    return w, carry + c2[:, HEAD:]


def _nt_dot(a, b):
    return lax.dot_general(a, b, (((1,), (1,)), ((), ())), preferred_element_type=F32)


def _sb_prompt_kernel(nbias_ref, q_ref, k_ref, v_ref, o_ref, carry_ref, acc_ref, *, tq, hp):
    hg = pl.program_id(1)
    qi = pl.program_id(2)
    suffix = _suffix_matrix()
    n_sub = tq // HEAD
    carry_ref[...] = jnp.zeros(carry_ref.shape, F32)
    acc_ref[...] = jnp.zeros(acc_ref.shape, F32)

    def block(start, masked):
        for e in range(hp):
            neg_bias = nbias_ref[hg * hp + e]
            s = _nt_dot(q_ref[0, :, e * HEAD:(e + 1) * HEAD], k_ref[0, e, pl.ds(start, tq), :])
            carry = carry_ref[e]
            ws = [None] * n_sub
            for j in range(n_sub - 1, -1, -1):
                if masked:
                    r0 = j * HEAD
                    mask = (lax.broadcasted_iota(jnp.int32, (tq - r0, HEAD), 1)
                            < lax.broadcasted_iota(jnp.int32, (tq - r0, HEAD), 0))
                    w, c = _sb_weights(s[r0:, j * HEAD:(j + 1) * HEAD], neg_bias, carry[r0:], suffix, mask)
                    if r0:
                        w = jnp.concatenate([jnp.zeros((r0, HEAD), F32), w], axis=0)
                        c = jnp.concatenate([carry[:r0], c], axis=0)
                    carry = c
                else:
                    w, carry = _sb_weights(s[:, j * HEAD:(j + 1) * HEAD], neg_bias, carry, suffix, None)
                ws[j] = w.astype(BF16)
            carry_ref[e] = carry
            acc_ref[e] += jnp.dot(jnp.concatenate(ws, axis=1), v_ref[0, e, pl.ds(start, tq), :],
                                  preferred_element_type=F32)

    block(pl.multiple_of(qi * tq, tq), True)

    odd = qi % 2

    @pl.when(odd == 1)
    def _():
        block(pl.multiple_of((qi - 1) * tq, tq), False)

    @pl.loop(0, qi // 2)
    def _(i):
        later = qi - odd - 1 - 2 * i
        block(pl.multiple_of(later * tq, tq), False)
        block(pl.multiple_of((later - 1) * tq, tq), False)

    for e in range(hp):
        o_ref[0, :, e * HEAD:(e + 1) * HEAD] = acc_ref[e].astype(o_ref.dtype)


def sb_prompt(q, k, v, bias):
    b, _, t, _ = k.shape
    tq = _tile(t, 512, HEAD)
    hp = SB_HEADS_PER_STEP
    return pl.pallas_call(
        functools.partial(_sb_prompt_kernel, tq=tq, hp=hp),
        grid_spec=pltpu.PrefetchScalarGridSpec(
            num_scalar_prefetch=1,
            grid=(b, SB_HEADS // hp, t // tq),
            in_specs=[
                pl.BlockSpec((1, tq, hp * HEAD), lambda i, h, j, nb: (i, j, h)),
                pl.BlockSpec((1, hp, t, HEAD), lambda i, h, j, nb: (i, h, 0, 0)),
                pl.BlockSpec((1, hp, t, HEAD), lambda i, h, j, nb: (i, h, 0, 0)),
            ],
            out_specs=pl.BlockSpec((1, tq, hp * HEAD), lambda i, h, j, nb: (i, j, h)),
            scratch_shapes=[pltpu.VMEM((hp, tq, HEAD), F32), pltpu.VMEM((hp, tq, HEAD), F32)],
        ),
        out_shape=jax.ShapeDtypeStruct((b, t, D_RNN), BF16),
        compiler_params=_params("parallel", "parallel", "arbitrary"),
        name="sb_prompt",
    )(-LOG2E * bias.astype(F32), q, k, v)


def _sb_sample_kernel(pt_ref, q_ref, kn_ref, vn_ref, bias_ref, *rest, pages_per_step, t_valid):
    kp_refs = rest[:pages_per_step]
    vp_refs = rest[pages_per_step:2 * pages_per_step]
    o_ref, carry_ref, acc_ref = rest[2 * pages_per_step:]
    g = pl.program_id(1)
    rows = SB_HEADS * Q_ROWS
    suffix = _suffix_matrix()
    neg_bias = bias_ref[...]
    hcols = [slice(h * HEAD, (h + 1) * HEAD) for h in range(SB_HEADS)]
    q_heads = [q_ref[0, :, hcols[h]] for h in range(SB_HEADS)]

    def update(blocks, mask):
        n = len(blocks)
        s_heads = [_nt_dot(q_heads[h], jnp.concatenate([blocks[r][0][h] for r in range(n)], axis=0))[0:Q_ROWS]
                   for h in range(SB_HEADS)]
        nzs, lks = [], []
        for r in range(n):
            s = jnp.concatenate([sh[:, r * HEAD:(r + 1) * HEAD] for sh in s_heads], axis=0)
            nz, lk = _sb_log_keep(s, neg_bias, mask)
            nzs.append(nz)
            lks.append(lk)
        c2 = _sb_suffix_sums(jnp.concatenate(lks, axis=0), suffix)
        carry = carry_ref[...]
        ws = []
        for r in range(n):
            w, carry = _sb_finish(nzs[r], c2[r * rows:(r + 1) * rows], carry, mask)
            ws.append(w)
        carry_ref[...] = carry
        zpad = jnp.zeros((SAMPLE_T_PAD - Q_ROWS, n * HEAD), F32)
        for h in range(SB_HEADS):
            rs = slice(h * Q_ROWS, (h + 1) * Q_ROWS)
            wh = jnp.concatenate([jnp.concatenate([ws[r][rs] for r in range(n)], axis=1), zpad], axis=0)
            vh = jnp.concatenate([blocks[r][1][h] for r in range(n)], axis=0)
            acc_ref[rs, :] += jnp.dot(wh.astype(BF16), vh, preferred_element_type=F32)[0:Q_ROWS]

    @pl.when(g == 0)
    def _():
        carry_ref[...] = jnp.zeros((rows, HEAD), F32)
        acc_ref[...] = jnp.zeros((rows, HEAD), F32)
        kz = jnp.zeros((HEAD - SAMPLE_T_PAD, HEAD), BF16)
        k_heads = [jnp.concatenate([kn_ref[0, :, hcols[h]], kz], axis=0) for h in range(SB_HEADS)]
        v_heads = [jnp.concatenate([vn_ref[0, :, hcols[h]], kz], axis=0) for h in range(SB_HEADS)]
        i_q = lax.broadcasted_iota(jnp.int32, (rows, HEAD), 0) % Q_ROWS
        j_k = lax.broadcasted_iota(jnp.int32, (rows, HEAD), 1)
        update([(k_heads, v_heads)], (j_k < i_q) & (j_k < t_valid))

    update([([kp_refs[r][0, h].astype(BF16) for h in range(SB_HEADS)],
             [vp_refs[r][0, h].astype(BF16) for h in range(SB_HEADS)]) for r in range(pages_per_step)], None)

    @pl.when(g == pl.num_programs(1) - 1)
    def _():
        z8 = jnp.zeros((SAMPLE_T_PAD - Q_ROWS, HEAD), F32)
        for h in range(SB_HEADS):
            blk = jnp.concatenate([acc_ref[h * Q_ROWS:(h + 1) * Q_ROWS, :], z8], axis=0)
            o_ref[0, :, h * HEAD:(h + 1) * HEAD] = blk.astype(o_ref.dtype)


def sb_sample(q, k_new, v_new, cache_k, cache_v, page_table, bias, *, t_valid, pages_per_step=8):
    b, n_pages = page_table.shape
    page = cache_k.shape[2]
    assert page == HEAD and n_pages % pages_per_step == 0 and t_valid <= Q_ROWS
    assert cache_k.shape[1:] == (SB_HEADS, page, HEAD)
    rows = SB_HEADS * Q_ROWS
    bias_rows = jnp.broadcast_to(jnp.repeat(-LOG2E * bias.astype(F32), Q_ROWS)[:, None], (rows, HEAD))

    def page_spec(r):
        return pl.BlockSpec(
            (1, SB_HEADS, page, HEAD),
            lambda i, g, pt: (pt[i, n_pages - 1 - (g * pages_per_step + r)], 0, 0, 0))

    return pl.pallas_call(
        functools.partial(_sb_sample_kernel, pages_per_step=pages_per_step, t_valid=t_valid),
        grid_spec=pltpu.PrefetchScalarGridSpec(
            num_scalar_prefetch=1,
            grid=(b, n_pages // pages_per_step),
            in_specs=[
                pl.BlockSpec((1, SAMPLE_T_PAD, D_RNN), lambda i, g, pt: (i, 0, 0)),
                pl.BlockSpec((1, SAMPLE_T_PAD, D_RNN), lambda i, g, pt: (i, 0, 0)),
                pl.BlockSpec((1, SAMPLE_T_PAD, D_RNN), lambda i, g, pt: (i, 0, 0)),
                pl.BlockSpec((rows, HEAD), lambda i, g, pt: (0, 0)),
            ] + [page_spec(r) for r in range(pages_per_step)] + [page_spec(r) for r in range(pages_per_step)],
            out_specs=pl.BlockSpec((1, SAMPLE_T_PAD, D_RNN), lambda i, g, pt: (i, 0, 0)),
            scratch_shapes=[pltpu.VMEM((rows, HEAD), F32), pltpu.VMEM((rows, HEAD), F32)],
        ),
        out_shape=jax.ShapeDtypeStruct((b, SAMPLE_T_PAD, D_RNN), BF16),
        compiler_params=_params("parallel", "arbitrary"),
        name="sb_sample",
    )(page_table, q, k_new, v_new, bias_rows, *([cache_k] * pages_per_step), *([cache_v] * pages_per_step))


def _trunk(x3, t_valid, rg_h0, rg_conv0, ffn_conv0, mem_k, mem_v, sb_attend, head_major_kv, P, W, emit_w):
    b, t, d = x3.shape
    m = b * t
    x = x3.reshape(m, d)
    rg_h_out, rg_conv_out, ffn_conv_out = [], [], []
    k_f32 = v_f32 = k_bf = v_bf = None
    emitted = {}

    def nmm(key, x, g, **kw):
        if not emit_w:
            kw.setdefault('tn_pref', 2048)
        out = norm_matmul(x, g, W[key], emit_w=emit_w, **kw)
        if not emit_w:
            return out
        *out, emitted[key] = out
        return out[0] if len(out) == 1 else tuple(out)

    def mmr(keys, acts, x, **kw):
        out = matmul_residual([(a, W[k]) for a, k in zip(acts, keys)], x, emit_w=emit_w, **kw)
        if not emit_w:
            return out
        for k, wb in zip(keys, out[1:]):
            emitted[k] = wb
        return out[0]

    for l in range(DEPTH):
        mem_gain = jnp.tile(P['mem_q_norm'][l] * ATTN_SCALE, MEM_HEADS)
        if l < N_A_LAYERS:
            p_rg = nmm(('in_a_rg', l), x, P['g_mix'][l])
            qm = nmm(('in_a_qm', l), x, P['g_mix'][l], head_gain=mem_gain)
            y_tok, h_last, cbuf = rglru(
                p_rg.reshape(b, t, 2 * D_RNN), rg_h0[l], rg_conv0[l], P['rg_conv_w'][l], P['rg_conv_b'][l],
                P['rg_gate_x_w'][l], P['rg_gate_x_b'][l], P['rg_gate_a_w'][l], P['rg_gate_a_b'][l],
                P['rg_lambda'][l], t_valid=t_valid)
            rg_h_out.append(h_last)
            rg_conv_out.append(cbuf)
            q3, q_col = qm.reshape(b, t, D_MEM), 0
        else:
            j = l - N_A_LAYERS
            gain = jnp.concatenate([jnp.tile(P['sb_q_norm'][j] * (ATTN_SCALE * LOG2E), SB_HEADS), mem_gain])
            pq = nmm(('in_b', l), x, P['g_mix'][l], head_gain=gain)
            q3, q_col = pq.reshape(b, t, D_RNN + D_MEM), D_RNN // D_MEM
            y_tok = sb_attend(q3, k_bf, v_bf, P['sb_beta_bias'][j])
        y_mem = mem_attn(q3, q_col, mem_k[l], mem_v[l])
        out_tiles = {} if emit_w else dict(tm_pref=512, tn_pref=2048)
        x = mmr([('out_tok', l), ('out_mem', l)], [y_tok.reshape(m, D_RNN), y_mem.reshape(m, D_MEM)], x,
                **out_tiles)
        act, fbuf, *wbs = ffn_up_act(x, P['g_ffn'][l], W[('ffn_u', l)], W[('ffn_g', l)], ffn_conv0[l],
                                     P['ffn_conv_w'][l], P['ffn_conv_b'][l], seq=t, t_valid=t_valid,
                                     emit_w=emit_w)
        if emit_w:
            emitted[('ffn_u', l)], emitted[('ffn_g', l)] = wbs
        x = mmr([('ffn_down', l)], [act], x, tn_pref=512)
        ffn_conv_out.append(fbuf)
        if l == N_A_LAYERS - 1:
            k_gain = jnp.tile(P['kv_k_norm'], SB_HEADS)
            hm = t if head_major_kv else None
            k_f32, k_bf = nmm(('k', l), x, P['kv_norm'], head_gain=k_gain, out_dtypes=(F32, BF16),
                              head_major_seq=hm)
            v_f32, v_bf = nmm(('v', l), x, P['kv_norm'], out_dtypes=(F32, BF16), head_major_seq=hm)
            if not head_major_kv:
                k_bf = k_bf.reshape(b, t, D_RNN)
                v_bf = v_bf.reshape(b, t, D_RNN)
    if head_major_kv:
        k_out, v_out = k_f32.transpose(0, 2, 1, 3), v_f32.transpose(0, 2, 1, 3)
    else:
        k_out, v_out = k_f32.reshape(b, t, SB_HEADS, HEAD), v_f32.reshape(b, t, SB_HEADS, HEAD)
    return (x.reshape(b, t, d), jnp.stack(rg_h_out), jnp.stack(rg_conv_out), jnp.stack(ffn_conv_out),
            k_out, v_out, emitted)


def kernel(x_prompt, x_sample, mem_prompt, state_rglru_h, state_rglru_conv, state_ffn_conv, cache_mem_k, cache_mem_v, cache_sb_k, cache_sb_v, page_table, g_mix, g_ffn, w_in_a, rg_conv_w, rg_conv_b, rg_gate_x_w, rg_gate_x_b, rg_gate_a_w, rg_gate_a_b, rg_lambda, w_in_b, sb_q_norm, sb_beta_bias, kv_norm, w_kv, kv_k_norm, mem_norm, w_mem_kv, mem_q_norm, mem_k_norm, w_out, w_ffn_up, ffn_conv_w, ffn_conv_b, w_ffn_down):
    P = {'g_mix': g_mix, 'g_ffn': g_ffn, 'rg_conv_w': rg_conv_w, 'rg_conv_b': rg_conv_b,
         'rg_gate_x_b': rg_gate_x_b, 'rg_gate_a_b': rg_gate_a_b, 'rg_lambda': rg_lambda,
         'sb_q_norm': sb_q_norm, 'sb_beta_bias': sb_beta_bias, 'kv_norm': kv_norm,
         'kv_k_norm': kv_k_norm, 'mem_q_norm': mem_q_norm,
         'ffn_conv_w': ffn_conv_w, 'ffn_conv_b': ffn_conv_b,
         'rg_gate_x_w': rg_gate_x_w, 'rg_gate_a_w': rg_gate_a_w}
    dm = D_MODEL
    W = {}
    for l in range(DEPTH):
        if l < N_A_LAYERS:
            W[('in_a_rg', l)] = WeightView(w_in_a, l, 0, dm, 0, 2 * D_RNN)
            W[('in_a_qm', l)] = WeightView(w_in_a, l, 0, dm, 2 * D_RNN, D_MEM)
        else:
            W[('in_b', l)] = WeightView(w_in_b, l - N_A_LAYERS, 0, dm, 0, D_RNN + D_MEM)
        W[('out_tok', l)] = WeightView(w_out, l, 0, D_RNN, 0, dm)
        W[('out_mem', l)] = WeightView(w_out, l, D_RNN, D_MEM, 0, dm)
        W[('ffn_u', l)] = WeightView(w_ffn_up, l, 0, dm, 0, D_FF)
        W[('ffn_g', l)] = WeightView(w_ffn_up, l, 0, dm, D_FF, D_FF)
        W[('ffn_down', l)] = WeightView(w_ffn_down, l, 0, D_FF, 0, dm)
    W[('k', N_A_LAYERS - 1)] = WeightView(w_kv, None, 0, dm, 0, D_RNN)
    W[('v', N_A_LAYERS - 1)] = WeightView(w_kv, None, 0, dm, D_RNN, D_RNN)

    db, dec_seq, _ = x_sample.shape
    xs = jnp.pad(x_sample, ((0, 0), (0, SAMPLE_T_PAD - dec_seq), (0, 0)))

    cache_k_hm = cache_sb_k.transpose(0, 2, 1, 3)
    cache_v_hm = cache_sb_v.transpose(0, 2, 1, 3)

    def sb_paged(q3, k_bf, v_bf, bias):
        return sb_sample(q3, k_bf, v_bf, cache_k_hm, cache_v_hm, page_table, bias, t_valid=dec_seq)

    (y_s, sample_rglru_h, sample_rglru_conv, sample_ffn_conv, s_k, s_v, w_bf16) = _trunk(
        xs, dec_seq, state_rglru_h, state_rglru_conv, state_ffn_conv,
        cache_mem_k.reshape(DEPTH, db, MEM_TOKENS, D_MEM), cache_mem_v.reshape(DEPTH, db, MEM_TOKENS, D_MEM),
        sb_paged, False, P, W, True)
    W_prompt = {key: _bf16_view(arr) for key, arr in w_bf16.items()}

    bp, seq, d = x_prompt.shape
    mem2 = mem_prompt.reshape(bp * MEM_TOKENS, d)
    mk_list, mv_list = [], []
    for l in range(DEPTH):
        mk_list.append(norm_matmul(mem2, mem_norm[l], WeightView(w_mem_kv, l, 0, dm, 0, D_MEM),
                                   head_gain=jnp.tile(mem_k_norm[l], MEM_HEADS), out_dtypes=(F32,)))
        mv_list.append(norm_matmul(mem2, mem_norm[l], WeightView(w_mem_kv, l, 0, dm, D_MEM, D_MEM),
                                   out_dtypes=(F32,)))
    prompt_mem_k = jnp.stack(mk_list).reshape(DEPTH, bp, MEM_TOKENS, D_MEM)
    prompt_mem_v = jnp.stack(mv_list).reshape(DEPTH, bp, MEM_TOKENS, D_MEM)
    zeros_h = jnp.zeros((N_A_LAYERS, bp, D_RNN), F32)
    zeros_rc = jnp.zeros((N_A_LAYERS, bp, RG_CONV - 1, D_RNN), F32)
    zeros_fc = jnp.zeros((DEPTH, bp, FFN_CONV - 1, D_FF), F32)
    (y_prompt, prompt_rglru_h, prompt_rglru_conv, prompt_ffn_conv, prompt_sb_k, prompt_sb_v, _) = _trunk(
        x_prompt, seq, zeros_h, zeros_rc, zeros_fc, prompt_mem_k, prompt_mem_v, sb_prompt, True, P, W_prompt,
        False)

    mem_shape = (DEPTH, bp, MEM_TOKENS, MEM_HEADS, HEAD)
    return (y_prompt, y_s[:, :dec_seq],
            prompt_rglru_h, prompt_rglru_conv, prompt_ffn_conv,
            prompt_sb_k, prompt_sb_v, prompt_mem_k.reshape(mem_shape), prompt_mem_v.reshape(mem_shape),
            sample_rglru_h, sample_rglru_conv, sample_ffn_conv,
            s_k[:, :dec_seq], s_v[:, :dec_seq])
```

```python
import functools
import math
from typing import NamedTuple

import jax
import jax.numpy as jnp
from jax import lax
from jax.experimental import pallas as pl
from jax.experimental.pallas import tpu as pltpu

F32 = jnp.float32
BF16 = jnp.bfloat16

LANE = 128
SUBLANE = 8
VMEM_LIMIT_BYTES = 56 * 1024 * 1024

D_MODEL = 2048
DEPTH = 4
N_A_LAYERS = 2
D_RNN = 1536
HEAD = 128
RG_HEADS = D_RNN // HEAD
RG_CONV = 4
RG_C = 8.0
SB_HEADS = D_RNN // HEAD
MEM_TOKENS = 256
MEM_HEADS = 4
D_MEM = MEM_HEADS * HEAD
D_FF = 3 * D_MODEL
FFN_CONV = 3
EPS = 1e-6
ATTN_SCALE = HEAD ** -0.5
LOG2E = math.log2(math.e)
SAMPLE_T_PAD = 16
Q_ROWS = 8
FFN_ROW_CHUNK = 256
NORM_ROW_CHUNK = 256
SB_HEADS_PER_STEP = 6


def _params(*sem):
    return pltpu.CompilerParams(dimension_semantics=sem, vmem_limit_bytes=VMEM_LIMIT_BYTES)


def _tile(n, pref, mult):
    if n <= pref:
        return n
    t = (pref // mult) * mult
    while t > mult and n % t:
        t -= mult
    assert n % t == 0, (n, pref, mult)
    return t


class WeightView(NamedTuple):
    array: jax.Array
    layer: int | None
    row0: int
    rows: int
    col0: int
    cols: int


def _weight_spec(wv, tk, tn, row_block, col_block):
    assert wv.row0 % tk == 0 and wv.col0 % tn == 0 and wv.rows % tk == 0 and wv.cols % tn == 0
    r0, c0 = wv.row0 // tk, wv.col0 // tn
    if wv.layer is None:
        return pl.BlockSpec((tk, tn), lambda *g: (r0 + row_block(*g), c0 + col_block(*g)))
    layer = wv.layer
    return pl.BlockSpec((None, tk, tn), lambda *g: (layer, r0 + row_block(*g), c0 + col_block(*g)))


def _softplus(z):
    return jnp.maximum(z, 0.0) + jnp.log1p(jnp.exp(-jnp.abs(z)))


def _head_rmsnorm(blk, gain):
    ms = jnp.mean(blk * blk, axis=-1, keepdims=True)
    return blk * lax.rsqrt(ms + EPS) * gain


def _norm_matmul_kernel(*refs, head_norm, head_major, n_out, emit_w):
    x_ref, g_ref, w_ref = refs[:3]
    pos = 3
    hg_ref = None
    if head_norm:
        hg_ref = refs[pos]
        pos += 1
    o_refs = refs[pos:pos + n_out]
    pos += n_out
    wb_ref = None
    if emit_w:
        wb_ref = refs[pos]
        pos += 1
    xn_ref = refs[pos]

    @pl.when(pl.program_id(1) == 0)
    def _():
        x = x_ref[...]
        ms = jnp.mean(x * x, axis=-1, keepdims=True)
        xn_ref[...] = (x * lax.rsqrt(ms + EPS) * g_ref[...]).astype(BF16)

    wb = w_ref[...].astype(BF16)
    if emit_w:
        wb_ref[...] = wb
    tm = xn_ref.shape[0]
    tn = wb.shape[1]
    if head_norm or head_major:
        rc = min(tm, NORM_ROW_CHUNK)
        for c in range(tm // rc):
            rows = slice(c * rc, (c + 1) * rc)
            acc = jnp.dot(xn_ref[rows, :], wb, preferred_element_type=F32)
            for h in range(tn // HEAD):
                cs = slice(h * HEAD, (h + 1) * HEAD)
                y = _head_rmsnorm(acc[:, cs], hg_ref[:, cs]) if head_norm else acc[:, cs]
                for o_ref in o_refs:
                    if head_major:
                        o_ref[0, h, rows, :] = y.astype(o_ref.dtype)
                    else:
                        o_ref[rows, cs] = y.astype(o_ref.dtype)
    else:
        acc = jnp.dot(xn_ref[...], wb, preferred_element_type=F32)
        for o_ref in o_refs:
            o_ref[...] = acc.astype(o_ref.dtype)


def _bf16_view(arr):
    return WeightView(arr, None, 0, arr.shape[0], 0, arr.shape[1])


def norm_matmul(x, g, w, *, head_gain=None, out_dtypes=(BF16,), head_major_seq=None, emit_w=False,
                tm_pref=1024, tn_pref=1024):
    m, d = x.shape
    n = w.cols
    assert w.rows == d
    tm = _tile(m, tm_pref, SUBLANE)
    tn = _tile(n, tn_pref, 2 * LANE)
    assert not emit_w or (m == tm and head_major_seq is None)
    if head_major_seq is None:
        out_specs = [pl.BlockSpec((tm, tn), lambda i, j: (i, j)) for _ in out_dtypes]
        out_shape = [jax.ShapeDtypeStruct((m, n), dt) for dt in out_dtypes]
    else:
        seq = head_major_seq
        assert seq % tm == 0 and m % seq == 0
        per_seq = seq // tm
        out_specs = [pl.BlockSpec((1, tn // HEAD, tm, HEAD), lambda i, j: (i // per_seq, j, i % per_seq, 0))
                     for _ in out_dtypes]
        out_shape = [jax.ShapeDtypeStruct((m // seq, n // HEAD, seq, HEAD), dt) for dt in out_dtypes]
    in_specs = [
        pl.BlockSpec((tm, d), lambda i, j: (i, 0)),
        pl.BlockSpec((1, d), lambda i, j: (0, 0)),
        _weight_spec(w, d, tn, lambda i, j: 0, lambda i, j: j),
    ]
    args = [x, g.reshape(1, d).astype(F32), w.array]
    if head_gain is not None:
        in_specs.append(pl.BlockSpec((1, tn), lambda i, j: (0, j)))
        args.append(head_gain.reshape(1, n).astype(F32))
    if emit_w:
        out_specs.append(pl.BlockSpec((d, tn), lambda i, j: (0, j)))
        out_shape.append(jax.ShapeDtypeStruct((d, n), BF16))
    outs = pl.pallas_call(
        functools.partial(_norm_matmul_kernel, head_norm=head_gain is not None,
                          head_major=head_major_seq is not None, n_out=len(out_dtypes), emit_w=emit_w),
        grid=(m // tm, n // tn),
        in_specs=in_specs,
        out_specs=out_specs,
        out_shape=out_shape,
        scratch_shapes=[pltpu.VMEM((tm, d), BF16)],
        compiler_params=_params("parallel", "arbitrary"),
        name="norm_matmul",
    )(*args)
    return outs[0] if len(outs) == 1 else tuple(outs)


def _matmul_residual_kernel(*refs, n_pairs, emit_w):
    x_ref = refs[2 * n_pairs]
    o_ref = refs[2 * n_pairs + 1]
    wb_refs = refs[2 * n_pairs + 2:]

    @pl.when(pl.program_id(2) == 0)
    def _():
        o_ref[...] = x_ref[...]

    acc = None
    for p in range(n_pairs):
        wb = refs[2 * p + 1][...].astype(BF16)
        if emit_w:
            wb_refs[p][...] = wb
        d = jnp.dot(refs[2 * p][...], wb, preferred_element_type=F32)
        acc = d if acc is None else acc + d
    o_ref[...] += acc


def matmul_residual(pairs, x, *, n_k=1, emit_w=False, tm_pref=1024, tn_pref=1024):
    m, n = x.shape
    tm = _tile(m, tm_pref, SUBLANE)
    tn = _tile(n, tn_pref, 2 * LANE)
    assert not emit_w or m == tm
    out_specs = [pl.BlockSpec((tm, tn), lambda i, j, k: (i, j))]
    out_shape = [jax.ShapeDtypeStruct((m, n), F32)]
    in_specs, args = [], []
    for a, w in pairs:
        kp = a.shape[1]
        assert kp % n_k == 0 and (w.rows, w.cols) == (kp, n)
        tk = kp // n_k
        in_specs.append(pl.BlockSpec((tm, tk), lambda i, j, k: (i, k)))
        in_specs.append(_weight_spec(w, tk, tn, lambda i, j, k: k, lambda i, j, k: j))
        args += [a, w.array]
        if emit_w:
            out_specs.append(pl.BlockSpec((tk, tn), lambda i, j, k: (k, j)))
            out_shape.append(jax.ShapeDtypeStruct((kp, n), BF16))
    in_specs.append(pl.BlockSpec((tm, tn), lambda i, j, k: (i, j)))
    args.append(x)
    outs = pl.pallas_call(
        functools.partial(_matmul_residual_kernel, n_pairs=len(pairs), emit_w=emit_w),
        grid=(m // tm, n // tn, n_k),
        in_specs=in_specs,
        out_specs=out_specs,
        out_shape=out_shape,
        compiler_params=_params("parallel", "parallel", "arbitrary"),
        name="matmul_residual",
    )(*args)
    return tuple(outs) if emit_w else outs[0]


def _rglru_kernel(p_ref, h0_ref, cb0_ref, cw_ref, cbias_ref, wgx_ref, bgx_ref, wga_ref, bga_ref, lam_ref,
                  y_ref, hl_ref, cbo_ref, ext_ref, hc_ref, *, tr, tile_last, r_last):
    t = pl.program_id(1)
    halo = SUBLANE

    @pl.when(t == 0)
    def _():
        ext_ref[0:halo, :] = jnp.zeros((halo, D_RNN), F32)
        ext_ref[halo - (RG_CONV - 1):halo, :] = cb0_ref[0]
        hc_ref[...] = h0_ref[0]

    ext_ref[halo:halo + tr, :] = p_ref[0, :, 0:D_RNN].astype(F32)
    sub_row = lax.broadcasted_iota(jnp.int32, (tr // SUBLANE, SUBLANE, HEAD), 1)
    for h in range(RG_HEADS):
        cs = slice(h * HEAD, (h + 1) * HEAD)
        xc = cbias_ref[:, cs] + ext_ref[halo - 3:halo - 3 + tr, cs] * cw_ref[0:1, cs]
        for j in range(1, RG_CONV):
            xc = xc + ext_ref[halo - 3 + j:halo - 3 + j + tr, cs] * cw_ref[j:j + 1, cs]
        xcb = xc.astype(BF16)
        gx = jax.nn.sigmoid(jnp.dot(xcb, wgx_ref[h].astype(BF16), preferred_element_type=F32) + bgx_ref[:, cs])
        ga = jax.nn.sigmoid(jnp.dot(xcb, wga_ref[h].astype(BF16), preferred_element_type=F32) + bga_ref[:, cs])
        log_a = (-RG_C) * ga * _softplus(-lam_ref[:, cs])
        a = jnp.exp(log_a)
        u = jnp.sqrt(-jnp.tanh(log_a) * (1.0 + a * a)) * (gx * xc)
        groups = tr // SUBLANE
        a3 = a.reshape(groups, SUBLANE, HEAD)
        u3 = u.reshape(groups, SUBLANE, HEAD)
        d = 1
        while d < SUBLANE:
            live = sub_row >= d
            u3 = jnp.where(live, a3 * pltpu.roll(u3, d, 1) + u3, u3)
            a3 = jnp.where(live, a3 * pltpu.roll(a3, d, 1), a3)
            d *= 2
        h_prev = jnp.broadcast_to(hc_ref[:, cs], (SUBLANE, HEAD))
        h_groups = []
        for k in range(groups):
            h_k = a3[k] * h_prev + u3[k]
            h_groups.append(h_k)
            h_prev = jnp.broadcast_to(h_k[SUBLANE - 1:SUBLANE, :], (SUBLANE, HEAD))
        u = jnp.concatenate(h_groups, axis=0)
        gate = jax.nn.gelu(p_ref[0, :, D_RNN + h * HEAD:D_RNN + (h + 1) * HEAD].astype(F32))
        y_ref[0, :, cs] = (u * gate).astype(y_ref.dtype)
        hc_ref[:, cs] = u[tr - 1:tr, :]

        @pl.when(t == tile_last)
        def _():
            hl_ref[0, :, cs] = u[r_last:r_last + 1, :]

    @pl.when(t == tile_last)
    def _():
        cbo_ref[0] = ext_ref[halo + r_last - 2:halo + r_last + 1, :]

    if tr >= halo:
        ext_ref[0:halo, :] = ext_ref[tr:tr + halo, :]


def rglru(p, h0, cb0, cw, cbias, wgx, bgx, wga, bga, lam, *, t_valid):
    b, t, _ = p.shape
    tr = _tile(t, 256, 16)
    row = lambda v: v.reshape(1, D_RNN).astype(F32)
    full = lambda shape: pl.BlockSpec(shape, lambda i, j: (0,) * len(shape))
    y, hl, cbo = pl.pallas_call(
        functools.partial(_rglru_kernel, tr=tr, tile_last=(t_valid - 1) // tr, r_last=(t_valid - 1) % tr),
        grid=(b, t // tr),
        in_specs=[
            pl.BlockSpec((1, tr, 2 * D_RNN), lambda i, j: (i, j, 0)),
            pl.BlockSpec((1, 1, D_RNN), lambda i, j: (i, 0, 0)),
            pl.BlockSpec((1, RG_CONV - 1, D_RNN), lambda i, j: (i, 0, 0)),
            full((RG_CONV, D_RNN)), full((1, D_RNN)),
            full((RG_HEADS, HEAD, HEAD)), full((1, D_RNN)),
            full((RG_HEADS, HEAD, HEAD)), full((1, D_RNN)),
            full((1, D_RNN)),
        ],
        out_specs=[
            pl.BlockSpec((1, tr, D_RNN), lambda i, j: (i, j, 0)),
            pl.BlockSpec((1, 1, D_RNN), lambda i, j: (i, 0, 0)),
            pl.BlockSpec((1, RG_CONV - 1, D_RNN), lambda i, j: (i, 0, 0)),
        ],
        out_shape=[
            jax.ShapeDtypeStruct((b, t, D_RNN), BF16),
            jax.ShapeDtypeStruct((b, 1, D_RNN), F32),
            jax.ShapeDtypeStruct((b, RG_CONV - 1, D_RNN), F32),
        ],
        scratch_shapes=[pltpu.VMEM((tr + SUBLANE, D_RNN), F32), pltpu.VMEM((1, D_RNN), F32)],
        compiler_params=_params("parallel", "arbitrary"),
        name="rglru",
    )(p, h0.reshape(b, 1, D_RNN), cb0, cw.astype(F32), row(cbias), wgx, row(bgx), wga, row(bga), row(lam))
    return y, hl.reshape(b, D_RNN), cbo


def _mem_attn_kernel(q_ref, k_ref, v_ref, o_ref):
    for h in range(MEM_HEADS):
        cs = slice(h * HEAD, (h + 1) * HEAD)
        q = q_ref[0, :, cs]
        k = k_ref[0, :, cs].astype(BF16)
        v = v_ref[0, :, cs].astype(BF16)
        s = lax.dot_general(q, k, (((1,), (1,)), ((), ())), preferred_element_type=F32)
        e = jnp.exp(s - jnp.max(s, axis=-1, keepdims=True))
        p = e / jnp.sum(e, axis=-1, keepdims=True)
        o_ref[0, :, cs] = jnp.dot(p.astype(BF16), v, preferred_element_type=F32).astype(o_ref.dtype)


def mem_attn(q, q_col_block, mk, mv):
    b, t, _ = q.shape
    tq = _tile(t, 2048, 16)
    return pl.pallas_call(
        _mem_attn_kernel,
        grid=(b, t // tq),
        in_specs=[
            pl.BlockSpec((1, tq, D_MEM), lambda i, j: (i, j, q_col_block)),
            pl.BlockSpec((1, MEM_TOKENS, D_MEM), lambda i, j: (i, 0, 0)),
            pl.BlockSpec((1, MEM_TOKENS, D_MEM), lambda i, j: (i, 0, 0)),
        ],
        out_specs=pl.BlockSpec((1, tq, D_MEM), lambda i, j: (i, j, 0)),
        out_shape=jax.ShapeDtypeStruct((b, t, D_MEM), BF16),
        compiler_params=_params("parallel", "parallel"),
        name="mem_attn",
    )(q, mk, mv)


def _shift_rows(u, shift, head_rows):
    s = pltpu.roll(u, shift, 0)
    row = lax.broadcasted_iota(jnp.int32, (SUBLANE, u.shape[1]), 0)
    top = s[0:SUBLANE]
    for r, v in enumerate(head_rows):
        top = jnp.where(row == r, v, top)
    return jnp.concatenate([top, s[SUBLANE:]], axis=0)


def _conv_gelu_gate(u, gate, prev2, prev1, cw_ref, cb_ref):
    uc = (cb_ref[...] + _shift_rows(u, 2, [prev2, prev1]) * cw_ref[0:1, :]
          + _shift_rows(u, 1, [prev1]) * cw_ref[1:2, :] + u * cw_ref[2:3, :])
    return jax.nn.gelu(uc) * gate


def _ffn_up_act_kernel(x_ref, g_ref, wu_in_ref, wg_in_ref, b0_ref, cw_ref, cb_ref, act_ref, nb_ref, *rest,
                       seq, tiles_per_seq, t_valid, cast_w):
    i = pl.program_id(0)
    j = pl.program_id(1)
    tm = x_ref.shape[0]
    if cast_w:
        wu_ref, wg_ref, xn_ref, tail_ref = rest
        wu_ref[...] = wu_in_ref[...].astype(BF16)
        wg_ref[...] = wg_in_ref[...].astype(BF16)
    else:
        xn_ref, tail_ref = rest
        wu_ref, wg_ref = wu_in_ref, wg_in_ref

    @pl.when(j == 0)
    def _():
        x = x_ref[...]
        ms = jnp.mean(x * x, axis=-1, keepdims=True)
        xn_ref[...] = (x * lax.rsqrt(ms + EPS) * g_ref[...]).astype(BF16)

    if tiles_per_seq >= 1:
        @pl.when(i == 0)
        def _():
            tail_ref[j] = jnp.zeros(tail_ref.shape[1:], F32)

        first = (i % tiles_per_seq) == 0
        prev2 = jnp.where(first, b0_ref[0, 0:1, :], tail_ref[j, SUBLANE - 2:SUBLANE - 1, :])
        prev1 = jnp.where(first, b0_ref[0, 1:2, :], tail_ref[j, SUBLANE - 1:SUBLANE, :])
        rc = min(tm, FFN_ROW_CHUNK)
        for c in range(tm // rc):
            xs = xn_ref[c * rc:(c + 1) * rc, :]
            u = jnp.dot(xs, wu_ref[...], preferred_element_type=F32)
            gate = jnp.dot(xs, wg_ref[...], preferred_element_type=F32)
            act_ref[c * rc:(c + 1) * rc, :] = _conv_gelu_gate(u, gate, prev2, prev1, cw_ref, cb_ref
                                                              ).astype(act_ref.dtype)
            prev2, prev1 = u[rc - 2:rc - 1], u[rc - 1:rc]
        tail_ref[j] = u[rc - SUBLANE:rc]

        @pl.when((i % tiles_per_seq) == tiles_per_seq - 1)
        def _():
            nb_ref[i // tiles_per_seq, j] = u[rc - 2:rc]
    else:
        xn = xn_ref[...]
        u = jnp.dot(xn, wu_ref[...], preferred_element_type=F32)
        gate = jnp.dot(xn, wg_ref[...], preferred_element_type=F32)
        for b in range(tm // seq):
            rs = slice(b * seq, (b + 1) * seq)
            a = _conv_gelu_gate(u[rs], gate[rs], b0_ref[b, 0:1, :], b0_ref[b, 1:2, :], cw_ref, cb_ref)
            act_ref[rs, :] = a.astype(act_ref.dtype)
            nb_ref[(tm // seq) * i + b, j] = u[b * seq + t_valid - 2:b * seq + t_valid]


def ffn_up_act(x, g, w_u, w_g, buf0, cw, cb, *, seq, t_valid, emit_w=False, tm_pref=1024):
    m, d = x.shape
    assert (w_u.rows, w_u.cols) == (d, D_FF) and (w_g.rows, w_g.cols) == (d, D_FF)
    b = m // seq
    tm = _tile(m, tm_pref, SUBLANE)
    assert t_valid >= FFN_CONV - 1 and (seq % tm == 0 or tm % seq == 0)
    tiles_per_seq = seq // tm if tm <= seq else 0
    assert tiles_per_seq == 0 or t_valid == seq
    seqs_per_tile = max(tm // seq, 1)
    cast_w = w_u.array.dtype != BF16
    tn = 512 if cast_w else 1536
    n_j = D_FF // tn
    assert not emit_w or (cast_w and m == tm)
    out_specs = [
        pl.BlockSpec((tm, tn), lambda i, j: (i, j)),
        pl.BlockSpec((b, n_j, FFN_CONV - 1, tn), lambda i, j: (0, 0, 0, 0)),
    ]
    out_shape = [
        jax.ShapeDtypeStruct((m, D_FF), BF16),
        jax.ShapeDtypeStruct((b, n_j, FFN_CONV - 1, tn), F32),
    ]
    scratch = [pltpu.VMEM((tm, d), BF16), pltpu.VMEM((n_j, SUBLANE, tn), F32)]
    if emit_w:
        out_specs += [pl.BlockSpec((d, tn), lambda i, j: (0, j)), pl.BlockSpec((d, tn), lambda i, j: (0, j))]
        out_shape += [jax.ShapeDtypeStruct((d, D_FF), BF16), jax.ShapeDtypeStruct((d, D_FF), BF16)]
    elif cast_w:
        scratch = [pltpu.VMEM((d, tn), BF16), pltpu.VMEM((d, tn), BF16)] + scratch
    outs = pl.pallas_call(
        functools.partial(_ffn_up_act_kernel, seq=seq, tiles_per_seq=tiles_per_seq, t_valid=t_valid,
                          cast_w=cast_w),
        grid=(m // tm, n_j),
        in_specs=[
            pl.BlockSpec((tm, d), lambda i, j: (i, 0)),
            pl.BlockSpec((1, d), lambda i, j: (0, 0)),
            _weight_spec(w_u, d, tn, lambda i, j: 0, lambda i, j: j),
            _weight_spec(w_g, d, tn, lambda i, j: 0, lambda i, j: j),
            pl.BlockSpec((seqs_per_tile, FFN_CONV - 1, tn),
                         lambda i, j: ((i // tiles_per_seq) if tiles_per_seq else i, 0, j)),
            pl.BlockSpec((FFN_CONV, tn), lambda i, j: (0, j)),
            pl.BlockSpec((1, tn), lambda i, j: (0, j)),
        ],
        out_specs=out_specs,
        out_shape=out_shape,
        scratch_shapes=scratch,
        compiler_params=_params("arbitrary", "arbitrary"),
        name="ffn_up_act",
    )(x, g.reshape(1, d).astype(F32), w_u.array, w_g.array, buf0, cw.astype(F32),
      cb.reshape(1, D_FF).astype(F32))
    nb = outs[1].transpose(0, 2, 1, 3).reshape(b, FFN_CONV - 1, D_FF)
    return (outs[0], nb) + tuple(outs[2:])


def _suffix_matrix():
    j = lax.broadcasted_iota(jnp.int32, (2 * HEAD, 2 * HEAD), 0) % HEAD
    s = lax.broadcasted_iota(jnp.int32, (2 * HEAD, 2 * HEAD), 1)
    return jnp.where((s >= HEAD) | (j >= s), 1.0, 0.0).astype(BF16)


def _sb_weights(s, neg_bias, carry, suffix, mask):
    nz, lk = _sb_log_keep(s, neg_bias, mask)
    return _sb_finish(nz, _sb_suffix_sums(lk, suffix), carry, mask)


def _sb_log_keep(s, neg_bias, mask):
    nz = neg_bias - s
    neg_abs = pltpu.bitcast(pltpu.bitcast(nz, jnp.uint32) | jnp.uint32(0x80000000), F32)
    lk = jnp.minimum(nz, 0.0) - jnp.log2(1.0 + jnp.exp2(neg_abs))
    if mask is not None:
        lk = jnp.where(mask, lk, 0.0)
    return nz, lk


def _sb_suffix_sums(lk, suffix):
    hi = lk.astype(BF16)
    lo = (lk - hi.astype(F32)).astype(BF16)
    return jnp.dot(jnp.concatenate([hi, lo], axis=1), suffix, preferred_element_type=F32)


def _sb_finish(nz, c2, carry, mask):
    w = jnp.exp2((carry + c2[:, :HEAD]) - nz)
    if mask is not None:
        w = jnp.where(mask, w, 0.0)
    return w, carry + c2[:, HEAD:]


def _nt_dot(a, b):
    return lax.dot_general(a, b, (((1,), (1,)), ((), ())), preferred_element_type=F32)


def _sb_prompt_kernel(nbias_ref, q_ref, k_ref, v_ref, o_ref, carry_ref, acc_ref, *, tq, hp):
    hg = pl.program_id(1)
    qi = pl.program_id(2)
    suffix = _suffix_matrix()
    n_sub = tq // HEAD
    carry_ref[...] = jnp.zeros(carry_ref.shape, F32)
    acc_ref[...] = jnp.zeros(acc_ref.shape, F32)

    def block(start, masked):
        for e in range(hp):
            neg_bias = nbias_ref[hg * hp + e]
            s = _nt_dot(q_ref[0, :, e * HEAD:(e + 1) * HEAD], k_ref[0, e, pl.ds(start, tq), :])
            carry = carry_ref[e]
            ws = [None] * n_sub
            for j in range(n_sub - 1, -1, -1):
                if masked:
                    r0 = j * HEAD
                    mask = (lax.broadcasted_iota(jnp.int32, (tq - r0, HEAD), 1)
                            < lax.broadcasted_iota(jnp.int32, (tq - r0, HEAD), 0))
                    w, c = _sb_weights(s[r0:, j * HEAD:(j + 1) * HEAD], neg_bias, carry[r0:], suffix, mask)
                    if r0:
                        w = jnp.concatenate([jnp.zeros((r0, HEAD), F32), w], axis=0)
                        c = jnp.concatenate([carry[:r0], c], axis=0)
                    carry = c
                else:
                    w, carry = _sb_weights(s[:, j * HEAD:(j + 1) * HEAD], neg_bias, carry, suffix, None)
                ws[j] = w.astype(BF16)
            carry_ref[e] = carry
            acc_ref[e] += jnp.dot(jnp.concatenate(ws, axis=1), v_ref[0, e, pl.ds(start, tq), :],
                                  preferred_element_type=F32)

    block(pl.multiple_of(qi * tq, tq), True)

    odd = qi % 2

    @pl.when(odd == 1)
    def _():
        block(pl.multiple_of((qi - 1) * tq, tq), False)

    @pl.loop(0, qi // 2)
    def _(i):
        later = qi - odd - 1 - 2 * i
        block(pl.multiple_of(later * tq, tq), False)
        block(pl.multiple_of((later - 1) * tq, tq), False)

    for e in range(hp):
        o_ref[0, :, e * HEAD:(e + 1) * HEAD] = acc_ref[e].astype(o_ref.dtype)


def sb_prompt(q, k, v, bias):
    b, _, t, _ = k.shape
    tq = _tile(t, 512, HEAD)
    hp = SB_HEADS_PER_STEP
    return pl.pallas_call(
        functools.partial(_sb_prompt_kernel, tq=tq, hp=hp),
        grid_spec=pltpu.PrefetchScalarGridSpec(
            num_scalar_prefetch=1,
            grid=(b, SB_HEADS // hp, t // tq),
            in_specs=[
                pl.BlockSpec((1, tq, hp * HEAD), lambda i, h, j, nb: (i, j, h)),
                pl.BlockSpec((1, hp, t, HEAD), lambda i, h, j, nb: (i, h, 0, 0)),
                pl.BlockSpec((1, hp, t, HEAD), lambda i, h, j, nb: (i, h, 0, 0)),
            ],
            out_specs=pl.BlockSpec((1, tq, hp * HEAD), lambda i, h, j, nb: (i, j, h)),
            scratch_shapes=[pltpu.VMEM((hp, tq, HEAD), F32), pltpu.VMEM((hp, tq, HEAD), F32)],
        ),
        out_shape=jax.ShapeDtypeStruct((b, t, D_RNN), BF16),
        compiler_params=_params("parallel", "parallel", "arbitrary"),
        name="sb_prompt",
    )(-LOG2E * bias.astype(F32), q, k, v)


def _sb_sample_kernel(pt_ref, q_ref, kn_ref, vn_ref, bias_ref, *rest, pages_per_step, t_valid):
    kp_refs = rest[:pages_per_step]
    vp_refs = rest[pages_per_step:2 * pages_per_step]
    o_ref, carry_ref, acc_ref = rest[2 * pages_per_step:]
    g = pl.program_id(1)
    rows = SB_HEADS * Q_ROWS
    suffix = _suffix_matrix()
    neg_bias = bias_ref[...]
    hcols = [slice(h * HEAD, (h + 1) * HEAD) for h in range(SB_HEADS)]
    q_heads = [q_ref[0, :, hcols[h]] for h in range(SB_HEADS)]

    def update(blocks, mask):
        n = len(blocks)
        s_heads = [_nt_dot(q_heads[h], jnp.concatenate([blocks[r][0][h] for r in range(n)], axis=0))[0:Q_ROWS]
                   for h in range(SB_HEADS)]
        nzs, lks = [], []
        for r in range(n):
            s = jnp.concatenate([sh[:, r * HEAD:(r + 1) * HEAD] for sh in s_heads], axis=0)
            nz, lk = _sb_log_keep(s, neg_bias, mask)
            nzs.append(nz)
            lks.append(lk)
        c2 = _sb_suffix_sums(jnp.concatenate(lks, axis=0), suffix)
        carry = carry_ref[...]
        ws = []
        for r in range(n):
            w, carry = _sb_finish(nzs[r], c2[r * rows:(r + 1) * rows], carry, mask)
            ws.append(w)
        carry_ref[...] = carry
        zpad = jnp.zeros((SAMPLE_T_PAD - Q_ROWS, n * HEAD), F32)
        for h in range(SB_HEADS):
            rs = slice(h * Q_ROWS, (h + 1) * Q_ROWS)
            wh = jnp.concatenate([jnp.concatenate([ws[r][rs] for r in range(n)], axis=1), zpad], axis=0)
            vh = jnp.concatenate([blocks[r][1][h] for r in range(n)], axis=0)
            acc_ref[rs, :] += jnp.dot(wh.astype(BF16), vh, preferred_element_type=F32)[0:Q_ROWS]

    @pl.when(g == 0)
    def _():
        carry_ref[...] = jnp.zeros((rows, HEAD), F32)
        acc_ref[...] = jnp.zeros((rows, HEAD), F32)
        kz = jnp.zeros((HEAD - SAMPLE_T_PAD, HEAD), BF16)
        k_heads = [jnp.concatenate([kn_ref[0, :, hcols[h]], kz], axis=0) for h in range(SB_HEADS)]
        v_heads = [jnp.concatenate([vn_ref[0, :, hcols[h]], kz], axis=0) for h in range(SB_HEADS)]
        i_q = lax.broadcasted_iota(jnp.int32, (rows, HEAD), 0) % Q_ROWS
        j_k = lax.broadcasted_iota(jnp.int32, (rows, HEAD), 1)
        update([(k_heads, v_heads)], (j_k < i_q) & (j_k < t_valid))

    update([([kp_refs[r][0, h].astype(BF16) for h in range(SB_HEADS)],
             [vp_refs[r][0, h].astype(BF16) for h in range(SB_HEADS)]) for r in range(pages_per_step)], None)

    @pl.when(g == pl.num_programs(1) - 1)
    def _():
        z8 = jnp.zeros((SAMPLE_T_PAD - Q_ROWS, HEAD), F32)
        for h in range(SB_HEADS):
            blk = jnp.concatenate([acc_ref[h * Q_ROWS:(h + 1) * Q_ROWS, :], z8], axis=0)
            o_ref[0, :, h * HEAD:(h + 1) * HEAD] = blk.astype(o_ref.dtype)


def sb_sample(q, k_new, v_new, cache_k, cache_v, page_table, bias, *, t_valid, pages_per_step=8):
    b, n_pages = page_table.shape
    page = cache_k.shape[2]
    assert page == HEAD and n_pages % pages_per_step == 0 and t_valid <= Q_ROWS
    assert cache_k.shape[1:] == (SB_HEADS, page, HEAD)
    rows = SB_HEADS * Q_ROWS
    bias_rows = jnp.broadcast_to(jnp.repeat(-LOG2E * bias.astype(F32), Q_ROWS)[:, None], (rows, HEAD))

    def page_spec(r):
        return pl.BlockSpec(
            (1, SB_HEADS, page, HEAD),
            lambda i, g, pt: (pt[i, n_pages - 1 - (g * pages_per_step + r)], 0, 0, 0))

    return pl.pallas_call(
        functools.partial(_sb_sample_kernel, pages_per_step=pages_per_step, t_valid=t_valid),
        grid_spec=pltpu.PrefetchScalarGridSpec(
            num_scalar_prefetch=1,
            grid=(b, n_pages // pages_per_step),
            in_specs=[
                pl.BlockSpec((1, SAMPLE_T_PAD, D_RNN), lambda i, g, pt: (i, 0, 0)),
                pl.BlockSpec((1, SAMPLE_T_PAD, D_RNN), lambda i, g, pt: (i, 0, 0)),
                pl.BlockSpec((1, SAMPLE_T_PAD, D_RNN), lambda i, g, pt: (i, 0, 0)),
                pl.BlockSpec((rows, HEAD), lambda i, g, pt: (0, 0)),
            ] + [page_spec(r) for r in range(pages_per_step)] + [page_spec(r) for r in range(pages_per_step)],
            out_specs=pl.BlockSpec((1, SAMPLE_T_PAD, D_RNN), lambda i, g, pt: (i, 0, 0)),
            scratch_shapes=[pltpu.VMEM((rows, HEAD), F32), pltpu.VMEM((rows, HEAD), F32)],
        ),
        out_shape=jax.ShapeDtypeStruct((b, SAMPLE_T_PAD, D_RNN), BF16),
        compiler_params=_params("parallel", "arbitrary"),
        name="sb_sample",
    )(page_table, q, k_new, v_new, bias_rows, *([cache_k] * pages_per_step), *([cache_v] * pages_per_step))


def _trunk(x3, t_valid, rg_h0, rg_conv0, ffn_conv0, mem_k, mem_v, sb_attend, head_major_kv, P, W, emit_w):
    b, t, d = x3.shape
    m = b * t
    x = x3.reshape(m, d)
    rg_h_out, rg_conv_out, ffn_conv_out = [], [], []
    k_f32 = v_f32 = k_bf = v_bf = None
    emitted = {}

    def nmm(key, x, g, **kw):
        if not emit_w:
            kw.setdefault('tn_pref', 2048)
        out = norm_matmul(x, g, W[key], emit_w=emit_w, **kw)
        if not emit_w:
            return out
        *out, emitted[key] = out
        return out[0] if len(out) == 1 else tuple(out)

    def mmr(keys, acts, x, **kw):
        out = matmul_residual([(a, W[k]) for a, k in zip(acts, keys)], x, emit_w=emit_w, **kw)
        if not emit_w:
            return out
        for k, wb in zip(keys, out[1:]):
            emitted[k] = wb
        return out[0]

    for l in range(DEPTH):
        mem_gain = jnp.tile(P['mem_q_norm'][l] * ATTN_SCALE, MEM_HEADS)
        if l < N_A_LAYERS:
            p_rg = nmm(('in_a_rg', l), x, P['g_mix'][l])
            qm = nmm(('in_a_qm', l), x, P['g_mix'][l], head_gain=mem_gain)
            y_tok, h_last, cbuf = rglru(
                p_rg.reshape(b, t, 2 * D_RNN), rg_h0[l], rg_conv0[l], P['rg_conv_w'][l], P['rg_conv_b'][l],
                P['rg_gate_x_w'][l], P['rg_gate_x_b'][l], P['rg_gate_a_w'][l], P['rg_gate_a_b'][l],
                P['rg_lambda'][l], t_valid=t_valid)
            rg_h_out.append(h_last)
            rg_conv_out.append(cbuf)
            q3, q_col = qm.reshape(b, t, D_MEM), 0
        else:
            j = l - N_A_LAYERS
            gain = jnp.concatenate([jnp.tile(P['sb_q_norm'][j] * (ATTN_SCALE * LOG2E), SB_HEADS), mem_gain])
            pq = nmm(('in_b', l), x, P['g_mix'][l], head_gain=gain)
            q3, q_col = pq.reshape(b, t, D_RNN + D_MEM), D_RNN // D_MEM
            y_tok = sb_attend(q3, k_bf, v_bf, P['sb_beta_bias'][j])
        y_mem = mem_attn(q3, q_col, mem_k[l], mem_v[l])
        out_tiles = {} if emit_w else dict(tm_pref=512, tn_pref=2048)
        x = mmr([('out_tok', l), ('out_mem', l)], [y_tok.reshape(m, D_RNN), y_mem.reshape(m, D_MEM)], x,
                **out_tiles)
        act, fbuf, *wbs = ffn_up_act(x, P['g_ffn'][l], W[('ffn_u', l)], W[('ffn_g', l)], ffn_conv0[l],
                                     P['ffn_conv_w'][l], P['ffn_conv_b'][l], seq=t, t_valid=t_valid,
                                     emit_w=emit_w)
        if emit_w:
            emitted[('ffn_u', l)], emitted[('ffn_g', l)] = wbs
        x = mmr([('ffn_down', l)], [act], x, tn_pref=512)
        ffn_conv_out.append(fbuf)
        if l == N_A_LAYERS - 1:
            k_gain = jnp.tile(P['kv_k_norm'], SB_HEADS)
            hm = t if head_major_kv else None
            k_f32, k_bf = nmm(('k', l), x, P['kv_norm'], head_gain=k_gain, out_dtypes=(F32, BF16),
                              head_major_seq=hm)
            v_f32, v_bf = nmm(('v', l), x, P['kv_norm'], out_dtypes=(F32, BF16), head_major_seq=hm)
            if not head_major_kv:
                k_bf = k_bf.reshape(b, t, D_RNN)
                v_bf = v_bf.reshape(b, t, D_RNN)
    if head_major_kv:
        k_out, v_out = k_f32.transpose(0, 2, 1, 3), v_f32.transpose(0, 2, 1, 3)
    else:
        k_out, v_out = k_f32.reshape(b, t, SB_HEADS, HEAD), v_f32.reshape(b, t, SB_HEADS, HEAD)
    return (x.reshape(b, t, d), jnp.stack(rg_h_out), jnp.stack(rg_conv_out), jnp.stack(ffn_conv_out),
            k_out, v_out, emitted)


def kernel(x_prompt, x_sample, mem_prompt, state_rglru_h, state_rglru_conv, state_ffn_conv, cache_mem_k, cache_mem_v, cache_sb_k, cache_sb_v, page_table, g_mix, g_ffn, w_in_a, rg_conv_w, rg_conv_b, rg_gate_x_w, rg_gate_x_b, rg_gate_a_w, rg_gate_a_b, rg_lambda, w_in_b, sb_q_norm, sb_beta_bias, kv_norm, w_kv, kv_k_norm, mem_norm, w_mem_kv, mem_q_norm, mem_k_norm, w_out, w_ffn_up, ffn_conv_w, ffn_conv_b, w_ffn_down):
    P = {'g_mix': g_mix, 'g_ffn': g_ffn, 'rg_conv_w': rg_conv_w, 'rg_conv_b': rg_conv_b,
         'rg_gate_x_b': rg_gate_x_b, 'rg_gate_a_b': rg_gate_a_b, 'rg_lambda': rg_lambda,
         'sb_q_norm': sb_q_norm, 'sb_beta_bias': sb_beta_bias, 'kv_norm': kv_norm,
         'kv_k_norm': kv_k_norm, 'mem_q_norm': mem_q_norm,
         'ffn_conv_w': ffn_conv_w, 'ffn_conv_b': ffn_conv_b,
         'rg_gate_x_w': rg_gate_x_w, 'rg_gate_a_w': rg_gate_a_w}
    dm = D_MODEL
    W = {}
    for l in range(DEPTH):
        if l < N_A_LAYERS:
            W[('in_a_rg', l)] = WeightView(w_in_a, l, 0, dm, 0, 2 * D_RNN)
            W[('in_a_qm', l)] = WeightView(w_in_a, l, 0, dm, 2 * D_RNN, D_MEM)
        else:
            W[('in_b', l)] = WeightView(w_in_b, l - N_A_LAYERS, 0, dm, 0, D_RNN + D_MEM)
        W[('out_tok', l)] = WeightView(w_out, l, 0, D_RNN, 0, dm)
        W[('out_mem', l)] = WeightView(w_out, l, D_RNN, D_MEM, 0, dm)
        W[('ffn_u', l)] = WeightView(w_ffn_up, l, 0, dm, 0, D_FF)
        W[('ffn_g', l)] = WeightView(w_ffn_up, l, 0, dm, D_FF, D_FF)
        W[('ffn_down', l)] = WeightView(w_ffn_down, l, 0, D_FF, 0, dm)
    W[('k', N_A_LAYERS - 1)] = WeightView(w_kv, None, 0, dm, 0, D_RNN)
    W[('v', N_A_LAYERS - 1)] = WeightView(w_kv, None, 0, dm, D_RNN, D_RNN)

    db, dec_seq, _ = x_sample.shape
    xs = jnp.pad(x_sample, ((0, 0), (0, SAMPLE_T_PAD - dec_seq), (0, 0)))

    cache_k_hm = cache_sb_k.transpose(0, 2, 1, 3)
    cache_v_hm = cache_sb_v.transpose(0, 2, 1, 3)

    def sb_paged(q3, k_bf, v_bf, bias):
        return sb_sample(q3, k_bf, v_bf, cache_k_hm, cache_v_hm, page_table, bias, t_valid=dec_seq)

    (y_s, sample_rglru_h, sample_rglru_conv, sample_ffn_conv, s_k, s_v, w_bf16) = _trunk(
        xs, dec_seq, state_rglru_h, state_rglru_conv, state_ffn_conv,
        cache_mem_k.reshape(DEPTH, db, MEM_TOKENS, D_MEM), cache_mem_v.reshape(DEPTH, db, MEM_TOKENS, D_MEM),
        sb_paged, False, P, W, True)
    W_prompt = {key: _bf16_view(arr) for key, arr in w_bf16.items()}

    bp, seq, d = x_prompt.shape
    mem2 = mem_prompt.reshape(bp * MEM_TOKENS, d)
    mk_list, mv_list = [], []
    for l in range(DEPTH):
        mk_list.append(norm_matmul(mem2, mem_norm[l], WeightView(w_mem_kv, l, 0, dm, 0, D_MEM),
                                   head_gain=jnp.tile(mem_k_norm[l], MEM_HEADS), out_dtypes=(F32,)))
        mv_list.append(norm_matmul(mem2, mem_norm[l], WeightView(w_mem_kv, l, 0, dm, D_MEM, D_MEM),
                                   out_dtypes=(F32,)))
    prompt_mem_k = jnp.stack(mk_list).reshape(DEPTH, bp, MEM_TOKENS, D_MEM)
    prompt_mem_v = jnp.stack(mv_list).reshape(DEPTH, bp, MEM_TOKENS, D_MEM)
    zeros_h = jnp.zeros((N_A_LAYERS, bp, D_RNN), F32)
    zeros_rc = jnp.zeros((N_A_LAYERS, bp, RG_CONV - 1, D_RNN), F32)
    zeros_fc = jnp.zeros((DEPTH, bp, FFN_CONV - 1, D_FF), F32)
    (y_prompt, prompt_rglru_h, prompt_rglru_conv, prompt_ffn_conv, prompt_sb_k, prompt_sb_v, _) = _trunk(
        x_prompt, seq, zeros_h, zeros_rc, zeros_fc, prompt_mem_k, prompt_mem_v, sb_prompt, True, P, W_prompt,
        False)

    mem_shape = (DEPTH, bp, MEM_TOKENS, MEM_HEADS, HEAD)
    return (y_prompt, y_s[:, :dec_seq],
            prompt_rglru_h, prompt_rglru_conv, prompt_ffn_conv,
            prompt_sb_k, prompt_sb_v, prompt_mem_k.reshape(mem_shape), prompt_mem_v.reshape(mem_shape),
            sample_rglru_h, sample_rglru_conv, sample_ffn_conv,
            s_k[:, :dec_seq], s_v[:, :dec_seq])
```

```python
import functools
import math
from typing import NamedTuple

import jax
import jax.numpy as jnp
from jax import lax
from jax.experimental import pallas as pl
from jax.experimental.pallas import tpu as pltpu

F32 = jnp.float32
BF16 = jnp.bfloat16

LANE = 128
SUBLANE = 8
VMEM_LIMIT_BYTES = 56 * 1024 * 1024

D_MODEL = 2048
DEPTH = 4
N_A_LAYERS = 2
D_RNN = 1536
HEAD = 128
RG_HEADS = D_RNN // HEAD
RG_CONV = 4
RG_C = 8.0
SB_HEADS = D_RNN // HEAD
MEM_TOKENS = 256
MEM_HEADS = 4
D_MEM = MEM_HEADS * HEAD
D_FF = 3 * D_MODEL
FFN_CONV = 3
EPS = 1e-6
ATTN_SCALE = HEAD ** -0.5
LOG2E = math.log2(math.e)
SAMPLE_T_PAD = 16
Q_ROWS = 8
FFN_ROW_CHUNK = 256
NORM_ROW_CHUNK = 256
SB_HEADS_PER_STEP = 6


def _params(*sem):
    return pltpu.CompilerParams(dimension_semantics=sem, vmem_limit_bytes=VMEM_LIMIT_BYTES)


def _tile(n, pref, mult):
    if n <= pref:
        return n
    t = (pref // mult) * mult
    while t > mult and n % t:
        t -= mult
    assert n % t == 0, (n, pref, mult)
    return t


class WeightView(NamedTuple):
    array: jax.Array
    layer: int | None
    row0: int
    rows: int
    col0: int
    cols: int


def _weight_spec(wv, tk, tn, row_block, col_block):
    assert wv.row0 % tk == 0 and wv.col0 % tn == 0 and wv.rows % tk == 0 and wv.cols % tn == 0
    r0, c0 = wv.row0 // tk, wv.col0 // tn
    if wv.layer is None:
        return pl.BlockSpec((tk, tn), lambda *g: (r0 + row_block(*g), c0 + col_block(*g)))
    layer = wv.layer
    return pl.BlockSpec((None, tk, tn), lambda *g: (layer, r0 + row_block(*g), c0 + col_block(*g)))


def _softplus(z):
    return jnp.maximum(z, 0.0) + jnp.log1p(jnp.exp(-jnp.abs(z)))


def _head_rmsnorm(blk, gain):
    ms = jnp.mean(blk * blk, axis=-1, keepdims=True)
    return blk * lax.rsqrt(ms + EPS) * gain


def _rmsnorm_bf16(x, g):
    ms = jnp.mean(x * x, axis=-1, keepdims=True)
    return (x * lax.rsqrt(ms + EPS) * g).astype(BF16)


def _norm_matmul_kernel(*refs, head_norm, head_major, n_out, emit_w):
    x_ref, g_ref, w_ref = refs[:3]
    pos = 3
    hg_ref = None
    if head_norm:
        hg_ref = refs[pos]
        pos += 1
    o_refs = refs[pos:pos + n_out]
    pos += n_out
    wb_ref = None
    if emit_w:
        wb_ref = refs[pos]
        pos += 1
    xn_ref = refs[pos]

    @pl.when(pl.program_id(1) == 0)
    def _():
        xn_ref[...] = _rmsnorm_bf16(x_ref[...], g_ref[...])

    wb = w_ref[...].astype(BF16)
    if emit_w:
        wb_ref[...] = wb
    tm = xn_ref.shape[0]
    tn = wb.shape[1]
    if head_norm or head_major:
        rc = min(tm, NORM_ROW_CHUNK)
        for c in range(tm // rc):
            rows = slice(c * rc, (c + 1) * rc)
            acc = jnp.dot(xn_ref[rows, :], wb, preferred_element_type=F32)
            for h in range(tn // HEAD):
                cs = slice(h * HEAD, (h + 1) * HEAD)
                y = _head_rmsnorm(acc[:, cs], hg_ref[:, cs]) if head_norm else acc[:, cs]
                for o_ref in o_refs:
                    if head_major:
                        o_ref[0, h, rows, :] = y.astype(o_ref.dtype)
                    else:
                        o_ref[rows, cs] = y.astype(o_ref.dtype)
    else:
        acc = jnp.dot(xn_ref[...], wb, preferred_element_type=F32)
        for o_ref in o_refs:
            o_ref[...] = acc.astype(o_ref.dtype)


def _bf16_view(arr):
    return WeightView(arr, None, 0, arr.shape[0], 0, arr.shape[1])


def norm_matmul(x, g, w, *, head_gain=None, out_dtypes=(BF16,), head_major_seq=None, emit_w=False,
                tm_pref=1024, tn_pref=1024):
    m, d = x.shape
    n = w.cols
    assert w.rows == d
    tm = _tile(m, tm_pref, SUBLANE)
    tn = _tile(n, tn_pref, 2 * LANE)
    assert not emit_w or (m == tm and head_major_seq is None)
    if head_major_seq is None:
        out_specs = [pl.BlockSpec((tm, tn), lambda i, j: (i, j)) for _ in out_dtypes]
        out_shape = [jax.ShapeDtypeStruct((m, n), dt) for dt in out_dtypes]
    else:
        seq = head_major_seq
        assert seq % tm == 0 and m % seq == 0
        per_seq = seq // tm
        out_specs = [pl.BlockSpec((1, tn // HEAD, tm, HEAD), lambda i, j: (i // per_seq, j, i % per_seq, 0))
                     for _ in out_dtypes]
        out_shape = [jax.ShapeDtypeStruct((m // seq, n // HEAD, seq, HEAD), dt) for dt in out_dtypes]
    in_specs = [
        pl.BlockSpec((tm, d), lambda i, j: (i, 0)),
        pl.BlockSpec((1, d), lambda i, j: (0, 0)),
        _weight_spec(w, d, tn, lambda i, j: 0, lambda i, j: j),
    ]
    args = [x, g.reshape(1, d).astype(F32), w.array]
    if head_gain is not None:
        in_specs.append(pl.BlockSpec((1, tn), lambda i, j: (0, j)))
        args.append(head_gain.reshape(1, n).astype(F32))
    if emit_w:
        out_specs.append(pl.BlockSpec((d, tn), lambda i, j: (0, j)))
        out_shape.append(jax.ShapeDtypeStruct((d, n), BF16))
    outs = pl.pallas_call(
        functools.partial(_norm_matmul_kernel, head_norm=head_gain is not None,
                          head_major=head_major_seq is not None, n_out=len(out_dtypes), emit_w=emit_w),
        grid=(m // tm, n // tn),
        in_specs=in_specs,
        out_specs=out_specs,
        out_shape=out_shape,
        scratch_shapes=[pltpu.VMEM((tm, d), BF16)],
        compiler_params=_params("parallel", "arbitrary"),
        name="norm_matmul",
    )(*args)
    return outs[0] if len(outs) == 1 else tuple(outs)


def _matmul_residual_kernel(*refs, n_pairs, emit_w):
    x_ref = refs[2 * n_pairs]
    o_ref = refs[2 * n_pairs + 1]
    wb_refs = refs[2 * n_pairs + 2:]

    @pl.when(pl.program_id(2) == 0)
    def _():
        o_ref[...] = x_ref[...]

    acc = None
    for p in range(n_pairs):
        wb = refs[2 * p + 1][...].astype(BF16)
        if emit_w:
            wb_refs[p][...] = wb
        d = jnp.dot(refs[2 * p][...], wb, preferred_element_type=F32)
        acc = d if acc is None else acc + d
    o_ref[...] += acc


def matmul_residual(pairs, x, *, n_k=1, emit_w=False, tm_pref=1024, tn_pref=1024):
    m, n = x.shape
    tm = _tile(m, tm_pref, SUBLANE)
    tn = _tile(n, tn_pref, 2 * LANE)
    assert not emit_w or m == tm
    out_specs = [pl.BlockSpec((tm, tn), lambda i, j, k: (i, j))]
    out_shape = [jax.ShapeDtypeStruct((m, n), F32)]
    in_specs, args = [], []
    for a, w in pairs:
        kp = a.shape[1]
        assert kp % n_k == 0 and (w.rows, w.cols) == (kp, n)
        tk = kp // n_k
        in_specs.append(pl.BlockSpec((tm, tk), lambda i, j, k: (i, k)))
        in_specs.append(_weight_spec(w, tk, tn, lambda i, j, k: k, lambda i, j, k: j))
        args += [a, w.array]
        if emit_w:
            out_specs.append(pl.BlockSpec((tk, tn), lambda i, j, k: (k, j)))
            out_shape.append(jax.ShapeDtypeStruct((kp, n), BF16))
    in_specs.append(pl.BlockSpec((tm, tn), lambda i, j, k: (i, j)))
    args.append(x)
    outs = pl.pallas_call(
        functools.partial(_matmul_residual_kernel, n_pairs=len(pairs), emit_w=emit_w),
        grid=(m // tm, n // tn, n_k),
        in_specs=in_specs,
        out_specs=out_specs,
        out_shape=out_shape,
        compiler_params=_params("parallel", "parallel", "arbitrary"),
        name="matmul_residual",
    )(*args)
    return tuple(outs) if emit_w else outs[0]


def _rglru_kernel(p_ref, h0_ref, cb0_ref, cw_ref, cbias_ref, wgx_ref, bgx_ref, wga_ref, bga_ref, lam_ref,
                  y_ref, hl_ref, cbo_ref, ext_ref, hc_ref, *, tr, tile_last, r_last):
    t = pl.program_id(1)
    halo = SUBLANE

    @pl.when(t == 0)
    def _():
        ext_ref[0:halo, :] = jnp.zeros((halo, D_RNN), F32)
        ext_ref[halo - (RG_CONV - 1):halo, :] = cb0_ref[0]
        hc_ref[...] = h0_ref[0]

    ext_ref[halo:halo + tr, :] = p_ref[0, :, 0:D_RNN].astype(F32)
    sub_row = lax.broadcasted_iota(jnp.int32, (tr // SUBLANE, SUBLANE, HEAD), 1)
    for h in range(RG_HEADS):
        cs = slice(h * HEAD, (h + 1) * HEAD)
        xc = cbias_ref[:, cs] + ext_ref[halo - 3:halo - 3 + tr, cs] * cw_ref[0:1, cs]
        for j in range(1, RG_CONV):
            xc = xc + ext_ref[halo - 3 + j:halo - 3 + j + tr, cs] * cw_ref[j:j + 1, cs]
        xcb = xc.astype(BF16)
        gx = jax.nn.sigmoid(jnp.dot(xcb, wgx_ref[h].astype(BF16), preferred_element_type=F32) + bgx_ref[:, cs])
        ga = jax.nn.sigmoid(jnp.dot(xcb, wga_ref[h].astype(BF16), preferred_element_type=F32) + bga_ref[:, cs])
        log_a = (-RG_C) * ga * _softplus(-lam_ref[:, cs])
        a = jnp.exp(log_a)
        u = jnp.sqrt(-jnp.tanh(log_a) * (1.0 + a * a)) * (gx * xc)
        groups = tr // SUBLANE
        a3 = a.reshape(groups, SUBLANE, HEAD)
        u3 = u.reshape(groups, SUBLANE, HEAD)
        d = 1
        while d < SUBLANE:
            live = sub_row >= d
            u3 = jnp.where(live, a3 * pltpu.roll(u3, d, 1) + u3, u3)
            a3 = jnp.where(live, a3 * pltpu.roll(a3, d, 1), a3)
            d *= 2
        h_prev = jnp.broadcast_to(hc_ref[:, cs], (SUBLANE, HEAD))
        h_groups = []
        for k in range(groups):
            h_k = a3[k] * h_prev + u3[k]
            h_groups.append(h_k)
            h_prev = jnp.broadcast_to(h_k[SUBLANE - 1:SUBLANE, :], (SUBLANE, HEAD))
        u = jnp.concatenate(h_groups, axis=0)
        gate = jax.nn.gelu(p_ref[0, :, D_RNN + h * HEAD:D_RNN + (h + 1) * HEAD].astype(F32))
        y_ref[0, :, cs] = (u * gate).astype(y_ref.dtype)
        hc_ref[:, cs] = u[tr - 1:tr, :]

        @pl.when(t == tile_last)
        def _():
            hl_ref[0, :, cs] = u[r_last:r_last + 1, :]

    @pl.when(t == tile_last)
    def _():
        cbo_ref[0] = ext_ref[halo + r_last - 2:halo + r_last + 1, :]

    if tr >= halo:
        ext_ref[0:halo, :] = ext_ref[tr:tr + halo, :]


def rglru(p, h0, cb0, cw, cbias, wgx, bgx, wga, bga, lam, *, t_valid):
    b, t, _ = p.shape
    tr = _tile(t, 256, 16)
    row = lambda v: v.reshape(1, D_RNN).astype(F32)
    full = lambda shape: pl.BlockSpec(shape, lambda i, j: (0,) * len(shape))
    y, hl, cbo = pl.pallas_call(
        functools.partial(_rglru_kernel, tr=tr, tile_last=(t_valid - 1) // tr, r_last=(t_valid - 1) % tr),
        grid=(b, t // tr),
        in_specs=[
            pl.BlockSpec((1, tr, 2 * D_RNN), lambda i, j: (i, j, 0)),
            pl.BlockSpec((1, 1, D_RNN), lambda i, j: (i, 0, 0)),
            pl.BlockSpec((1, RG_CONV - 1, D_RNN), lambda i, j: (i, 0, 0)),
            full((RG_CONV, D_RNN)), full((1, D_RNN)),
            full((RG_HEADS, HEAD, HEAD)), full((1, D_RNN)),
            full((RG_HEADS, HEAD, HEAD)), full((1, D_RNN)),
            full((1, D_RNN)),
        ],
        out_specs=[
            pl.BlockSpec((1, tr, D_RNN), lambda i, j: (i, j, 0)),
            pl.BlockSpec((1, 1, D_RNN), lambda i, j: (i, 0, 0)),
            pl.BlockSpec((1, RG_CONV - 1, D_RNN), lambda i, j: (i, 0, 0)),
        ],
        out_shape=[
            jax.ShapeDtypeStruct((b, t, D_RNN), BF16),
            jax.ShapeDtypeStruct((b, 1, D_RNN), F32),
            jax.ShapeDtypeStruct((b, RG_CONV - 1, D_RNN), F32),
        ],
        scratch_shapes=[pltpu.VMEM((tr + SUBLANE, D_RNN), F32), pltpu.VMEM((1, D_RNN), F32)],
        compiler_params=_params("parallel", "arbitrary"),
        name="rglru",
    )(p, h0.reshape(b, 1, D_RNN), cb0, cw.astype(F32), row(cbias), wgx, row(bgx), wga, row(bga), row(lam))
    return y, hl.reshape(b, D_RNN), cbo


def _mem_attn_kernel(q_ref, k_ref, v_ref, o_ref):
    for h in range(MEM_HEADS):
        cs = slice(h * HEAD, (h + 1) * HEAD)
        q = q_ref[0, :, cs]
        k = k_ref[0, :, cs].astype(BF16)
        v = v_ref[0, :, cs].astype(BF16)
        s = lax.dot_general(q, k, (((1,), (1,)), ((), ())), preferred_element_type=F32)
        e = jnp.exp(s - jnp.max(s, axis=-1, keepdims=True))
        p = e / jnp.sum(e, axis=-1, keepdims=True)
        o_ref[0, :, cs] = jnp.dot(p.astype(BF16), v, preferred_element_type=F32).astype(o_ref.dtype)


def mem_attn(q, q_col_block, mk, mv):
    b, t, _ = q.shape
    tq = _tile(t, 2048, 16)
    return pl.pallas_call(
        _mem_attn_kernel,
        grid=(b, t // tq),
        in_specs=[
            pl.BlockSpec((1, tq, D_MEM), lambda i, j: (i, j, q_col_block)),
            pl.BlockSpec((1, MEM_TOKENS, D_MEM), lambda i, j: (i, 0, 0)),
            pl.BlockSpec((1, MEM_TOKENS, D_MEM), lambda i, j: (i, 0, 0)),
        ],
        out_specs=pl.BlockSpec((1, tq, D_MEM), lambda i, j: (i, j, 0)),
        out_shape=jax.ShapeDtypeStruct((b, t, D_MEM), BF16),
        compiler_params=_params("parallel", "parallel"),
        name="mem_attn",
    )(q, mk, mv)


def _shift_rows(u, shift, head_rows):
    s = pltpu.roll(u, shift, 0)
    row = lax.broadcasted_iota(jnp.int32, (SUBLANE, u.shape[1]), 0)
    top = s[0:SUBLANE]
    for r, v in enumerate(head_rows):
        top = jnp.where(row == r, v, top)
    return jnp.concatenate([top, s[SUBLANE:]], axis=0)


def _conv_gelu_gate(u, gate, prev2, prev1, cw_ref, cb_ref):
    uc = (cb_ref[...] + _shift_rows(u, 2, [prev2, prev1]) * cw_ref[0:1, :]
          + _shift_rows(u, 1, [prev1]) * cw_ref[1:2, :] + u * cw_ref[2:3, :])
    return jax.nn.gelu(uc) * gate


def _ffn_up_act_kernel(x_ref, g_ref, wu_in_ref, wg_in_ref, b0_ref, cw_ref, cb_ref, act_ref, nb_ref, *rest,
                       seq, tiles_per_seq, t_valid, cast_w):
    i = pl.program_id(0)
    j = pl.program_id(1)
    tm = x_ref.shape[0]
    if cast_w:
        wu_ref, wg_ref, xn_ref, tail_ref = rest
        wu_ref[...] = wu_in_ref[...].astype(BF16)
        wg_ref[...] = wg_in_ref[...].astype(BF16)
    else:
        xn_ref, tail_ref = rest
        wu_ref, wg_ref = wu_in_ref, wg_in_ref

    @pl.when(j == 0)
    def _():
        xn_ref[...] = _rmsnorm_bf16(x_ref[...], g_ref[...])

    if tiles_per_seq >= 1:
        @pl.when(i == 0)
        def _():
            tail_ref[j] = jnp.zeros(tail_ref.shape[1:], F32)

        first = (i % tiles_per_seq) == 0
        prev2 = jnp.where(first, b0_ref[0, 0:1, :], tail_ref[j, SUBLANE - 2:SUBLANE - 1, :])
        prev1 = jnp.where(first, b0_ref[0, 1:2, :], tail_ref[j, SUBLANE - 1:SUBLANE, :])
        rc = min(tm, FFN_ROW_CHUNK)
        for c in range(tm // rc):
            xs = xn_ref[c * rc:(c + 1) * rc, :]
            u = jnp.dot(xs, wu_ref[...], preferred_element_type=F32)
            gate = jnp.dot(xs, wg_ref[...], preferred_element_type=F32)
            act_ref[c * rc:(c + 1) * rc, :] = _conv_gelu_gate(u, gate, prev2, prev1, cw_ref, cb_ref
                                                              ).astype(act_ref.dtype)
            prev2, prev1 = u[rc - 2:rc - 1], u[rc - 1:rc]
        tail_ref[j] = u[rc - SUBLANE:rc]

        @pl.when((i % tiles_per_seq) == tiles_per_seq - 1)
        def _():
            nb_ref[i // tiles_per_seq, j] = u[rc - 2:rc]
    else:
        xn = xn_ref[...]
        u = jnp.dot(xn, wu_ref[...], preferred_element_type=F32)
        gate = jnp.dot(xn, wg_ref[...], preferred_element_type=F32)
        for b in range(tm // seq):
            rs = slice(b * seq, (b + 1) * seq)
            a = _conv_gelu_gate(u[rs], gate[rs], b0_ref[b, 0:1, :], b0_ref[b, 1:2, :], cw_ref, cb_ref)
            act_ref[rs, :] = a.astype(act_ref.dtype)
            nb_ref[(tm // seq) * i + b, j] = u[b * seq + t_valid - 2:b * seq + t_valid]


def ffn_up_act(x, g, w_u, w_g, buf0, cw, cb, *, seq, t_valid, emit_w=False, tm_pref=1024):
    m, d = x.shape
    assert (w_u.rows, w_u.cols) == (d, D_FF) and (w_g.rows, w_g.cols) == (d, D_FF)
    b = m // seq
    tm = _tile(m, tm_pref, SUBLANE)
    assert t_valid >= FFN_CONV - 1 and (seq % tm == 0 or tm % seq == 0)
    tiles_per_seq = seq // tm if tm <= seq else 0
    assert tiles_per_seq == 0 or t_valid == seq
    seqs_per_tile = max(tm // seq, 1)
    cast_w = w_u.array.dtype != BF16
    tn = 512 if cast_w else 1536
    n_j = D_FF // tn
    assert not emit_w or (cast_w and m == tm)
    out_specs = [
        pl.BlockSpec((tm, tn), lambda i, j: (i, j)),
        pl.BlockSpec((b, n_j, FFN_CONV - 1, tn), lambda i, j: (0, 0, 0, 0)),
    ]
    out_shape = [
        jax.ShapeDtypeStruct((m, D_FF), BF16),
        jax.ShapeDtypeStruct((b, n_j, FFN_CONV - 1, tn), F32),
    ]
    scratch = [pltpu.VMEM((tm, d), BF16), pltpu.VMEM((n_j, SUBLANE, tn), F32)]
    if emit_w:
        out_specs += [pl.BlockSpec((d, tn), lambda i, j: (0, j)), pl.BlockSpec((d, tn), lambda i, j: (0, j))]
        out_shape += [jax.ShapeDtypeStruct((d, D_FF), BF16), jax.ShapeDtypeStruct((d, D_FF), BF16)]
    elif cast_w:
        scratch = [pltpu.VMEM((d, tn), BF16), pltpu.VMEM((d, tn), BF16)] + scratch
    outs = pl.pallas_call(
        functools.partial(_ffn_up_act_kernel, seq=seq, tiles_per_seq=tiles_per_seq, t_valid=t_valid,
                          cast_w=cast_w),
        grid=(m // tm, n_j),
        in_specs=[
            pl.BlockSpec((tm, d), lambda i, j: (i, 0)),
            pl.BlockSpec((1, d), lambda i, j: (0, 0)),
            _weight_spec(w_u, d, tn, lambda i, j: 0, lambda i, j: j),
            _weight_spec(w_g, d, tn, lambda i, j: 0, lambda i, j: j),
            pl.BlockSpec((seqs_per_tile, FFN_CONV - 1, tn),
                         lambda i, j: ((i // tiles_per_seq) if tiles_per_seq else i, 0, j)),
            pl.BlockSpec((FFN_CONV, tn), lambda i, j: (0, j)),
            pl.BlockSpec((1, tn), lambda i, j: (0, j)),
        ],
        out_specs=out_specs,
        out_shape=out_shape,
        scratch_shapes=scratch,
        compiler_params=_params("arbitrary", "arbitrary"),
        name="ffn_up_act",
    )(x, g.reshape(1, d).astype(F32), w_u.array, w_g.array, buf0, cw.astype(F32),
      cb.reshape(1, D_FF).astype(F32))
    nb = outs[1].transpose(0, 2, 1, 3).reshape(b, FFN_CONV - 1, D_FF)
    return (outs[0], nb) + tuple(outs[2:])


def _suffix_matrix():
    j = lax.broadcasted_iota(jnp.int32, (2 * HEAD, 2 * HEAD), 0) % HEAD
    s = lax.broadcasted_iota(jnp.int32, (2 * HEAD, 2 * HEAD), 1)
    return jnp.where((s >= HEAD) | (j >= s), 1.0, 0.0).astype(BF16)


def _sb_weights(s, neg_bias, carry, suffix, mask):
    nz, lk = _sb_log_keep(s, neg_bias, mask)
    return _sb_finish(nz, _sb_suffix_sums(lk, suffix), carry, mask)


def _sb_log_keep(s, neg_bias, mask):
    nz = neg_bias - s
    lk = jnp.minimum(nz, 0.0) - jnp.log2(1.0 + jnp.exp2(-jnp.abs(nz)))
    if mask is not None:
        lk = jnp.where(mask, lk, 0.0)
    return nz, lk


def _sb_suffix_sums(lk, suffix):
    hi = lk.astype(BF16)
    lo = (lk - hi.astype(F32)).astype(BF16)
    return jnp.dot(jnp.concatenate([hi, lo], axis=1), suffix, preferred_element_type=F32)


def _sb_finish(nz, c2, carry, mask):
    w = jnp.exp2((carry + c2[:, :HEAD]) - nz)
    if mask is not None:
        w = jnp.where(mask, w, 0.0)
    return w, carry + c2[:, HEAD:]


def _nt_dot(a, b):
    return lax.dot_general(a, b, (((1,), (1,)), ((), ())), preferred_element_type=F32)


def _sb_prompt_kernel(nbias_ref, q_ref, k_ref, v_ref, o_ref, carry_ref, acc_ref, *, tq, hp):
    hg = pl.program_id(1)
    qi = pl.program_id(2)
    suffix = _suffix_matrix()
    n_sub = tq // HEAD
    carry_ref[...] = jnp.zeros(carry_ref.shape, F32)
    acc_ref[...] = jnp.zeros(acc_ref.shape, F32)

    def block(start, masked):
        for e in range(hp):
            neg_bias = nbias_ref[hg * hp + e]
            s = _nt_dot(q_ref[0, :, e * HEAD:(e + 1) * HEAD], k_ref[0, e, pl.ds(start, tq), :])
            carry = carry_ref[e]
            ws = [None] * n_sub
            for j in range(n_sub - 1, -1, -1):
                if masked:
                    r0 = j * HEAD
                    mask = (lax.broadcasted_iota(jnp.int32, (tq - r0, HEAD), 1)
                            < lax.broadcasted_iota(jnp.int32, (tq - r0, HEAD), 0))
                    w, c = _sb_weights(s[r0:, j * HEAD:(j + 1) * HEAD], neg_bias, carry[r0:], suffix, mask)
                    if r0:
                        w = jnp.concatenate([jnp.zeros((r0, HEAD), F32), w], axis=0)
                        c = jnp.concatenate([carry[:r0], c], axis=0)
                    carry = c
                else:
                    w, carry = _sb_weights(s[:, j * HEAD:(j + 1) * HEAD], neg_bias, carry, suffix, None)
                ws[j] = w.astype(BF16)
            carry_ref[e] = carry
            acc_ref[e] += jnp.dot(jnp.concatenate(ws, axis=1), v_ref[0, e, pl.ds(start, tq), :],
                                  preferred_element_type=F32)

    block(pl.multiple_of(qi * tq, tq), True)

    odd = qi % 2

    @pl.when(odd == 1)
    def _():
        block(pl.multiple_of((qi - 1) * tq, tq), False)

    @pl.loop(0, qi // 2)
    def _(i):
        later = qi - odd - 1 - 2 * i
        block(pl.multiple_of(later * tq, tq), False)
        block(pl.multiple_of((later - 1) * tq, tq), False)

    for e in range(hp):
        o_ref[0, :, e * HEAD:(e + 1) * HEAD] = acc_ref[e].astype(o_ref.dtype)


def sb_prompt(q, k, v, bias):
    b, _, t, _ = k.shape
    tq = _tile(t, 512, HEAD)
    hp = SB_HEADS_PER_STEP
    return pl.pallas_call(
        functools.partial(_sb_prompt_kernel, tq=tq, hp=hp),
        grid_spec=pltpu.PrefetchScalarGridSpec(
            num_scalar_prefetch=1,
            grid=(b, SB_HEADS // hp, t // tq),
            in_specs=[
                pl.BlockSpec((1, tq, hp * HEAD), lambda i, h, j, nb: (i, j, h)),
                pl.BlockSpec((1, hp, t, HEAD), lambda i, h, j, nb: (i, h, 0, 0)),
                pl.BlockSpec((1, hp, t, HEAD), lambda i, h, j, nb: (i, h, 0, 0)),
            ],
            out_specs=pl.BlockSpec((1, tq, hp * HEAD), lambda i, h, j, nb: (i, j, h)),
            scratch_shapes=[pltpu.VMEM((hp, tq, HEAD), F32), pltpu.VMEM((hp, tq, HEAD), F32)],
        ),
        out_shape=jax.ShapeDtypeStruct((b, t, D_RNN), BF16),
        compiler_params=_params("parallel", "parallel", "arbitrary"),
        name="sb_prompt",
    )(-LOG2E * bias.astype(F32), q, k, v)


def _sb_sample_kernel(pt_ref, q_ref, kn_ref, vn_ref, bias_ref, *rest, pages_per_step, t_valid):
    kp_refs = rest[:pages_per_step]
    vp_refs = rest[pages_per_step:2 * pages_per_step]
    o_ref, carry_ref, acc_ref = rest[2 * pages_per_step:]
    g = pl.program_id(1)
    rows = SB_HEADS * Q_ROWS
    suffix = _suffix_matrix()
    neg_bias = bias_ref[...]
    hcols = [slice(h * HEAD, (h + 1) * HEAD) for h in range(SB_HEADS)]
    q_heads = [q_ref[0, :, hcols[h]] for h in range(SB_HEADS)]

    def update(blocks, mask):
        n = len(blocks)
        s_heads = [_nt_dot(q_heads[h], jnp.concatenate([blocks[r][0][h] for r in range(n)], axis=0))[0:Q_ROWS]
                   for h in range(SB_HEADS)]
        nzs, lks = [], []
        for r in range(n):
            s = jnp.concatenate([sh[:, r * HEAD:(r + 1) * HEAD] for sh in s_heads], axis=0)
            nz, lk = _sb_log_keep(s, neg_bias, mask)
            nzs.append(nz)
            lks.append(lk)
        c2 = _sb_suffix_sums(jnp.concatenate(lks, axis=0), suffix)
        carry = carry_ref[...]
        ws = []
        for r in range(n):
            w, carry = _sb_finish(nzs[r], c2[r * rows:(r + 1) * rows], carry, mask)
            ws.append(w)
        carry_ref[...] = carry
        zpad = jnp.zeros((SAMPLE_T_PAD - Q_ROWS, n * HEAD), F32)
        for h in range(SB_HEADS):
            rs = slice(h * Q_ROWS, (h + 1) * Q_ROWS)
            wh = jnp.concatenate([jnp.concatenate([ws[r][rs] for r in range(n)], axis=1), zpad], axis=0)
            vh = jnp.concatenate([blocks[r][1][h] for r in range(n)], axis=0)
            acc_ref[rs, :] += jnp.dot(wh.astype(BF16), vh, preferred_element_type=F32)[0:Q_ROWS]

    @pl.when(g == 0)
    def _():
        carry_ref[...] = jnp.zeros((rows, HEAD), F32)
        acc_ref[...] = jnp.zeros((rows, HEAD), F32)
        kz = jnp.zeros((HEAD - SAMPLE_T_PAD, HEAD), BF16)
        k_heads = [jnp.concatenate([kn_ref[0, :, hcols[h]], kz], axis=0) for h in range(SB_HEADS)]
        v_heads = [jnp.concatenate([vn_ref[0, :, hcols[h]], kz], axis=0) for h in range(SB_HEADS)]
        i_q = lax.broadcasted_iota(jnp.int32, (rows, HEAD), 0) % Q_ROWS
        j_k = lax.broadcasted_iota(jnp.int32, (rows, HEAD), 1)
        update([(k_heads, v_heads)], (j_k < i_q) & (j_k < t_valid))

    update([([kp_refs[r][0, h].astype(BF16) for h in range(SB_HEADS)],
             [vp_refs[r][0, h].astype(BF16) for h in range(SB_HEADS)]) for r in range(pages_per_step)], None)

    @pl.when(g == pl.num_programs(1) - 1)
    def _():
        z8 = jnp.zeros((SAMPLE_T_PAD - Q_ROWS, HEAD), F32)
        for h in range(SB_HEADS):
            blk = jnp.concatenate([acc_ref[h * Q_ROWS:(h + 1) * Q_ROWS, :], z8], axis=0)
            o_ref[0, :, h * HEAD:(h + 1) * HEAD] = blk.astype(o_ref.dtype)


def sb_sample(q, k_new, v_new, cache_k, cache_v, page_table, bias, *, t_valid, pages_per_step=8):
    b, n_pages = page_table.shape
    page = cache_k.shape[2]
    assert page == HEAD and n_pages % pages_per_step == 0 and t_valid <= Q_ROWS
    assert cache_k.shape[1:] == (SB_HEADS, page, HEAD)
    rows = SB_HEADS * Q_ROWS
    bias_rows = jnp.broadcast_to(jnp.repeat(-LOG2E * bias.astype(F32), Q_ROWS)[:, None], (rows, HEAD))

    def page_spec(r):
        return pl.BlockSpec(
            (1, SB_HEADS, page, HEAD),
            lambda i, g, pt: (pt[i, n_pages - 1 - (g * pages_per_step + r)], 0, 0, 0))

    return pl.pallas_call(
        functools.partial(_sb_sample_kernel, pages_per_step=pages_per_step, t_valid=t_valid),
        grid_spec=pltpu.PrefetchScalarGridSpec(
            num_scalar_prefetch=1,
            grid=(b, n_pages // pages_per_step),
            in_specs=[
                pl.BlockSpec((1, SAMPLE_T_PAD, D_RNN), lambda i, g, pt: (i, 0, 0)),
                pl.BlockSpec((1, SAMPLE_T_PAD, D_RNN), lambda i, g, pt: (i, 0, 0)),
                pl.BlockSpec((1, SAMPLE_T_PAD, D_RNN), lambda i, g, pt: (i, 0, 0)),
                pl.BlockSpec((rows, HEAD), lambda i, g, pt: (0, 0)),
            ] + [page_spec(r) for r in range(pages_per_step)] + [page_spec(r) for r in range(pages_per_step)],
            out_specs=pl.BlockSpec((1, SAMPLE_T_PAD, D_RNN), lambda i, g, pt: (i, 0, 0)),
            scratch_shapes=[pltpu.VMEM((rows, HEAD), F32), pltpu.VMEM((rows, HEAD), F32)],
        ),
        out_shape=jax.ShapeDtypeStruct((b, SAMPLE_T_PAD, D_RNN), BF16),
        compiler_params=_params("parallel", "arbitrary"),
        name="sb_sample",
    )(page_table, q, k_new, v_new, bias_rows, *([cache_k] * pages_per_step), *([cache_v] * pages_per_step))


def _trunk(x3, t_valid, rg_h0, rg_conv0, ffn_conv0, mem_k, mem_v, sb_attend, head_major_kv, P, W, emit_w):
    b, t, d = x3.shape
    m = b * t
    x = x3.reshape(m, d)
    rg_h_out, rg_conv_out, ffn_conv_out = [], [], []
    k_f32 = v_f32 = k_bf = v_bf = None
    emitted = {}

    def nmm(key, x, g, **kw):
        if not emit_w:
            kw.setdefault('tn_pref', 2048)
        out = norm_matmul(x, g, W[key], emit_w=emit_w, **kw)
        if not emit_w:
            return out
        *out, emitted[key] = out
        return out[0] if len(out) == 1 else tuple(out)

    def mmr(keys, acts, x, **kw):
        out = matmul_residual([(a, W[k]) for a, k in zip(acts, keys)], x, emit_w=emit_w, **kw)
        if not emit_w:
            return out
        for k, wb in zip(keys, out[1:]):
            emitted[k] = wb
        return out[0]

    for l in range(DEPTH):
        mem_gain = jnp.tile(P['mem_q_norm'][l] * ATTN_SCALE, MEM_HEADS)
        if l < N_A_LAYERS:
            p_rg = nmm(('in_a_rg', l), x, P['g_mix'][l])
            qm = nmm(('in_a_qm', l), x, P['g_mix'][l], head_gain=mem_gain)
            y_tok, h_last, cbuf = rglru(
                p_rg.reshape(b, t, 2 * D_RNN), rg_h0[l], rg_conv0[l], P['rg_conv_w'][l], P['rg_conv_b'][l],
                P['rg_gate_x_w'][l], P['rg_gate_x_b'][l], P['rg_gate_a_w'][l], P['rg_gate_a_b'][l],
                P['rg_lambda'][l], t_valid=t_valid)
            rg_h_out.append(h_last)
            rg_conv_out.append(cbuf)
            q3, q_col = qm.reshape(b, t, D_MEM), 0
        else:
            j = l - N_A_LAYERS
            gain = jnp.concatenate([jnp.tile(P['sb_q_norm'][j] * (ATTN_SCALE * LOG2E), SB_HEADS), mem_gain])
            pq = nmm(('in_b', l), x, P['g_mix'][l], head_gain=gain)
            q3, q_col = pq.reshape(b, t, D_RNN + D_MEM), D_RNN // D_MEM
            y_tok = sb_attend(q3, k_bf, v_bf, P['sb_beta_bias'][j])
        y_mem = mem_attn(q3, q_col, mem_k[l], mem_v[l])
        out_tiles = {} if emit_w else dict(tm_pref=512, tn_pref=2048)
        x = mmr([('out_tok', l), ('out_mem', l)], [y_tok.reshape(m, D_RNN), y_mem.reshape(m, D_MEM)], x,
                **out_tiles)
        act, fbuf, *wbs = ffn_up_act(x, P['g_ffn'][l], W[('ffn_u', l)], W[('ffn_g', l)], ffn_conv0[l],
                                     P['ffn_conv_w'][l], P['ffn_conv_b'][l], seq=t, t_valid=t_valid,
                                     emit_w=emit_w)
        if emit_w:
            emitted[('ffn_u', l)], emitted[('ffn_g', l)] = wbs
        x = mmr([('ffn_down', l)], [act], x, tn_pref=512)
        ffn_conv_out.append(fbuf)
        if l == N_A_LAYERS - 1:
            k_gain = jnp.tile(P['kv_k_norm'], SB_HEADS)
            hm = t if head_major_kv else None
            k_f32, k_bf = nmm(('k', l), x, P['kv_norm'], head_gain=k_gain, out_dtypes=(F32, BF16),
                              head_major_seq=hm)
            v_f32, v_bf = nmm(('v', l), x, P['kv_norm'], out_dtypes=(F32, BF16), head_major_seq=hm)
            if not head_major_kv:
                k_bf = k_bf.reshape(b, t, D_RNN)
                v_bf = v_bf.reshape(b, t, D_RNN)
    if head_major_kv:
        k_out, v_out = k_f32.transpose(0, 2, 1, 3), v_f32.transpose(0, 2, 1, 3)
    else:
        k_out, v_out = k_f32.reshape(b, t, SB_HEADS, HEAD), v_f32.reshape(b, t, SB_HEADS, HEAD)
    return (x.reshape(b, t, d), jnp.stack(rg_h_out), jnp.stack(rg_conv_out), jnp.stack(ffn_conv_out),
            k_out, v_out, emitted)


def kernel(x_prompt, x_sample, mem_prompt, state_rglru_h, state_rglru_conv, state_ffn_conv, cache_mem_k, cache_mem_v, cache_sb_k, cache_sb_v, page_table, g_mix, g_ffn, w_in_a, rg_conv_w, rg_conv_b, rg_gate_x_w, rg_gate_x_b, rg_gate_a_w, rg_gate_a_b, rg_lambda, w_in_b, sb_q_norm, sb_beta_bias, kv_norm, w_kv, kv_k_norm, mem_norm, w_mem_kv, mem_q_norm, mem_k_norm, w_out, w_ffn_up, ffn_conv_w, ffn_conv_b, w_ffn_down):
    P = {'g_mix': g_mix, 'g_ffn': g_ffn, 'rg_conv_w': rg_conv_w, 'rg_conv_b': rg_conv_b,
         'rg_gate_x_b': rg_gate_x_b, 'rg_gate_a_b': rg_gate_a_b, 'rg_lambda': rg_lambda,
         'sb_q_norm': sb_q_norm, 'sb_beta_bias': sb_beta_bias, 'kv_norm': kv_norm,
         'kv_k_norm': kv_k_norm, 'mem_q_norm': mem_q_norm,
         'ffn_conv_w': ffn_conv_w, 'ffn_conv_b': ffn_conv_b,
         'rg_gate_x_w': rg_gate_x_w, 'rg_gate_a_w': rg_gate_a_w}
    dm = D_MODEL
    W = {}
    for l in range(DEPTH):
        if l < N_A_LAYERS:
            W[('in_a_rg', l)] = WeightView(w_in_a, l, 0, dm, 0, 2 * D_RNN)
            W[('in_a_qm', l)] = WeightView(w_in_a, l, 0, dm, 2 * D_RNN, D_MEM)
        else:
            W[('in_b', l)] = WeightView(w_in_b, l - N_A_LAYERS, 0, dm, 0, D_RNN + D_MEM)
        W[('out_tok', l)] = WeightView(w_out, l, 0, D_RNN, 0, dm)
        W[('out_mem', l)] = WeightView(w_out, l, D_RNN, D_MEM, 0, dm)
        W[('ffn_u', l)] = WeightView(w_ffn_up, l, 0, dm, 0, D_FF)
        W[('ffn_g', l)] = WeightView(w_ffn_up, l, 0, dm, D_FF, D_FF)
        W[('ffn_down', l)] = WeightView(w_ffn_down, l, 0, D_FF, 0, dm)
    W[('k', N_A_LAYERS - 1)] = WeightView(w_kv, None, 0, dm, 0, D_RNN)
    W[('v', N_A_LAYERS - 1)] = WeightView(w_kv, None, 0, dm, D_RNN, D_RNN)

    db, dec_seq, _ = x_sample.shape
    xs = jnp.pad(x_sample, ((0, 0), (0, SAMPLE_T_PAD - dec_seq), (0, 0)))

    cache_k_hm = cache_sb_k.transpose(0, 2, 1, 3)
    cache_v_hm = cache_sb_v.transpose(0, 2, 1, 3)

    def sb_paged(q3, k_bf, v_bf, bias):
        return sb_sample(q3, k_bf, v_bf, cache_k_hm, cache_v_hm, page_table, bias, t_valid=dec_seq)

    (y_s, sample_rglru_h, sample_rglru_conv, sample_ffn_conv, s_k, s_v, w_bf16) = _trunk(
        xs, dec_seq, state_rglru_h, state_rglru_conv, state_ffn_conv,
        cache_mem_k.reshape(DEPTH, db, MEM_TOKENS, D_MEM), cache_mem_v.reshape(DEPTH, db, MEM_TOKENS, D_MEM),
        sb_paged, False, P, W, True)
    W_prompt = {key: _bf16_view(arr) for key, arr in w_bf16.items()}

    bp, seq, d = x_prompt.shape
    mem2 = mem_prompt.reshape(bp * MEM_TOKENS, d)
    mk_list, mv_list = [], []
    for l in range(DEPTH):
        mk_list.append(norm_matmul(mem2, mem_norm[l], WeightView(w_mem_kv, l, 0, dm, 0, D_MEM),
                                   head_gain=jnp.tile(mem_k_norm[l], MEM_HEADS), out_dtypes=(F32,)))
        mv_list.append(norm_matmul(mem2, mem_norm[l], WeightView(w_mem_kv, l, 0, dm, D_MEM, D_MEM),
                                   out_dtypes=(F32,)))
    prompt_mem_k = jnp.stack(mk_list).reshape(DEPTH, bp, MEM_TOKENS, D_MEM)
    prompt_mem_v = jnp.stack(mv_list).reshape(DEPTH, bp, MEM_TOKENS, D_MEM)
    zeros_h = jnp.zeros((N_A_LAYERS, bp, D_RNN), F32)
    zeros_rc = jnp.zeros((N_A_LAYERS, bp, RG_CONV - 1, D_RNN), F32)
    zeros_fc = jnp.zeros((DEPTH, bp, FFN_CONV - 1, D_FF), F32)
    (y_prompt, prompt_rglru_h, prompt_rglru_conv, prompt_ffn_conv, prompt_sb_k, prompt_sb_v, _) = _trunk(
        x_prompt, seq, zeros_h, zeros_rc, zeros_fc, prompt_mem_k, prompt_mem_v, sb_prompt, True, P, W_prompt,
        False)

    mem_shape = (DEPTH, bp, MEM_TOKENS, MEM_HEADS, HEAD)
    return (y_prompt, y_s[:, :dec_seq],
            prompt_rglru_h, prompt_rglru_conv, prompt_ffn_conv,
            prompt_sb_k, prompt_sb_v, prompt_mem_k.reshape(mem_shape), prompt_mem_v.reshape(mem_shape),
            sample_rglru_h, sample_rglru_conv, sample_ffn_conv,
            s_k[:, :dec_seq], s_v[:, :dec_seq])
```
